```python
import jax, jax.numpy as jnp
from jax import lax
import numpy as np

D_MODEL = 1024
BATCH = 2
SEQ = 8192
DEPTH = 2

HEAD_DIM = 64
POOL_WINDOWS = (2, 4, 8, 16)
POOL_WIDTH = D_MODEL // 4
POOL_GROUP = POOL_WIDTH // len(POOL_WINDOWS)
RWKV_WIDTH = 3 * D_MODEL // 8
RWKV_HEADS = RWKV_WIDTH // HEAD_DIM
DECAY_LORA = 64
ICLR_LORA = 64
GATE_LORA = 128
VRES_LORA = 32
RWKV_GN_EPS = 64e-5
RWKV_COLS = 3 * RWKV_WIDTH + DECAY_LORA + ICLR_LORA + GATE_LORA
MLSTM_WIDTH = D_MODEL - POOL_WIDTH - RWKV_WIDTH
MLSTM_HEADS = MLSTM_WIDTH // HEAD_DIM
MLSTM_CONV = 4
MLSTM_CHUNK = 64
MLSTM_COLS = 4 * MLSTM_WIDTH + 2 * MLSTM_HEADS
D_MIX = POOL_WIDTH + RWKV_WIDTH + MLSTM_WIDTH
D_IN = POOL_WIDTH + RWKV_COLS + MLSTM_COLS
N_EXPERTS = 32
TOP_K = 4
D_FF = D_MODEL
SWIGLU_LIMIT = 7.0
SWIGLU_ALPHA = 1.702
MOE_BLOCK = 256
LN_EPS = 1e-5
DEEPNORM_ALPHA = (2 * DEPTH) ** 0.25
DEEPNORM_BETA = (8 * DEPTH) ** -0.25

kernel_name = 'hybrid_pool_rwkv7_mlstm_moe_deepnorm'

F32 = jnp.float32


def layer_norm(x, g, b):
    xf = x.astype(F32)
    mu = xf.mean(-1, keepdims=True)
    var = jnp.square(xf - mu).mean(-1, keepdims=True)
    return ((xf - mu) * lax.rsqrt(var + LN_EPS) * g + b).astype(x.dtype)


def token_shift(x):
    return jnp.pad(x, ((0, 0), (1, 0), (0, 0)))[:, :x.shape[1]]


def causal_depthwise_conv(x, w):
    k_taps, s = w.shape[0], x.shape[1]
    xp = jnp.pad(x, ((0, 0), (k_taps - 1, 0), (0, 0)))
    return sum(xp[:, i:i + s] * w[i] for i in range(k_taps))


def pool_mixer(u, pool_w, pool_scale):
    b, s, _ = u.shape
    uf = u.astype(F32)
    cs = jnp.pad(jnp.cumsum(uf, axis=1), ((0, 0), (1, 0), (0, 0)))
    t = jnp.arange(s)
    outs = []
    for gi, win in enumerate(POOL_WINDOWS):
        sl = slice(gi * POOL_GROUP, (gi + 1) * POOL_GROUP)
        lo = jnp.maximum(t + 1 - win, 0)
        cnt = jnp.minimum(t + 1, win).astype(F32)
        outs.append((cs[:, 1:, sl] - cs[:, lo, sl]) / cnt[None, :, None] - uf[..., sl])
    d = jnp.stack(outs, axis=2)
    y = jnp.einsum('bsgc,gcd->bsgd', d, pool_w.astype(F32)).reshape(b, s, POOL_WIDTH)
    return (y * pool_scale).astype(u.dtype)


def rwkv7_recurrence(r, decay, k, v, a_vec, b_vec):
    b, s, h, n = r.shape

    def step(state, inp):
        r_t, w_t, k_t, v_t, a_t, b_t = inp
        sa = jnp.einsum('bhvk,bhk->bhv', state, a_t)
        state = state * w_t[:, :, None, :] + sa[..., None] * b_t[:, :, None, :] + v_t[..., None] * k_t[:, :, None, :]
        return state, jnp.einsum('bhvk,bhk->bhv', state, r_t)

    xs = tuple(jnp.moveaxis(z, 1, 0) for z in (r, decay, k, v, a_vec, b_vec))
    _, ys = lax.scan(step, jnp.zeros((b, h, n, n), F32), xs)
    return jnp.moveaxis(ys, 0, 1)


def rwkv7_mixer(p, mu, w0, w2, a0, a2, g2, kk_scale, ka, rk, lnx_g, lnx_b, v_first, v_gate):
    b, s, _ = p.shape
    pf = p.astype(F32)
    pf = pf + mu * (token_shift(pf) - pf)
    W = RWKV_WIDTH
    r, k, v, wd, ad, gd = jnp.split(pf, [W, 2 * W, 3 * W, 3 * W + DECAY_LORA, 3 * W + DECAY_LORA + ICLR_LORA], axis=-1)
    w_log = -jax.nn.softplus(-(w0 + jnp.tanh(wd) @ w2)) - 0.5
    decay = jnp.exp(-jnp.exp(w_log))
    a = jax.nn.sigmoid(a0 + ad @ a2)
    g = jax.nn.sigmoid(gd) @ g2
    if v_first is None:
        v_first = v
    else:
        v = v + (v_first - v) * v_gate

    def heads(z):
        return z.reshape(b, s, RWKV_HEADS, HEAD_DIM)

    kk = heads(k * kk_scale)
    kk = kk / jnp.maximum(jnp.sqrt(jnp.sum(kk * kk, -1, keepdims=True)), 1e-12)
    k = k * (1.0 + (a - 1.0) * ka)
    rh, kh, vh = heads(r), heads(k), heads(v)
    y = rwkv7_recurrence(rh, heads(decay), kh, vh, -kk, kk * heads(a))
    ym = y.mean(-1, keepdims=True)
    yv = jnp.square(y - ym).mean(-1, keepdims=True)
    y = ((y - ym) * lax.rsqrt(yv + RWKV_GN_EPS)).reshape(b, s, W) * lnx_g + lnx_b
    bonus = jnp.sum(rh * kh * rk, -1, keepdims=True) * vh
    y = y + bonus.reshape(b, s, W)
    return (y * g).astype(p.dtype), v_first


def mlstm_mixer(p, conv_w, conv_b, b_i, b_f, norm_g):
    b, s, _ = p.shape
    H, Dh, L = MLSTM_HEADS, HEAD_DIM, MLSTM_CHUNK
    nc = s // L
    W = MLSTM_WIDTH
    pf = p.astype(F32)
    qk, v, o, ig, fg = jnp.split(pf, [2 * W, 3 * W, 4 * W, 4 * W + H], axis=-1)
    qk = jax.nn.silu(causal_depthwise_conv(qk, conv_w) + conv_b)
    q, k = qk[..., :W], qk[..., W:]

    def chunks(z):
        return z.reshape(b, nc, L, H, Dh).transpose(0, 3, 1, 2, 4)

    def gchunks(z):
        return z.reshape(b, nc, L, H).transpose(0, 3, 1, 2)

    q = chunks(q) * (Dh ** -0.5)
    k = chunks(k)
    v = chunks(v)
    ig = gchunks(ig + b_i)
    g = jnp.cumsum(jax.nn.log_sigmoid(gchunks(fg + b_f)), axis=-1)
    g_last = g[..., -1]
    e = g_last[..., None] - g + ig
    m_loc = e.max(-1)
    w_e = jnp.exp(e - m_loc[..., None])
    kv_loc = jnp.einsum('bhcs,bhcsk,bhcsv->bhckv', w_e, k, v)
    n_loc = jnp.einsum('bhcs,bhcsk->bhck', w_e, k)

    def step(carry, inp):
        c_st, n_st, m_st = carry
        gl, ml, kvl, nl = inp
        m_new = jnp.maximum(gl + m_st, ml)
        a_old = jnp.exp(gl + m_st - m_new)
        a_new = jnp.exp(ml - m_new)
        c_new = a_old[..., None, None] * c_st + a_new[..., None, None] * kvl
        n_new = a_old[..., None] * n_st + a_new[..., None] * nl
        return (c_new, n_new, m_new), (c_st, n_st, m_st)

    init = (jnp.zeros((b, H, Dh, Dh), F32), jnp.zeros((b, H, Dh), F32), jnp.zeros((b, H), F32))
    xs = (jnp.moveaxis(g_last, 2, 0), jnp.moveaxis(m_loc, 2, 0), jnp.moveaxis(kv_loc, 2, 0), jnp.moveaxis(n_loc, 2, 0))
    _, (c_prev, n_prev, m_prev) = lax.scan(step, init, xs)
    c_prev = jnp.moveaxis(c_prev, 0, 2)
    n_prev = jnp.moveaxis(n_prev, 0, 2)
    m_prev = jnp.moveaxis(m_prev, 0, 2)
    causal = jnp.tril(jnp.ones((L, L), bool))
    d_log = jnp.where(causal, g[..., :, None] - g[..., None, :] + ig[..., None, :], -jnp.inf)
    inter_log = g + m_prev[..., None]
    m_j = jnp.maximum(d_log.max(-1), inter_log)
    w_intra = jnp.exp(d_log - m_j[..., None]) * jnp.einsum('bhcjd,bhcsd->bhcjs', q, k)
    w_inter = jnp.exp(inter_log - m_j)
    num = jnp.einsum('bhcjs,bhcsv->bhcjv', w_intra, v) + w_inter[..., None] * jnp.einsum('bhcjd,bhcdv->bhcjv', q, c_prev)
    den = w_intra.sum(-1) + w_inter * jnp.einsum('bhcjd,bhcd->bhcj', q, n_prev)
    h = num / jnp.maximum(jnp.abs(den), jnp.exp(-m_j))[..., None]
    h = h.transpose(0, 2, 3, 1, 4).reshape(b, s, H, Dh)
    hm = h.mean(-1, keepdims=True)
    hv = jnp.square(h - hm).mean(-1, keepdims=True)
    hn = ((h - hm) * lax.rsqrt(hv + LN_EPS)).reshape(b, s, W) * norm_g
    return (jax.nn.sigmoid(o) * hn).astype(p.dtype)


def moe_ffn(h, router_w, router_b, w_gu, b_gu, w_dn, b_dn):
    bsz, s, d = h.shape
    xt = h.reshape(-1, d)
    t = xt.shape[0]
    logits = (xt @ router_w + router_b).astype(F32)
    top_val, top_idx = lax.top_k(logits, TOP_K)
    gate = jax.nn.softmax(top_val, axis=-1)
    flat_e = top_idx.reshape(-1)
    order = jnp.argsort(flat_e)
    e_sorted = flat_e[order]
    tok_sorted = order // TOP_K
    counts = jnp.bincount(flat_e, length=N_EXPERTS)
    padded = (counts + MOE_BLOCK - 1) // MOE_BLOCK * MOE_BLOCK
    pad_end = jnp.cumsum(padded)
    pad_start = pad_end - padded
    start = jnp.cumsum(counts) - counts
    dest = pad_start[e_sorted] + jnp.arange(t * TOP_K) - start[e_sorted]
    n_blocks = -(-(t * TOP_K) // MOE_BLOCK) + N_EXPERTS
    buf = jnp.zeros((n_blocks * MOE_BLOCK, d), h.dtype).at[dest].set(xt[tok_sorted])
    block_e = jnp.minimum(jnp.searchsorted(pad_end, jnp.arange(n_blocks) * MOE_BLOCK, side='right'), N_EXPERTS - 1)

    def expert_block(args):
        xb, e = args
        gu = xb @ w_gu[e] + b_gu[e]
        glu = jnp.minimum(gu[:, :D_FF], SWIGLU_LIMIT)
        lin = jnp.clip(gu[:, D_FF:], -SWIGLU_LIMIT, SWIGLU_LIMIT)
        act = glu * jax.nn.sigmoid(SWIGLU_ALPHA * glu) * (lin + 1.0)
        return act @ w_dn[e] + b_dn[e]

    out = lax.map(expert_block, (buf.reshape(n_blocks, MOE_BLOCK, d), block_e)).reshape(-1, d)
    y_sorted = out[dest] * gate.reshape(-1)[order][:, None].astype(h.dtype)
    y = jnp.zeros_like(xt).at[tok_sorted].add(y_sorted)
    return y.reshape(bsz, s, d)


def setup_inputs(seed: int = 0) -> dict:
    key = jax.random.key(seed)
    ks = iter(jax.random.split(key, 48))
    Lr = DEPTH

    def nrm(shape, scale):
        return jax.random.normal(next(ks), shape, F32) * scale

    W, H = RWKV_WIDTH, RWKV_HEADS
    return {
        'x': nrm((BATCH, SEQ, D_MODEL), 1.0),
        'w_in': nrm((Lr, D_MODEL, D_IN), D_MODEL ** -0.5),
        'pool_w': nrm((Lr, len(POOL_WINDOWS), POOL_GROUP, POOL_GROUP), POOL_GROUP ** -0.5),
        'pool_scale': 1.0 + nrm((Lr, POOL_WIDTH), 0.1),
        'rwkv_mu': jax.random.uniform(next(ks), (Lr, RWKV_COLS), F32),
        'rwkv_w0': jnp.linspace(-6.0, -1.0, W, dtype=F32)[None] + nrm((Lr, W), 0.1),
        'rwkv_w2': nrm((Lr, DECAY_LORA, W), 0.1),
        'rwkv_a0': nrm((Lr, W), 0.1),
        'rwkv_a2': nrm((Lr, ICLR_LORA, W), 0.1),
        'rwkv_g2': nrm((Lr, GATE_LORA, W), GATE_LORA ** -0.5),
        'rwkv_kk_scale': 0.85 + nrm((Lr, W), 0.05),
        'rwkv_ka': 1.0 + nrm((Lr, W), 0.05),
        'rwkv_rk': nrm((Lr, H, HEAD_DIM), 0.1),
        'rwkv_lnx_g': 1.0 + nrm((Lr, W), 0.1),
        'rwkv_lnx_b': nrm((Lr, W), 0.01),
        'rwkv_v0': 1.0 + nrm((Lr - 1, W), 0.1),
        'rwkv_v1': nrm((Lr - 1, D_MODEL, VRES_LORA), D_MODEL ** -0.5),
        'rwkv_v2': nrm((Lr - 1, VRES_LORA, W), VRES_LORA ** -0.5),
        'mlstm_conv_w': nrm((Lr, MLSTM_CONV, 2 * MLSTM_WIDTH), MLSTM_CONV ** -0.5),
        'mlstm_conv_b': nrm((Lr, 2 * MLSTM_WIDTH), 0.01),
        'mlstm_b_i': nrm((Lr, MLSTM_HEADS), 0.1),
        'mlstm_b_f': jnp.linspace(3.0, 6.0, MLSTM_HEADS, dtype=F32)[None] + nrm((Lr, MLSTM_HEADS), 0.1),
        'mlstm_norm_g': 1.0 + nrm((Lr, MLSTM_WIDTH), 0.1),
        'w_out': nrm((Lr, D_MIX, D_MODEL), D_MIX ** -0.5 * DEEPNORM_BETA),
        'ln1_g': 1.0 + nrm((Lr, D_MODEL), 0.1),
        'ln1_b': nrm((Lr, D_MODEL), 0.01),
        'router_w': nrm((Lr, D_MODEL, N_EXPERTS), D_MODEL ** -0.5),
        'router_b': nrm((Lr, N_EXPERTS), 0.01),
        'w_gate_up': nrm((Lr, N_EXPERTS, D_MODEL, 2 * D_FF), D_MODEL ** -0.5),
        'b_gate_up': nrm((Lr, N_EXPERTS, 2 * D_FF), 0.01),
        'w_down': nrm((Lr, N_EXPERTS, D_FF, D_MODEL), D_FF ** -0.5 * DEEPNORM_BETA),
        'b_down': nrm((Lr, N_EXPERTS, D_MODEL), 0.01),
        'ln2_g': 1.0 + nrm((Lr, D_MODEL), 0.1),
        'ln2_b': nrm((Lr, D_MODEL), 0.01),
    }


def reference(x, w_in, pool_w, pool_scale, rwkv_mu, rwkv_w0, rwkv_w2, rwkv_a0, rwkv_a2, rwkv_g2,
              rwkv_kk_scale, rwkv_ka, rwkv_rk, rwkv_lnx_g, rwkv_lnx_b, rwkv_v0, rwkv_v1, rwkv_v2,
              mlstm_conv_w, mlstm_conv_b, mlstm_b_i, mlstm_b_f, mlstm_norm_g, w_out, ln1_g, ln1_b,
              router_w, router_b, w_gate_up, b_gate_up, w_down, b_down, ln2_g, ln2_b):
    h = x
    v_first = None
    for l in range(DEPTH):
        p = h @ w_in[l]
        p_pool = p[..., :POOL_WIDTH]
        p_rwkv = p[..., POOL_WIDTH:POOL_WIDTH + RWKV_COLS]
        p_mlstm = p[..., POOL_WIDTH + RWKV_COLS:]
        y_pool = pool_mixer(p_pool, pool_w[l], pool_scale[l])
        if l == 0:
            v_gate = None
        else:
            v_gate = jax.nn.sigmoid(rwkv_v0[l - 1] + (h @ rwkv_v1[l - 1]) @ rwkv_v2[l - 1]).astype(F32)
        y_rwkv, v_first = rwkv7_mixer(p_rwkv, rwkv_mu[l], rwkv_w0[l], rwkv_w2[l], rwkv_a0[l], rwkv_a2[l],
                                      rwkv_g2[l], rwkv_kk_scale[l], rwkv_ka[l], rwkv_rk[l],
                                      rwkv_lnx_g[l], rwkv_lnx_b[l], v_first, v_gate)
        y_mlstm = mlstm_mixer(p_mlstm, mlstm_conv_w[l], mlstm_conv_b[l], mlstm_b_i[l], mlstm_b_f[l], mlstm_norm_g[l])
        mix = jnp.concatenate([y_pool, y_rwkv, y_mlstm], axis=-1) @ w_out[l]
        h = layer_norm(DEEPNORM_ALPHA * h + mix, ln1_g[l], ln1_b[l])
        ffn = moe_ffn(h, router_w[l], router_b[l], w_gate_up[l], b_gate_up[l], w_down[l], b_down[l])
        h = layer_norm(DEEPNORM_ALPHA * h + ffn, ln2_g[l], ln2_b[l])
    return h
```

```python
import functools

import jax
import jax.numpy as jnp
from jax import lax
from jax.experimental import pallas as pl
from jax.experimental.pallas import tpu as pltpu

F32 = jnp.float32
BF16 = jnp.bfloat16
I32 = jnp.int32

D_MODEL = 1024
HEAD_DIM = 64
POOL_WINDOWS = (2, 4, 8, 16)
POOL_WIDTH = 256
POOL_GROUP = 64
RWKV_WIDTH = 384
RWKV_HEADS = 6
DECAY_LORA = 64
ICLR_LORA = 64
GATE_LORA = 128
VRES_LORA = 32
RWKV_GN_EPS = 64e-5
RWKV_COLS = 3 * RWKV_WIDTH + DECAY_LORA + ICLR_LORA + GATE_LORA
MLSTM_WIDTH = 384
MLSTM_HEADS = 6
MLSTM_CONV = 4
MLSTM_COLS = 4 * MLSTM_WIDTH + 2 * MLSTM_HEADS
D_IN = POOL_WIDTH + RWKV_COLS + MLSTM_COLS
N_EXPERTS = 32
TOP_K = 4
D_FF = D_MODEL
SWIGLU_LIMIT = 7.0
SWIGLU_ALPHA = 1.702
LN_EPS = 1e-5
DEPTH = 2
DEEPNORM_ALPHA = (2 * DEPTH) ** 0.25

LANES = 128
D_IN_PAD = 3328
RWKV_OFF = POOL_WIDTH
MLSTM_OFF = POOL_WIDTH + RWKV_COLS
GATE_OFF = MLSTM_OFF + 4 * MLSTM_WIDTH
VRES_OFF = 2 * MLSTM_HEADS
HALO = 16
CHUNK = 64

PREP_ROWS = 256
SEQ_ROWS = 256
EXPERT_ROWS = 256
VMEM_LIMIT = 48 * 1024 * 1024


def _dot(a, b):
    return jnp.dot(a.astype(BF16), b.astype(BF16), preferred_element_type=F32)


def _dot_nt(a, b):
    return lax.dot_general(a.astype(BF16), b.astype(BF16), (((1,), (1,)), ((), ())),
                           preferred_element_type=F32)


def _dot_tn(a, b):
    return lax.dot_general(a.astype(BF16), b.astype(BF16), (((0,), (0,)), ((), ())),
                           preferred_element_type=F32)


def _split(x):
    hi = x.astype(BF16)
    lo = (x - hi.astype(F32)).astype(BF16)
    return hi, lo


def _dot_lhs2(a, b_bf16):
    hi, lo = _split(a)
    return (jnp.dot(hi, b_bf16, preferred_element_type=F32)
            + jnp.dot(lo, b_bf16, preferred_element_type=F32))


def _dot_rhs2(a_bf16, b):
    hi, lo = _split(b)
    return (jnp.dot(a_bf16, hi, preferred_element_type=F32)
            + jnp.dot(a_bf16, lo, preferred_element_type=F32))


def _dot3(a, b):
    ah, al = _split(a)
    bh, bl = _split(b)
    return (jnp.dot(ah, bh, preferred_element_type=F32)
            + jnp.dot(ah, bl, preferred_element_type=F32)
            + jnp.dot(al, bh, preferred_element_type=F32))


def _sigmoid(x):
    return 1.0 / (1.0 + jnp.exp(-x))


def _softplus(x):
    return jnp.maximum(x, 0.0) + jnp.log(1.0 + jnp.exp(-jnp.abs(x)))


def _head_norm(y, ones_bd, eps):
    inv = 1.0 / HEAD_DIM
    mean = _dot_lhs2(y, ones_bd) * inv
    d = y - mean
    var = _dot_lhs2(d * d, ones_bd) * inv
    return d * lax.rsqrt(var + eps)


def _prep_kernel(x_ref, w_ref, poolw_ref, pscale_ref, mu_ref, w0_ref, w2_ref, a0_ref, a2_ref, g2_ref,
                 kks_ref, ka_ref, rk_ref, v0_ref, v2_ref, vfirst_ref, cw_ref, cb_ref, gbias_ref, ones_ref,
                 ypool_ref, r_ref, ld_ref, k_ref, v_ref, kk_ref, b_ref, g_ref, bonus_ref,
                 mq_ref, mk_ref, mv_ref, mo_ref, mg_ref,
                 p_scr, *, has_vres):
    i = pl.program_id(1)
    ts = x_ref.shape[0]

    @pl.when(i == 0)
    def _():
        p_scr[0:HALO, :] = jnp.zeros((HALO, D_IN_PAD), F32)

    @pl.when(i > 0)
    def _():
        p_scr[0:HALO, :] = p_scr[ts:ts + HALO, :]

    p_scr[HALO:HALO + ts, :] = jnp.dot(x_ref[...].astype(BF16), w_ref[...], preferred_element_type=F32)

    def rows(shift, c0, c1):
        return p_scr[HALO - shift:HALO - shift + ts, c0:c1]

    u = rows(0, 0, POOL_WIDTH)
    acc = u
    sums = {}
    for s in range(1, POOL_WINDOWS[-1]):
        acc = acc + rows(s, 0, POOL_WIDTH)
        if s + 1 in POOL_WINDOWS:
            sums[s + 1] = acc
    pos = (i * ts + lax.broadcasted_iota(I32, (ts, 1), 0) + 1).astype(F32)
    lane = lax.broadcasted_iota(I32, (ts, POOL_WIDTH), 1)
    d = None
    for gi, win in reversed(list(enumerate(POOL_WINDOWS))):
        dg = sums[win] / jnp.minimum(pos, float(win))
        d = dg if d is None else jnp.where(lane < (gi + 1) * POOL_GROUP, dg, d)
    d = d - u
    ypool_ref[...] = _dot(d, poolw_ref[...]) * pscale_ref[...]

    cur = rows(0, RWKV_OFF, RWKV_OFF + RWKV_COLS)
    prev = rows(1, RWKV_OFF, RWKV_OFF + RWKV_COLS)
    pf = cur + mu_ref[...] * (prev - cur)
    W = RWKV_WIDTH
    r = pf[:, 0:W]
    k = pf[:, W:2 * W]
    v = pf[:, 2 * W:3 * W]
    z = pf[:, 3 * W:3 * W + LANES]
    gd = pf[:, 3 * W + LANES:3 * W + 2 * LANES]
    w_log = -_softplus(-(w0_ref[...] + _dot(jnp.tanh(z), w2_ref[...]))) - 0.5
    ld_ref[...] = -jnp.exp(w_log)
    a = _sigmoid(a0_ref[...] + _dot(z, a2_ref[...]))
    g_ref[...] = _dot(_sigmoid(gd), g2_ref[...])
    gates = rows(0, GATE_OFF, GATE_OFF + LANES)
    if has_vres:
        v_gate = _sigmoid(v0_ref[...] + _dot(gates, v2_ref[...]))
        v = v + (vfirst_ref[...] - v) * v_gate
    ones_bd = ones_ref[...]
    kk = k * kks_ref[...]
    kk = kk / jnp.maximum(jnp.sqrt(_dot_lhs2(kk * kk, ones_bd)), 1e-12)
    k = k * (1.0 + (a - 1.0) * ka_ref[...])
    r_ref[...] = r
    k_ref[...] = k
    v_ref[...] = v
    kk_ref[...] = kk
    b_ref[...] = kk * a
    bonus_ref[...] = _dot_lhs2(r * k * rk_ref[...], ones_bd) * v

    qk = cb_ref[...] + rows(0, MLSTM_OFF, MLSTM_OFF + 2 * MLSTM_WIDTH) * cw_ref[MLSTM_CONV - 1:MLSTM_CONV, :]
    for tap in range(MLSTM_CONV - 1):
        shift = MLSTM_CONV - 1 - tap
        qk = qk + rows(shift, MLSTM_OFF, MLSTM_OFF + 2 * MLSTM_WIDTH) * cw_ref[tap:tap + 1, :]
    qk = qk * _sigmoid(qk)
    mq_ref[...] = qk[:, 0:MLSTM_WIDTH] * (HEAD_DIM ** -0.5)
    mk_ref[...] = qk[:, MLSTM_WIDTH:]
    mv_ref[...] = rows(0, MLSTM_OFF + 2 * MLSTM_WIDTH, MLSTM_OFF + 3 * MLSTM_WIDTH)
    mo_ref[...] = _sigmoid(rows(0, MLSTM_OFF + 3 * MLSTM_WIDTH, MLSTM_OFF + 4 * MLSTM_WIDTH))
    gb = gates + gbias_ref[...]
    glane = lax.broadcasted_iota(I32, (ts, LANES), 1)
    mg_ref[...] = jnp.where(glane < MLSTM_HEADS, gb, -_softplus(-gb))


def _prep_call(x, lw, vfirst, *, has_vres):
    B, S, _ = x.shape
    ts = PREP_ROWS
    grid = (B, S // ts)
    row3 = lambda c: pl.BlockSpec((None, ts, c), lambda b, i: (b, i, 0))
    full = lambda a: pl.BlockSpec(a.shape, lambda b, i: (0,) * a.ndim)
    params = [lw['w_in'], lw['pool_w'], lw['pool_scale'], lw['mu'], lw['w0'], lw['w2'], lw['a0'], lw['a2'],
              lw['g2'], lw['kk_scale'], lw['ka'], lw['rk'], lw['v0'], lw['v2']]
    tail = [lw['conv_w'], lw['conv_b'], lw['gate_bias'], lw['ones_bd']]
    in_specs = ([row3(D_MODEL)] + [full(a) for a in params] + [row3(RWKV_WIDTH)] + [full(a) for a in tail])
    widths = [POOL_WIDTH] + [RWKV_WIDTH] * 8 + [MLSTM_WIDTH] * 4 + [LANES]
    out_shape = [jax.ShapeDtypeStruct((B, S, c), F32) for c in widths]
    out_specs = [row3(c) for c in widths]
    return pl.pallas_call(
        functools.partial(_prep_kernel, has_vres=has_vres),
        grid=grid, in_specs=in_specs, out_specs=out_specs, out_shape=out_shape,
        scratch_shapes=[pltpu.VMEM((HALO + ts, D_IN_PAD), F32)],
        compiler_params=pltpu.CompilerParams(dimension_semantics=("arbitrary", "arbitrary"),
                                             vmem_limit_bytes=VMEM_LIMIT),
        name="prep",
    )(x, *params, vfirst, *tail)


def _rwkv_kernel(r_ref, ld_ref, k_ref, v_ref, kk_ref, b_ref, y_ref, s_scr):
    c = pl.program_id(1)
    L = CHUNK
    n_chunks = r_ref.shape[0] // L

    @pl.when(c == 0)
    def _():
        s_scr[...] = jnp.zeros(s_scr.shape, F32)

    row = lax.broadcasted_iota(I32, (L, L), 0)
    col = lax.broadcasted_iota(I32, (L, L), 1)
    strict = col < row
    incl = col <= row
    tri = incl.astype(BF16)
    eye = (row == col).astype(F32)

    for ci in range(n_chunks):
        sl = pl.ds(ci * L, L)
        ld = ld_ref[sl, :]
        cin = _dot_rhs2(tri, ld)
        cex = cin - ld
        c_last = cin[L - 1:L, :]
        e_in = jnp.exp(cin)
        e_neg = jnp.exp(-cin)
        e_tail = jnp.exp(c_last - cin)
        g_last = jnp.exp(c_last)
        kk = kk_ref[sl, :]
        bb = b_ref[sl, :]
        kx = k_ref[sl, :]
        a_t = -kk * jnp.exp(cex)
        r_t = r_ref[sl, :] * e_in
        b_t = bb * e_neg
        k_t = kx * e_neg
        b_h = bb * e_tail
        k_h = kx * e_tail
        vv = v_ref[sl, :]
        ys = []
        for h in range(RWKV_HEADS):
            hs = slice(h * HEAD_DIM, (h + 1) * HEAD_DIM)
            A, R, Bm, Km, Bh, Kh, V = a_t[:, hs], r_t[:, hs], b_t[:, hs], k_t[:, hs], b_h[:, hs], k_h[:, hs], vv[:, hs]
            M = _dot_nt(jnp.concatenate([A, R], axis=0), jnp.concatenate([Bm, Km], axis=0))
            m_ab = jnp.where(strict, M[0:L, 0:L], 0.0)
            m_ak = jnp.where(strict, M[0:L, L:2 * L], 0.0)
            m_rb = jnp.where(incl, M[L:2 * L, 0:L], 0.0)
            m_rk = jnp.where(incl, M[L:2 * L, L:2 * L], 0.0)
            T = eye + m_ab
            pw = m_ab
            for _ in range(5):
                pw = _dot3(pw, pw)
                T = T + _dot3(T, pw)
            WU = _dot3(T, jnp.concatenate([A, _dot(m_ak, V)], axis=1))
            GY = _dot(m_rb, WU)
            G = R + GY[:, 0:L]
            Y0 = GY[:, L:2 * L] + _dot(m_rk, V)
            S0 = s_scr[h]
            ys.append(_dot_nt(G, S0) + Y0)
            WB = _dot_tn(WU, Bh)
            P = WB[0:L, :] + eye * g_last[:, hs]
            Q = WB[L:2 * L, :] + _dot_tn(V, Kh)
            s_scr[h] = _dot3(S0, P) + Q
        y_ref[sl, :] = jnp.concatenate(ys, axis=1)


def _rwkv_call(r, ld, k, v, kk, b):
    B, S, W = r.shape
    ts = SEQ_ROWS
    spec = pl.BlockSpec((None, ts, W), lambda bi, c: (bi, c, 0))
    return pl.pallas_call(
        _rwkv_kernel, grid=(B, S // ts), in_specs=[spec] * 6, out_specs=spec,
        out_shape=jax.ShapeDtypeStruct((B, S, W), F32),
        scratch_shapes=[pltpu.VMEM((RWKV_HEADS, HEAD_DIM, HEAD_DIM), F32)],
        compiler_params=pltpu.CompilerParams(dimension_semantics=("arbitrary", "arbitrary"),
                                             vmem_limit_bytes=VMEM_LIMIT),
        name="rwkv",
    )(r, ld, k, v, kk, b)


def _mlstm_kernel(q_ref, k_ref, v_ref, g_ref, h_ref, c_scr, n_scr, m_scr):
    c = pl.program_id(1)
    L = CHUNK
    n_chunks = q_ref.shape[0] // L
    H = MLSTM_HEADS

    @pl.when(c == 0)
    def _():
        c_scr[...] = jnp.zeros(c_scr.shape, F32)
        n_scr[...] = jnp.zeros(n_scr.shape, F32)
        m_scr[...] = jnp.zeros(m_scr.shape, F32)

    row = lax.broadcasted_iota(I32, (L, L), 0)
    col = lax.broadcasted_iota(I32, (L, L), 1)
    incl = col <= row
    tri = incl.astype(BF16)

    for ci in range(n_chunks):
        sl = pl.ds(ci * L, L)
        gates = g_ref[sl, :]
        gcum = _dot_rhs2(tri, gates)
        gates_t = gates.T
        gcum_t = gcum.T
        hs_out = []
        for h in range(H):
            hs = slice(h * HEAD_DIM, (h + 1) * HEAD_DIM)
            q, k, v = q_ref[sl, hs], k_ref[sl, hs], v_ref[sl, hs]
            ig_c = gates[:, h:h + 1]
            g_c = gcum[:, H + h:H + h + 1]
            ig_r = gates_t[h:h + 1, :]
            g_r = gcum_t[H + h:H + h + 1, :]
            g_last = g_c[L - 1:L, :]
            c_prev = c_scr[h]
            n_prev = n_scr[h]
            m_prev = m_scr[h:h + 1, 0:1]
            d_log = jnp.where(incl, g_c - g_r + ig_r, -jnp.inf)
            inter_log = g_c + m_prev
            m_j = jnp.maximum(jnp.max(d_log, axis=-1, keepdims=True), inter_log)
            w_intra = jnp.exp(d_log - m_j) * _dot_nt(q, k)
            w_inter = jnp.exp(inter_log - m_j)
            num = _dot(w_intra, v) + w_inter * _dot(q, c_prev)
            den = jnp.sum(w_intra, axis=-1, keepdims=True) + w_inter * jnp.sum(q * n_prev, axis=-1, keepdims=True)
            hs_out.append(num / jnp.maximum(jnp.abs(den), jnp.exp(-m_j)))
            e = g_last - g_c + ig_c
            m_loc = jnp.max(e, axis=0, keepdims=True)
            wk = k * jnp.exp(e - m_loc)
            kv_loc = _dot_tn(wk, v)
            n_loc = jnp.sum(wk, axis=0, keepdims=True)
            m_new = jnp.maximum(g_last + m_prev, m_loc)
            a_old = jnp.exp(g_last + m_prev - m_new)
            a_new = jnp.exp(m_loc - m_new)
            c_scr[h] = a_old * c_prev + a_new * kv_loc
            n_scr[h] = a_old * n_prev + a_new * n_loc
            m_scr[h:h + 1, :] = jnp.broadcast_to(m_new, (1, LANES))
        h_ref[sl, :] = jnp.concatenate(hs_out, axis=1)


def _mlstm_call(q, k, v, g):
    B, S, W = q.shape
    ts = SEQ_ROWS
    spec = pl.BlockSpec((None, ts, W), lambda bi, c: (bi, c, 0))
    gspec = pl.BlockSpec((None, ts, LANES), lambda bi, c: (bi, c, 0))
    return pl.pallas_call(
        _mlstm_kernel, grid=(B, S // ts), in_specs=[spec, spec, spec, gspec], out_specs=spec,
        out_shape=jax.ShapeDtypeStruct((B, S, W), F32),
        scratch_shapes=[pltpu.VMEM((MLSTM_HEADS, HEAD_DIM, HEAD_DIM), F32),
                        pltpu.VMEM((MLSTM_HEADS, 1, HEAD_DIM), F32),
                        pltpu.VMEM((8, LANES), F32)],
        compiler_params=pltpu.CompilerParams(dimension_semantics=("arbitrary", "arbitrary"),
                                             vmem_limit_bytes=VMEM_LIMIT),
        name="mlstm",
    )(q, k, v, g)


def _layer_norm(z, g, b):
    mu = jnp.mean(z, axis=-1, keepdims=True)
    d = z - mu
    var = jnp.mean(d * d, axis=-1, keepdims=True)
    return d * lax.rsqrt(var + LN_EPS) * g + b


def _post_kernel(h_ref, ypool_ref, yr_ref, bonus_ref, g_ref, hm_ref, mo_ref,
                 lnxg_ref, lnxb_ref, ng_ref, wout_ref, ln1g_ref, ln1b_ref, rwt_ref, rb_ref, ones_ref,
                 h1_ref, gate_ref, route_ref, cnt_ref):
    step = pl.program_id(0)
    ts = h_ref.shape[0]

    @pl.when(step == 0)
    def _():
        cnt_ref[...] = jnp.zeros(cnt_ref.shape, F32)

    ones_bd = ones_ref[...]
    y_rwkv = (_head_norm(yr_ref[...], ones_bd, RWKV_GN_EPS) * lnxg_ref[...] + lnxb_ref[...]
              + bonus_ref[...]) * g_ref[...]
    y_ml = mo_ref[...] * (_head_norm(hm_ref[...], ones_bd, LN_EPS) * ng_ref[...])
    mix = (_dot(ypool_ref[...], wout_ref[0:POOL_WIDTH, :])
           + _dot(y_rwkv, wout_ref[POOL_WIDTH:POOL_WIDTH + RWKV_WIDTH, :])
           + _dot(y_ml, wout_ref[POOL_WIDTH + RWKV_WIDTH:, :]))
    h1 = _layer_norm(DEEPNORM_ALPHA * h_ref[...] + mix, ln1g_ref[...], ln1b_ref[...])
    h1_ref[...] = h1

    hh, hl = _split(h1)
    wh, wl = _split(rwt_ref[...])
    nt = lambda a, b: lax.dot_general(a, b, (((1,), (1,)), ((), ())), preferred_element_type=F32)
    logits = nt(wh, hh) + nt(wh, hl) + nt(wl, hh) + rb_ref[...]
    eidx = lax.broadcasted_iota(I32, (N_EXPERTS, ts), 0)
    vals = logits
    tops, hots, idxs = [], [], []
    for _ in range(TOP_K):
        mx = jnp.max(vals, axis=0, keepdims=True)
        idx = jnp.min(jnp.where(vals == mx, eidx, N_EXPERTS), axis=0, keepdims=True)
        hot = eidx == idx
        vals = jnp.where(hot, -jnp.inf, vals)
        tops.append(mx)
        hots.append(hot)
        idxs.append(idx)
    exps = [jnp.exp(t - tops[0]) for t in tops]
    denom = exps[0] + exps[1] + exps[2] + exps[3]
    gate_rows = [e / denom for e in exps]
    gate_ref[...] = jnp.concatenate(gate_rows + [jnp.zeros((8 - TOP_K, ts), F32)], axis=0)

    any_hot = (hots[0] | hots[1] | hots[2] | hots[3])
    hot_f = any_hot.astype(F32)
    r_i = lax.broadcasted_iota(I32, (ts, ts), 0)
    c_i = lax.broadcasted_iota(I32, (ts, ts), 1)
    before = (r_i < c_i).astype(BF16)
    carry = cnt_ref[:, 0:1]
    cum = jnp.dot(hot_f.astype(BF16), before, preferred_element_type=F32) + carry
    ranks = [jnp.sum(jnp.where(hot, cum, 0.0), axis=0, keepdims=True) for hot in hots]
    route_ref[...] = jnp.concatenate(idxs + [rk.astype(I32) for rk in ranks], axis=0)
    cnt_ref[...] = cnt_ref[...] + jnp.sum(hot_f, axis=1, keepdims=True)


def _post_call(h, ypool, yr, bonus, g, hm, mo, lw):
    T = h.shape[0]
    ts = PREP_ROWS
    row = lambda c: pl.BlockSpec((ts, c), lambda i: (i, 0))
    colb = lambda r: pl.BlockSpec((r, ts), lambda i: (0, i))
    full = lambda a: pl.BlockSpec(a.shape, lambda i: (0,) * a.ndim)
    params = [lw['lnx_g'], lw['lnx_b'], lw['norm_g'], lw['w_out'], lw['ln1_g'], lw['ln1_b'],
              lw['router_wt'], lw['router_b'], lw['ones_bd']]
    in_specs = ([row(D_MODEL), row(POOL_WIDTH)] + [row(RWKV_WIDTH)] * 5 + [full(a) for a in params])
    out_shape = [jax.ShapeDtypeStruct((T, D_MODEL), F32), jax.ShapeDtypeStruct((8, T), F32),
                 jax.ShapeDtypeStruct((8, T), I32), jax.ShapeDtypeStruct((N_EXPERTS, LANES), F32)]
    out_specs = [row(D_MODEL), colb(8), colb(8), pl.BlockSpec((N_EXPERTS, LANES), lambda i: (0, 0))]
    return pl.pallas_call(
        _post_kernel, grid=(T // ts,), in_specs=in_specs, out_specs=out_specs, out_shape=out_shape,
        compiler_params=pltpu.CompilerParams(dimension_semantics=("arbitrary",),
                                             vmem_limit_bytes=VMEM_LIMIT),
        name="post",
    )(h, ypool, yr, bonus, g, hm, mo, *params)


def _row_copy(src_ref, src_row, dst_ref, dst_row, sem):
    return pltpu.make_async_copy(src_ref.at[pl.ds(src_row, 1), :], dst_ref.at[pl.ds(dst_row, 1), :], sem)


def _dispatch_kernel(dest_ref, h_ref, buf_in_ref, buf_ref, sem):
    del buf_in_ref
    ts = h_ref.shape[0]

    def start(r, carry):
        for kslot in range(TOP_K):
            _row_copy(h_ref, r, buf_ref, dest_ref[0, kslot * ts + r], sem).start()
        return carry

    lax.fori_loop(0, ts, start, 0)

    def wait(r, carry):
        for kslot in range(TOP_K):
            _row_copy(h_ref, r, buf_ref, dest_ref[0, kslot * ts + r], sem).wait()
        return carry

    lax.fori_loop(0, ts, wait, 0)


def _dispatch_call(dest_tiles, h1, buf0):
    T = h1.shape[0]
    ts = PREP_ROWS
    return pl.pallas_call(
        _dispatch_kernel, grid=(T // ts,),
        in_specs=[pl.BlockSpec((None, 1, TOP_K * ts), lambda i: (i, 0, 0), memory_space=pltpu.SMEM),
                  pl.BlockSpec((ts, D_MODEL), lambda i: (i, 0)),
                  pl.BlockSpec(memory_space=pl.ANY)],
        out_specs=pl.BlockSpec(memory_space=pl.ANY),
        out_shape=jax.ShapeDtypeStruct(buf0.shape, buf0.dtype),
        scratch_shapes=[pltpu.SemaphoreType.DMA(())],
        input_output_aliases={2: 0},
        compiler_params=pltpu.CompilerParams(dimension_semantics=("arbitrary",),
                                             vmem_limit_bytes=VMEM_LIMIT),
        name="dispatch",
    )(dest_tiles, h1, buf0)


def _expert_kernel(be_ref, nb_ref, x_ref, wgu_ref, bgu_ref, wdn_ref, bdn_ref, o_ref, wgu_scr, wdn_scr):
    j = pl.program_id(0)
    used = j < nb_ref[0]
    changed = jnp.logical_or(j == 0, be_ref[j] != be_ref[jnp.maximum(j - 1, 0)])

    @pl.when(jnp.logical_and(used, changed))
    def _():
        wgu_scr[...] = wgu_ref[...].astype(BF16)
        wdn_scr[...] = wdn_ref[...].astype(BF16)

    @pl.when(used)
    def _():
        gu = jnp.dot(x_ref[...].astype(BF16), wgu_scr[...], preferred_element_type=F32) + bgu_ref[...]
        glu = jnp.minimum(gu[:, 0:D_FF], SWIGLU_LIMIT)
        lin = jnp.clip(gu[:, D_FF:], -SWIGLU_LIMIT, SWIGLU_LIMIT)
        act = glu * _sigmoid(SWIGLU_ALPHA * glu) * (lin + 1.0)
        o_ref[...] = jnp.dot(act.astype(BF16), wdn_scr[...], preferred_element_type=F32) + bdn_ref[...]

    @pl.when(jnp.logical_not(used))
    def _():
        o_ref[...] = jnp.zeros(o_ref.shape, F32)


def _expert_call(block_e, n_used, buf, lw):
    n_rows = buf.shape[0]
    rows = EXPERT_ROWS
    n_blocks = n_rows // rows

    def blk(j, be, nb):
        return jnp.minimum(j, nb[0] - 1)

    grid_spec = pltpu.PrefetchScalarGridSpec(
        num_scalar_prefetch=2, grid=(n_blocks,),
        in_specs=[pl.BlockSpec((rows, D_MODEL), lambda j, be, nb: (blk(j, be, nb), 0)),
                  pl.BlockSpec((None, D_MODEL, 2 * D_FF), lambda j, be, nb: (be[j], 0, 0)),
                  pl.BlockSpec((None, 1, 2 * D_FF), lambda j, be, nb: (be[j], 0, 0)),
                  pl.BlockSpec((None, D_FF, D_MODEL), lambda j, be, nb: (be[j], 0, 0)),
                  pl.BlockSpec((None, 1, D_MODEL), lambda j, be, nb: (be[j], 0, 0))],
        out_specs=pl.BlockSpec((rows, D_MODEL), lambda j, be, nb: (j, 0)),
        scratch_shapes=[pltpu.VMEM((D_MODEL, 2 * D_FF), BF16), pltpu.VMEM((D_FF, D_MODEL), BF16)])
    return pl.pallas_call(
        _expert_kernel, grid_spec=grid_spec,
        out_shape=jax.ShapeDtypeStruct((n_rows, D_MODEL), F32),
        compiler_params=pltpu.CompilerParams(dimension_semantics=("arbitrary",),
                                             vmem_limit_bytes=56 * 1024 * 1024),
        name="expert",
    )(block_e, n_used, buf, lw['w_gate_up'], lw['b_gate_up'], lw['w_down'], lw['b_down'])


def _combine_kernel(dest_ref, h_ref, gate_ref, out_hbm_ref, ln2g_ref, ln2b_ref, o_ref, gbuf, sem):
    ts = h_ref.shape[0]

    def start(r, carry):
        for kslot in range(TOP_K):
            _row_copy(out_hbm_ref, dest_ref[0, kslot * ts + r], gbuf.at[kslot], r, sem).start()
        return carry

    lax.fori_loop(0, ts, start, 0)
    gate_cols = jnp.concatenate([gate_ref[...], jnp.zeros((LANES - 8, ts), F32)], axis=0).T

    def wait(r, carry):
        for kslot in range(TOP_K):
            _row_copy(out_hbm_ref, dest_ref[0, kslot * ts + r], gbuf.at[kslot], r, sem).wait()
        return carry

    lax.fori_loop(0, ts, wait, 0)
    y = gate_cols[:, 0:1] * gbuf[0]
    for kslot in range(1, TOP_K):
        y = y + gate_cols[:, kslot:kslot + 1] * gbuf[kslot]
    o_ref[...] = _layer_norm(DEEPNORM_ALPHA * h_ref[...] + y, ln2g_ref[...], ln2b_ref[...])


def _combine_call(dest_tiles, h1, gate8, out_rows, lw):
    T = h1.shape[0]
    ts = PREP_ROWS
    full = lambda a: pl.BlockSpec(a.shape, lambda i: (0,) * a.ndim)
    return pl.pallas_call(
        _combine_kernel, grid=(T // ts,),
        in_specs=[pl.BlockSpec((None, 1, TOP_K * ts), lambda i: (i, 0, 0), memory_space=pltpu.SMEM),
                  pl.BlockSpec((ts, D_MODEL), lambda i: (i, 0)),
                  pl.BlockSpec((8, ts), lambda i: (0, i)),
                  pl.BlockSpec(memory_space=pl.ANY),
                  full(lw['ln2_g']), full(lw['ln2_b'])],
        out_specs=pl.BlockSpec((ts, D_MODEL), lambda i: (i, 0)),
        out_shape=jax.ShapeDtypeStruct((T, D_MODEL), F32),
        scratch_shapes=[pltpu.VMEM((TOP_K, ts, D_MODEL), F32), pltpu.SemaphoreType.DMA(())],
        compiler_params=pltpu.CompilerParams(dimension_semantics=("arbitrary",),
                                             vmem_limit_bytes=VMEM_LIMIT),
        name="combine",
    )(dest_tiles, h1, gate8, out_rows, lw['ln2_g'], lw['ln2_b'])


def _block_diag_ones(width):
    hid = jnp.arange(width) // HEAD_DIM
    return (hid[:, None] == hid[None, :]).astype(BF16)


def _layer_params(l, w_in, pool_w, pool_scale, rwkv_mu, rwkv_w0, rwkv_w2, rwkv_a0, rwkv_a2, rwkv_g2,
                  rwkv_kk_scale, rwkv_ka, rwkv_rk, rwkv_lnx_g, rwkv_lnx_b, rwkv_v0, rwkv_v1, rwkv_v2,
                  mlstm_conv_w, mlstm_conv_b, mlstm_b_i, mlstm_b_f, mlstm_norm_g, w_out, ln1_g, ln1_b,
                  router_w, router_b, w_gate_up, b_gate_up, w_down, b_down, ln2_g, ln2_b):
    row = lambda a: a.reshape(1, -1).astype(F32)
    pad_cols = D_IN_PAD - D_IN
    if l > 0:
        extra = jnp.concatenate([rwkv_v1[l - 1], jnp.zeros((D_MODEL, pad_cols - VRES_LORA), F32)], axis=1)
        v0 = row(rwkv_v0[l - 1])
        v2 = jnp.zeros((LANES, RWKV_WIDTH), F32).at[VRES_OFF:VRES_OFF + VRES_LORA].set(rwkv_v2[l - 1])
    else:
        extra = jnp.zeros((D_MODEL, pad_cols), F32)
        v0 = jnp.zeros((1, RWKV_WIDTH), F32)
        v2 = jnp.zeros((LANES, RWKV_WIDTH), F32)
    pw = jnp.zeros((POOL_WIDTH, POOL_WIDTH), F32)
    for gi in range(len(POOL_WINDOWS)):
        sl = slice(gi * POOL_GROUP, (gi + 1) * POOL_GROUP)
        pw = pw.at[sl, sl].set(pool_w[l, gi])
    zero_lora = jnp.zeros((DECAY_LORA, RWKV_WIDTH), F32)
    gate_bias = jnp.zeros((1, LANES), F32).at[0, 0:MLSTM_HEADS].set(mlstm_b_i[l])
    gate_bias = gate_bias.at[0, MLSTM_HEADS:2 * MLSTM_HEADS].set(mlstm_b_f[l])
    return {
        'w_in': jnp.concatenate([w_in[l], extra], axis=1).astype(BF16),
        'pool_w': pw.astype(BF16), 'pool_scale': row(pool_scale[l]), 'mu': row(rwkv_mu[l]),
        'w0': row(rwkv_w0[l]), 'w2': jnp.concatenate([rwkv_w2[l], zero_lora], axis=0).astype(BF16),
        'a0': row(rwkv_a0[l]), 'a2': jnp.concatenate([zero_lora, rwkv_a2[l]], axis=0).astype(BF16),
        'g2': rwkv_g2[l].astype(BF16), 'kk_scale': row(rwkv_kk_scale[l]), 'ka': row(rwkv_ka[l]),
        'rk': row(rwkv_rk[l]), 'v0': v0, 'v2': v2.astype(BF16),
        'conv_w': mlstm_conv_w[l], 'conv_b': row(mlstm_conv_b[l]), 'gate_bias': gate_bias,
        'ones_bd': _block_diag_ones(RWKV_WIDTH),
        'lnx_g': row(rwkv_lnx_g[l]), 'lnx_b': row(rwkv_lnx_b[l]), 'norm_g': row(mlstm_norm_g[l]),
        'w_out': w_out[l].astype(BF16), 'ln1_g': row(ln1_g[l]), 'ln1_b': row(ln1_b[l]),
        'router_wt': router_w[l].T, 'router_b': router_b[l].reshape(N_EXPERTS, 1),
        'w_gate_up': w_gate_up[l], 'b_gate_up': b_gate_up[l].reshape(N_EXPERTS, 1, 2 * D_FF),
        'w_down': w_down[l], 'b_down': b_down[l].reshape(N_EXPERTS, 1, D_MODEL),
        'ln2_g': row(ln2_g[l]), 'ln2_b': row(ln2_b[l]),
    }


def _tile_major(a4, ts):
    T = a4.shape[1]
    return a4.reshape(TOP_K, T // ts, ts).transpose(1, 0, 2).reshape(T // ts, 1, TOP_K * ts)


def _layer(h, v_first, lw, *, has_vres):
    B, S, _ = h.shape
    T = B * S
    outs = _prep_call(h, lw, v_first, has_vres=has_vres)
    ypool, r, ld, k, v, kk, b, g, bonus, mq, mk, mv, mo, mg = outs
    yr = _rwkv_call(r, ld, k, v, kk, b)
    hm = _mlstm_call(mq, mk, mv, mg)
    flat = lambda a: a.reshape(T, a.shape[-1])
    h_flat = flat(h)
    h1, gate8, route8, cnt = _post_call(h_flat, flat(ypool), flat(yr), flat(bonus), flat(g), flat(hm),
                                        flat(mo), lw)
    counts = cnt[:, 0].astype(I32)
    rows = EXPERT_ROWS
    padded = (counts + rows - 1) // rows * rows
    pad_end = jnp.cumsum(padded)
    pad_start = pad_end - padded
    n_blocks = T * TOP_K // rows + N_EXPERTS
    block_e = jnp.minimum(jnp.searchsorted(pad_end, jnp.arange(n_blocks, dtype=I32) * rows, side='right'),
                          N_EXPERTS - 1).astype(I32)
    n_used = (pad_end[-1] // rows).astype(I32).reshape(1)
    block_e = jnp.where(jnp.arange(n_blocks) < n_used[0], block_e, block_e[n_used[0] - 1])
    dest = pad_start[route8[0:TOP_K]] + route8[TOP_K:2 * TOP_K]
    dest_tiles = _tile_major(dest, PREP_ROWS)
    buf = _dispatch_call(dest_tiles, h1, jnp.zeros((n_blocks * rows, D_MODEL), F32))
    out_rows = _expert_call(block_e, n_used, buf, lw)
    h2 = _combine_call(dest_tiles, h1, gate8, out_rows, lw)
    return h2.reshape(B, S, D_MODEL), v


def kernel(x, w_in, pool_w, pool_scale, rwkv_mu, rwkv_w0, rwkv_w2, rwkv_a0, rwkv_a2, rwkv_g2, rwkv_kk_scale, rwkv_ka, rwkv_rk, rwkv_lnx_g, rwkv_lnx_b, rwkv_v0, rwkv_v1, rwkv_v2, mlstm_conv_w, mlstm_conv_b, mlstm_b_i, mlstm_b_f, mlstm_norm_g, w_out, ln1_g, ln1_b, router_w, router_b, w_gate_up, b_gate_up, w_down, b_down, ln2_g, ln2_b):
    weights = (w_in, pool_w, pool_scale, rwkv_mu, rwkv_w0, rwkv_w2, rwkv_a0, rwkv_a2, rwkv_g2, rwkv_kk_scale,
               rwkv_ka, rwkv_rk, rwkv_lnx_g, rwkv_lnx_b, rwkv_v0, rwkv_v1, rwkv_v2, mlstm_conv_w, mlstm_conv_b,
               mlstm_b_i, mlstm_b_f, mlstm_norm_g, w_out, ln1_g, ln1_b, router_w, router_b, w_gate_up,
               b_gate_up, w_down, b_down, ln2_g, ln2_b)
    h = x
    v_first = jnp.zeros(x.shape[:2] + (RWKV_WIDTH,), F32)
    for l in range(w_in.shape[0]):
        lw = _layer_params(l, *weights)
        h, v_l = _layer(h, v_first, lw, has_vres=l > 0)
        if l == 0:
            v_first = v_l
    return h
```

```python
import functools

import jax
import jax.numpy as jnp
from jax import lax
from jax.experimental import pallas as pl
from jax.experimental.pallas import tpu as pltpu

F32 = jnp.float32
BF16 = jnp.bfloat16
I32 = jnp.int32

D_MODEL = 1024
HEAD_DIM = 64
POOL_WINDOWS = (2, 4, 8, 16)
POOL_WIDTH = 256
POOL_GROUP = 64
RWKV_WIDTH = 384
RWKV_HEADS = 6
DECAY_LORA = 64
ICLR_LORA = 64
GATE_LORA = 128
VRES_LORA = 32
RWKV_GN_EPS = 64e-5
RWKV_COLS = 3 * RWKV_WIDTH + DECAY_LORA + ICLR_LORA + GATE_LORA
MLSTM_WIDTH = 384
MLSTM_HEADS = 6
MLSTM_CONV = 4
MLSTM_COLS = 4 * MLSTM_WIDTH + 2 * MLSTM_HEADS
D_IN = POOL_WIDTH + RWKV_COLS + MLSTM_COLS
N_EXPERTS = 32
TOP_K = 4
D_FF = D_MODEL
SWIGLU_LIMIT = 7.0
SWIGLU_ALPHA = 1.702
LN_EPS = 1e-5
DEPTH = 2
DEEPNORM_ALPHA = (2 * DEPTH) ** 0.25

LANES = 128
D_IN_PAD = 3328
RWKV_OFF = POOL_WIDTH
MLSTM_OFF = POOL_WIDTH + RWKV_COLS
GATE_OFF = MLSTM_OFF + 4 * MLSTM_WIDTH
VRES_OFF = 2 * MLSTM_HEADS
HALO = 16
CHUNK = 64

PREP_ROWS = 256
SEQ_ROWS = 256
EXPERT_ROWS = 512
VMEM_LIMIT = 48 * 1024 * 1024


def _dot(a, b):
    return jnp.dot(a.astype(BF16), b.astype(BF16), preferred_element_type=F32)


def _dot_nt(a, b):
    return lax.dot_general(a.astype(BF16), b.astype(BF16), (((1,), (1,)), ((), ())),
                           preferred_element_type=F32)


def _dot_tn(a, b):
    return lax.dot_general(a.astype(BF16), b.astype(BF16), (((0,), (0,)), ((), ())),
                           preferred_element_type=F32)


def _split(x):
    hi = x.astype(BF16)
    lo = (x - hi.astype(F32)).astype(BF16)
    return hi, lo


def _dot_lhs2(a, b_bf16):
    hi, lo = _split(a)
    return (jnp.dot(hi, b_bf16, preferred_element_type=F32)
            + jnp.dot(lo, b_bf16, preferred_element_type=F32))


def _dot_rhs2(a_bf16, b):
    hi, lo = _split(b)
    return (jnp.dot(a_bf16, hi, preferred_element_type=F32)
            + jnp.dot(a_bf16, lo, preferred_element_type=F32))


def _dot3(a, b):
    ah, al = _split(a)
    bh, bl = _split(b)
    return (jnp.dot(ah, bh, preferred_element_type=F32)
            + jnp.dot(ah, bl, preferred_element_type=F32)
            + jnp.dot(al, bh, preferred_element_type=F32))


def _sigmoid(x):
    return 1.0 / (1.0 + jnp.exp(-x))


def _softplus(x):
    return jnp.maximum(x, 0.0) + jnp.log(1.0 + jnp.exp(-jnp.abs(x)))


def _head_norm(y, ones_bd, eps):
    inv = 1.0 / HEAD_DIM
    mean = _dot_lhs2(y, ones_bd) * inv
    d = y - mean
    var = _dot_lhs2(d * d, ones_bd) * inv
    return d * lax.rsqrt(var + eps)


def _prep_kernel(x_ref, w_ref, poolw_ref, pscale_ref, mu_ref, w0_ref, w2_ref, a0_ref, a2_ref, g2_ref,
                 kks_ref, ka_ref, rk_ref, v0_ref, v2_ref, vfirst_ref, cw_ref, cb_ref, gbias_ref, ones_ref,
                 ypool_ref, r_ref, ld_ref, k_ref, v_ref, kk_ref, b_ref, g_ref, bonus_ref,
                 mq_ref, mk_ref, mv_ref, mo_ref, mg_ref,
                 p_scr, *, has_vres):
    i = pl.program_id(1)
    ts = x_ref.shape[0]

    @pl.when(i == 0)
    def _():
        p_scr[0:HALO, :] = jnp.zeros((HALO, D_IN_PAD), F32)

    @pl.when(i > 0)
    def _():
        p_scr[0:HALO, :] = p_scr[ts:ts + HALO, :]

    p_scr[HALO:HALO + ts, :] = jnp.dot(x_ref[...].astype(BF16), w_ref[...], preferred_element_type=F32)

    def rows(shift, c0, c1):
        return p_scr[HALO - shift:HALO - shift + ts, c0:c1]

    u = rows(0, 0, POOL_WIDTH)
    acc = u
    sums = {}
    for s in range(1, POOL_WINDOWS[-1]):
        acc = acc + rows(s, 0, POOL_WIDTH)
        if s + 1 in POOL_WINDOWS:
            sums[s + 1] = acc
    pos = (i * ts + lax.broadcasted_iota(I32, (ts, 1), 0) + 1).astype(F32)
    lane = lax.broadcasted_iota(I32, (ts, POOL_WIDTH), 1)
    d = None
    for gi, win in reversed(list(enumerate(POOL_WINDOWS))):
        dg = sums[win] / jnp.minimum(pos, float(win))
        d = dg if d is None else jnp.where(lane < (gi + 1) * POOL_GROUP, dg, d)
    d = d - u
    ypool_ref[...] = _dot(d, poolw_ref[...]) * pscale_ref[...]

    cur = rows(0, RWKV_OFF, RWKV_OFF + RWKV_COLS)
    prev = rows(1, RWKV_OFF, RWKV_OFF + RWKV_COLS)
    pf = cur + mu_ref[...] * (prev - cur)
    W = RWKV_WIDTH
    r = pf[:, 0:W]
    k = pf[:, W:2 * W]
    v = pf[:, 2 * W:3 * W]
    z = pf[:, 3 * W:3 * W + LANES]
    gd = pf[:, 3 * W + LANES:3 * W + 2 * LANES]
    w_log = -_softplus(-(w0_ref[...] + _dot(jnp.tanh(z), w2_ref[...]))) - 0.5
    ld_ref[...] = -jnp.exp(w_log)
    a = _sigmoid(a0_ref[...] + _dot(z, a2_ref[...]))
    g_ref[...] = _dot(_sigmoid(gd), g2_ref[...])
    gates = rows(0, GATE_OFF, GATE_OFF + LANES)
    if has_vres:
        v_gate = _sigmoid(v0_ref[...] + _dot(gates, v2_ref[...]))
        v = v + (vfirst_ref[...] - v) * v_gate
    ones_bd = ones_ref[...]
    kk = k * kks_ref[...]
    kk = kk / jnp.maximum(jnp.sqrt(_dot_lhs2(kk * kk, ones_bd)), 1e-12)
    k = k * (1.0 + (a - 1.0) * ka_ref[...])
    r_ref[...] = r
    k_ref[...] = k
    v_ref[...] = v
    kk_ref[...] = kk
    b_ref[...] = kk * a
    bonus_ref[...] = _dot_lhs2(r * k * rk_ref[...], ones_bd) * v

    qk = cb_ref[...] + rows(0, MLSTM_OFF, MLSTM_OFF + 2 * MLSTM_WIDTH) * cw_ref[MLSTM_CONV - 1:MLSTM_CONV, :]
    for tap in range(MLSTM_CONV - 1):
        shift = MLSTM_CONV - 1 - tap
        qk = qk + rows(shift, MLSTM_OFF, MLSTM_OFF + 2 * MLSTM_WIDTH) * cw_ref[tap:tap + 1, :]
    qk = qk * _sigmoid(qk)
    mq_ref[...] = qk[:, 0:MLSTM_WIDTH] * (HEAD_DIM ** -0.5)
    mk_ref[...] = qk[:, MLSTM_WIDTH:]
    mv_ref[...] = rows(0, MLSTM_OFF + 2 * MLSTM_WIDTH, MLSTM_OFF + 3 * MLSTM_WIDTH)
    mo_ref[...] = _sigmoid(rows(0, MLSTM_OFF + 3 * MLSTM_WIDTH, MLSTM_OFF + 4 * MLSTM_WIDTH))
    gb = gates + gbias_ref[...]
    glane = lax.broadcasted_iota(I32, (ts, LANES), 1)
    mg_ref[...] = jnp.where(glane < MLSTM_HEADS, gb, -_softplus(-gb))


def _prep_call(x, lw, vfirst, *, has_vres):
    B, S, _ = x.shape
    ts = PREP_ROWS
    grid = (B, S // ts)
    row3 = lambda c: pl.BlockSpec((None, ts, c), lambda b, i: (b, i, 0))
    full = lambda a: pl.BlockSpec(a.shape, lambda b, i: (0,) * a.ndim)
    params = [lw['w_in'], lw['pool_w'], lw['pool_scale'], lw['mu'], lw['w0'], lw['w2'], lw['a0'], lw['a2'],
              lw['g2'], lw['kk_scale'], lw['ka'], lw['rk'], lw['v0'], lw['v2']]
    tail = [lw['conv_w'], lw['conv_b'], lw['gate_bias'], lw['ones_bd']]
    in_specs = ([row3(D_MODEL)] + [full(a) for a in params] + [row3(RWKV_WIDTH)] + [full(a) for a in tail])
    widths = [POOL_WIDTH] + [RWKV_WIDTH] * 8 + [MLSTM_WIDTH] * 4 + [LANES]
    out_shape = [jax.ShapeDtypeStruct((B, S, c), F32) for c in widths]
    out_specs = [row3(c) for c in widths]
    return pl.pallas_call(
        functools.partial(_prep_kernel, has_vres=has_vres),
        grid=grid, in_specs=in_specs, out_specs=out_specs, out_shape=out_shape,
        scratch_shapes=[pltpu.VMEM((HALO + ts, D_IN_PAD), F32)],
        compiler_params=pltpu.CompilerParams(dimension_semantics=("arbitrary", "arbitrary"),
                                             vmem_limit_bytes=VMEM_LIMIT),
        name="prep",
    )(x, *params, vfirst, *tail)


def _rwkv_kernel(r_ref, ld_ref, k_ref, v_ref, kk_ref, b_ref, y_ref, s_scr):
    c = pl.program_id(1)
    L = CHUNK
    n_chunks = r_ref.shape[0] // L

    @pl.when(c == 0)
    def _():
        s_scr[...] = jnp.zeros(s_scr.shape, F32)

    row = lax.broadcasted_iota(I32, (L, L), 0)
    col = lax.broadcasted_iota(I32, (L, L), 1)
    strict = col < row
    incl = col <= row
    tri = incl.astype(BF16)
    eye = (row == col).astype(F32)

    units = [(ci, h) for ci in range(n_chunks) for h in range(RWKV_HEADS)]
    per_chunk = []
    for ci in range(n_chunks):
        sl = pl.ds(ci * L, L)
        ld = ld_ref[sl, :]
        cin = _dot_rhs2(tri, ld)
        c_last = cin[L - 1:L, :]
        e_neg = jnp.exp(-cin)
        e_tail = jnp.exp(c_last - cin)
        kk = kk_ref[sl, :]
        bb = b_ref[sl, :]
        kx = k_ref[sl, :]
        per_chunk.append(dict(
            A=-kk * jnp.exp(cin - ld), R=r_ref[sl, :] * jnp.exp(cin), B=bb * e_neg, K=kx * e_neg,
            Bh=bb * e_tail, Kh=kx * e_tail, V=v_ref[sl, :], g_last=jnp.exp(c_last)))

    def part(name, u):
        ci, h = u
        return per_chunk[ci][name][:, h * HEAD_DIM:(h + 1) * HEAD_DIM]

    M = [_dot_nt(jnp.concatenate([part('A', u), part('R', u)], axis=0),
                 jnp.concatenate([part('B', u), part('K', u)], axis=0)) for u in units]
    m_ab = [jnp.where(strict, m[0:L, 0:L], 0.0) for m in M]
    m_ak = [jnp.where(strict, m[0:L, L:2 * L], 0.0) for m in M]
    m_rb = [jnp.where(incl, m[L:2 * L, 0:L], 0.0) for m in M]
    m_rk = [jnp.where(incl, m[L:2 * L, L:2 * L], 0.0) for m in M]
    MV = [_dot(m, part('V', u)) for m, u in zip(m_ak, units)]
    YK = [_dot(m, part('V', u)) for m, u in zip(m_rk, units)]
    T = [eye + m for m in m_ab]
    pw = m_ab
    for _ in range(5):
        pw = [_dot3(p, p) for p in pw]
        T = [t + _dot3(t, p) for t, p in zip(T, pw)]
    WU = [_dot3(t, jnp.concatenate([part('A', u), mv], axis=1)) for t, u, mv in zip(T, units, MV)]
    GY = [_dot(m, wu) for m, wu in zip(m_rb, WU)]
    G = [part('R', u) + gy[:, 0:L] for u, gy in zip(units, GY)]
    Y0 = [gy[:, L:2 * L] + yk for gy, yk in zip(GY, YK)]
    WB = [_dot_tn(wu, part('Bh', u)) for wu, u in zip(WU, units)]
    VK = [_dot_tn(part('V', u), part('Kh', u)) for u in units]
    P = [wb[0:L, :] + eye * part('g_last', u) for wb, u in zip(WB, units)]
    Q = [wb[L:2 * L, :] + vk for wb, vk in zip(WB, VK)]

    state = [s_scr[h] for h in range(RWKV_HEADS)]
    for ci in range(n_chunks):
        base = ci * RWKV_HEADS
        ys = [_dot_nt(G[base + h], state[h]) + Y0[base + h] for h in range(RWKV_HEADS)]
        state = [_dot3(state[h], P[base + h]) + Q[base + h] for h in range(RWKV_HEADS)]
        y_ref[pl.ds(ci * L, L), :] = jnp.concatenate(ys, axis=1)
    for h in range(RWKV_HEADS):
        s_scr[h] = state[h]


def _rwkv_call(r, ld, k, v, kk, b):
    B, S, W = r.shape
    ts = SEQ_ROWS
    spec = pl.BlockSpec((None, ts, W), lambda bi, c: (bi, c, 0))
    return pl.pallas_call(
        _rwkv_kernel, grid=(B, S // ts), in_specs=[spec] * 6, out_specs=spec,
        out_shape=jax.ShapeDtypeStruct((B, S, W), F32),
        scratch_shapes=[pltpu.VMEM((RWKV_HEADS, HEAD_DIM, HEAD_DIM), F32)],
        compiler_params=pltpu.CompilerParams(dimension_semantics=("arbitrary", "arbitrary"),
                                             vmem_limit_bytes=VMEM_LIMIT),
        name="rwkv",
    )(r, ld, k, v, kk, b)


def _mlstm_kernel(q_ref, k_ref, v_ref, g_ref, expand_ref, h_ref, c_scr, n_scr, m_scr):
    c = pl.program_id(1)
    L = CHUNK
    n_chunks = q_ref.shape[0] // L
    H = MLSTM_HEADS

    @pl.when(c == 0)
    def _():
        c_scr[...] = jnp.zeros(c_scr.shape, F32)
        n_scr[...] = jnp.zeros(n_scr.shape, F32)
        m_scr[...] = jnp.zeros(m_scr.shape, F32)

    row = lax.broadcasted_iota(I32, (L, L), 0)
    col = lax.broadcasted_iota(I32, (L, L), 1)
    incl = col <= row
    tri = incl.astype(BF16)

    units = [(ci, h) for ci in range(n_chunks) for h in range(H)]
    ig_rep, g_rep, gates_t, gcum_t = [], [], [], []
    for ci in range(n_chunks):
        gt = g_ref[pl.ds(ci * L, L), :]
        rep = _dot_lhs2(gt, expand_ref[...])
        ig_rep.append(rep[:, 0:MLSTM_WIDTH])
        g_rep.append(_dot_rhs2(tri, rep[:, MLSTM_WIDTH:]))
        gates_t.append(gt.T)
        gcum_t.append(_dot_rhs2(tri, gt).T)

    def part(ref, u):
        ci, h = u
        return ref[pl.ds(ci * L, L), h * HEAD_DIM:(h + 1) * HEAD_DIM]

    def head(x, h):
        return x[:, h * HEAD_DIM:(h + 1) * HEAD_DIM]

    ones = jnp.ones((L, HEAD_DIM), BF16)
    ig_c = [head(ig_rep[ci], h) for ci, h in units]
    g_c = [head(g_rep[ci], h) for ci, h in units]
    g_last = [g[L - 1:L, :] for g in g_c]
    d_log = [jnp.where(incl, g_c[i] - gcum_t[ci][H + h:H + h + 1, :] + gates_t[ci][h:h + 1, :], -jnp.inf)
             for i, (ci, h) in enumerate(units)]
    d_max = [jnp.max(d, axis=-1, keepdims=True) for d in d_log]
    qk = [_dot_nt(part(q_ref, u), part(k_ref, u)) for u in units]
    e = [gl - g + ig for gl, g, ig in zip(g_last, g_c, ig_c)]
    m_loc = [jnp.max(x, axis=0, keepdims=True) for x in e]
    wk = [part(k_ref, u) * jnp.exp(x - m) for u, x, m in zip(units, e, m_loc)]
    kv_loc = [_dot_tn(w, part(v_ref, u)) for w, u in zip(wk, units)]
    n_loc = [_dot_tn(w, ones) for w in wk]

    c_st = [c_scr[h] for h in range(H)]
    n_st = [n_scr[h] for h in range(H)]
    m_st = [m_scr[h:h + 1, 0:HEAD_DIM] for h in range(H)]
    c_prev, n_prev, m_prev = [], [], []
    for i, (ci, h) in enumerate(units):
        c_prev.append(c_st[h])
        n_prev.append(n_st[h])
        m_prev.append(m_st[h])
        m_new = jnp.maximum(g_last[i] + m_st[h], m_loc[i])
        a_old = jnp.exp(g_last[i] + m_st[h] - m_new)
        a_new = jnp.exp(m_loc[i] - m_new)
        c_st[h] = a_old * c_st[h] + a_new * kv_loc[i]
        n_st[h] = a_old * n_st[h] + a_new * n_loc[i]
        m_st[h] = m_new
    for h in range(H):
        c_scr[h] = c_st[h]
        n_scr[h] = n_st[h]
        m_scr[h:h + 1, 0:HEAD_DIM] = m_st[h]

    inter_log = [g + m for g, m in zip(g_c, m_prev)]
    m_j = [jnp.maximum(dm, il) for dm, il in zip(d_max, inter_log)]
    w_intra = [jnp.exp(d - m) * s for d, m, s in zip(d_log, m_j, qk)]
    w_inter = [jnp.exp(il - m) for il, m in zip(inter_log, m_j)]
    intra = [_dot(w, part(v_ref, u)) for w, u in zip(w_intra, units)]
    inter = [_dot(part(q_ref, u), c) for u, c in zip(units, c_prev)]
    den_a = [_dot(w, ones) for w in w_intra]
    den_b = [_dot(part(q_ref, u), n) for u, n in zip(units, n_prev)]
    outs = [(ia + wi * ie) / jnp.maximum(jnp.abs(da + wi * db), jnp.exp(-m))
            for ia, wi, ie, da, db, m in zip(intra, w_inter, inter, den_a, den_b, m_j)]
    for ci in range(n_chunks):
        h_ref[pl.ds(ci * L, L), :] = jnp.concatenate(outs[ci * H:(ci + 1) * H], axis=1)


def _gate_expand_matrix():
    lane = jnp.arange(LANES)[:, None]
    col = jnp.arange(2 * MLSTM_WIDTH)[None, :]
    src = jnp.where(col < MLSTM_WIDTH, col // HEAD_DIM, MLSTM_HEADS + (col - MLSTM_WIDTH) // HEAD_DIM)
    return (lane == src).astype(BF16)


def _mlstm_call(q, k, v, g):
    B, S, W = q.shape
    ts = SEQ_ROWS
    spec = pl.BlockSpec((None, ts, W), lambda bi, c: (bi, c, 0))
    gspec = pl.BlockSpec((None, ts, LANES), lambda bi, c: (bi, c, 0))
    espec = pl.BlockSpec((LANES, 2 * W), lambda bi, c: (0, 0))
    return pl.pallas_call(
        _mlstm_kernel, grid=(B, S // ts), in_specs=[spec, spec, spec, gspec, espec], out_specs=spec,
        out_shape=jax.ShapeDtypeStruct((B, S, W), F32),
        scratch_shapes=[pltpu.VMEM((MLSTM_HEADS, HEAD_DIM, HEAD_DIM), F32),
                        pltpu.VMEM((MLSTM_HEADS, HEAD_DIM, HEAD_DIM), F32),
                        pltpu.VMEM((8, LANES), F32)],
        compiler_params=pltpu.CompilerParams(dimension_semantics=("arbitrary", "arbitrary"),
                                             vmem_limit_bytes=VMEM_LIMIT),
        name="mlstm",
    )(q, k, v, g, _gate_expand_matrix())


def _layer_norm(z, g, b):
    mu = jnp.mean(z, axis=-1, keepdims=True)
    d = z - mu
    var = jnp.mean(d * d, axis=-1, keepdims=True)
    return d * lax.rsqrt(var + LN_EPS) * g + b


def _post_kernel(h_ref, ypool_ref, yr_ref, bonus_ref, g_ref, hm_ref, mo_ref,
                 lnxg_ref, lnxb_ref, ng_ref, wout_ref, ln1g_ref, ln1b_ref, rwt_ref, rb_ref, ones_ref,
                 h1_ref, gate_ref, route_ref, cnt_ref):
    step = pl.program_id(0)
    ts = h_ref.shape[0]

    @pl.when(step == 0)
    def _():
        cnt_ref[...] = jnp.zeros(cnt_ref.shape, F32)

    ones_bd = ones_ref[...]
    y_rwkv = (_head_norm(yr_ref[...], ones_bd, RWKV_GN_EPS) * lnxg_ref[...] + lnxb_ref[...]
              + bonus_ref[...]) * g_ref[...]
    y_ml = mo_ref[...] * (_head_norm(hm_ref[...], ones_bd, LN_EPS) * ng_ref[...])
    mix = (_dot(ypool_ref[...], wout_ref[0:POOL_WIDTH, :])
           + _dot(y_rwkv, wout_ref[POOL_WIDTH:POOL_WIDTH + RWKV_WIDTH, :])
           + _dot(y_ml, wout_ref[POOL_WIDTH + RWKV_WIDTH:, :]))
    h1 = _layer_norm(DEEPNORM_ALPHA * h_ref[...] + mix, ln1g_ref[...], ln1b_ref[...])
    h1_ref[...] = h1

    hh, hl = _split(h1)
    wh, wl = _split(rwt_ref[...])
    nt = lambda a, b: lax.dot_general(a, b, (((1,), (1,)), ((), ())), preferred_element_type=F32)
    logits = nt(wh, hh) + nt(wh, hl) + nt(wl, hh) + rb_ref[...]
    eidx = lax.broadcasted_iota(I32, (N_EXPERTS, ts), 0)
    vals = logits
    tops, hots, idxs = [], [], []
    for _ in range(TOP_K):
        mx = jnp.max(vals, axis=0, keepdims=True)
        idx = jnp.min(jnp.where(vals == mx, eidx, N_EXPERTS), axis=0, keepdims=True)
        hot = eidx == idx
        vals = jnp.where(hot, -jnp.inf, vals)
        tops.append(mx)
        hots.append(hot)
        idxs.append(idx)
    exps = [jnp.exp(t - tops[0]) for t in tops]
    denom = exps[0] + exps[1] + exps[2] + exps[3]
    gate_rows = [e / denom for e in exps]
    gate_ref[...] = jnp.concatenate(gate_rows + [jnp.zeros((8 - TOP_K, ts), F32)], axis=0)

    any_hot = (hots[0] | hots[1] | hots[2] | hots[3])
    hot_f = any_hot.astype(F32)
    r_i = lax.broadcasted_iota(I32, (ts, ts), 0)
    c_i = lax.broadcasted_iota(I32, (ts, ts), 1)
    before = (r_i < c_i).astype(BF16)
    carry = cnt_ref[:, 0:1]
    cum = jnp.dot(hot_f.astype(BF16), before, preferred_element_type=F32) + carry
    ranks = [jnp.sum(jnp.where(hot, cum, 0.0), axis=0, keepdims=True) for hot in hots]
    route_ref[...] = jnp.concatenate(idxs + [rk.astype(I32) for rk in ranks], axis=0)
    cnt_ref[...] = cnt_ref[...] + jnp.sum(hot_f, axis=1, keepdims=True)


def _post_call(h, ypool, yr, bonus, g, hm, mo, lw):
    T = h.shape[0]
    ts = PREP_ROWS
    row = lambda c: pl.BlockSpec((ts, c), lambda i: (i, 0))
    colb = lambda r: pl.BlockSpec((r, ts), lambda i: (0, i))
    full = lambda a: pl.BlockSpec(a.shape, lambda i: (0,) * a.ndim)
    params = [lw['lnx_g'], lw['lnx_b'], lw['norm_g'], lw['w_out'], lw['ln1_g'], lw['ln1_b'],
              lw['router_wt'], lw['router_b'], lw['ones_bd']]
    in_specs = ([row(D_MODEL), row(POOL_WIDTH)] + [row(RWKV_WIDTH)] * 5 + [full(a) for a in params])
    out_shape = [jax.ShapeDtypeStruct((T, D_MODEL), F32), jax.ShapeDtypeStruct((8, T), F32),
                 jax.ShapeDtypeStruct((8, T), I32), jax.ShapeDtypeStruct((N_EXPERTS, LANES), F32)]
    out_specs = [row(D_MODEL), colb(8), colb(8), pl.BlockSpec((N_EXPERTS, LANES), lambda i: (0, 0))]
    return pl.pallas_call(
        _post_kernel, grid=(T // ts,), in_specs=in_specs, out_specs=out_specs, out_shape=out_shape,
        compiler_params=pltpu.CompilerParams(dimension_semantics=("arbitrary",),
                                             vmem_limit_bytes=VMEM_LIMIT),
        name="post",
    )(h, ypool, yr, bonus, g, hm, mo, *params)


def _row_copy(src_ref, src_row, dst_ref, dst_row, sem):
    return pltpu.make_async_copy(src_ref.at[pl.ds(src_row, 1), :], dst_ref.at[pl.ds(dst_row, 1), :], sem)


def _dispatch_kernel(dest_ref, h_ref, buf_in_ref, buf_ref, sem):
    del buf_in_ref
    ts = h_ref.shape[0]

    def start(r, carry):
        for kslot in range(TOP_K):
            _row_copy(h_ref, r, buf_ref, dest_ref[0, kslot * ts + r], sem).start()
        return carry

    lax.fori_loop(0, ts, start, 0)

    def wait(r, carry):
        for kslot in range(TOP_K):
            _row_copy(h_ref, r, buf_ref, dest_ref[0, kslot * ts + r], sem).wait()
        return carry

    lax.fori_loop(0, ts, wait, 0)


def _dispatch_call(dest_tiles, h1, buf0):
    T = h1.shape[0]
    ts = PREP_ROWS
    return pl.pallas_call(
        _dispatch_kernel, grid=(T // ts,),
        in_specs=[pl.BlockSpec((None, 1, TOP_K * ts), lambda i: (i, 0, 0), memory_space=pltpu.SMEM),
                  pl.BlockSpec((ts, D_MODEL), lambda i: (i, 0)),
                  pl.BlockSpec(memory_space=pl.ANY)],
        out_specs=pl.BlockSpec(memory_space=pl.ANY),
        out_shape=jax.ShapeDtypeStruct(buf0.shape, buf0.dtype),
        scratch_shapes=[pltpu.SemaphoreType.DMA(())],
        input_output_aliases={2: 0},
        compiler_params=pltpu.CompilerParams(dimension_semantics=("arbitrary",),
                                             vmem_limit_bytes=VMEM_LIMIT),
        name="dispatch",
    )(dest_tiles, h1, buf0)


def _expert_kernel(be_ref, nb_ref, x_ref, wgu_ref, bgu_ref, wdn_ref, bdn_ref, o_ref, wgu_scr, wdn_scr):
    j = pl.program_id(0)
    used = j < nb_ref[0]
    changed = jnp.logical_or(j == 0, be_ref[j] != be_ref[jnp.maximum(j - 1, 0)])

    @pl.when(jnp.logical_and(used, changed))
    def _():
        wgu_scr[...] = wgu_ref[...].astype(BF16)
        wdn_scr[...] = wdn_ref[...].astype(BF16)

    @pl.when(used)
    def _():
        gu = jnp.dot(x_ref[...].astype(BF16), wgu_scr[...], preferred_element_type=F32) + bgu_ref[...]
        glu = jnp.minimum(gu[:, 0:D_FF], SWIGLU_LIMIT)
        lin = jnp.clip(gu[:, D_FF:], -SWIGLU_LIMIT, SWIGLU_LIMIT)
        act = glu * _sigmoid(SWIGLU_ALPHA * glu) * (lin + 1.0)
        o_ref[...] = jnp.dot(act.astype(BF16), wdn_scr[...], preferred_element_type=F32) + bdn_ref[...]

    @pl.when(jnp.logical_not(used))
    def _():
        o_ref[...] = jnp.zeros(o_ref.shape, F32)


def _expert_call(block_e, n_used, buf, lw):
    n_rows = buf.shape[0]
    rows = EXPERT_ROWS
    n_blocks = n_rows // rows

    def blk(j, be, nb):
        return jnp.minimum(j, nb[0] - 1)

    layer = lw['layer']
    grid_spec = pltpu.PrefetchScalarGridSpec(
        num_scalar_prefetch=2, grid=(n_blocks,),
        in_specs=[pl.BlockSpec((rows, D_MODEL), lambda j, be, nb: (blk(j, be, nb), 0)),
                  pl.BlockSpec((None, None, D_MODEL, 2 * D_FF), lambda j, be, nb: (layer, be[j], 0, 0)),
                  pl.BlockSpec((None, None, 1, 2 * D_FF), lambda j, be, nb: (layer, be[j], 0, 0)),
                  pl.BlockSpec((None, None, D_FF, D_MODEL), lambda j, be, nb: (layer, be[j], 0, 0)),
                  pl.BlockSpec((None, None, 1, D_MODEL), lambda j, be, nb: (layer, be[j], 0, 0))],
        out_specs=pl.BlockSpec((rows, D_MODEL), lambda j, be, nb: (j, 0)),
        scratch_shapes=[pltpu.VMEM((D_MODEL, 2 * D_FF), BF16), pltpu.VMEM((D_FF, D_MODEL), BF16)])
    return pl.pallas_call(
        _expert_kernel, grid_spec=grid_spec,
        out_shape=jax.ShapeDtypeStruct((n_rows, D_MODEL), F32),
        compiler_params=pltpu.CompilerParams(dimension_semantics=("arbitrary",),
                                             vmem_limit_bytes=56 * 1024 * 1024),
        name="expert",
    )(block_e, n_used, buf, lw['w_gate_up'], lw['b_gate_up'], lw['w_down'], lw['b_down'])


def _combine_kernel(dest_ref, h_ref, gate_ref, out_hbm_ref, ln2g_ref, ln2b_ref, o_ref, gbuf, sem):
    ts = h_ref.shape[0]

    def start(r, carry):
        for kslot in range(TOP_K):
            _row_copy(out_hbm_ref, dest_ref[0, kslot * ts + r], gbuf.at[kslot], r, sem).start()
        return carry

    lax.fori_loop(0, ts, start, 0)
    gate_cols = jnp.concatenate([gate_ref[...], jnp.zeros((LANES - 8, ts), F32)], axis=0).T

    def wait(r, carry):
        for kslot in range(TOP_K):
            _row_copy(out_hbm_ref, dest_ref[0, kslot * ts + r], gbuf.at[kslot], r, sem).wait()
        return carry

    lax.fori_loop(0, ts, wait, 0)
    y = gate_cols[:, 0:1] * gbuf[0]
    for kslot in range(1, TOP_K):
        y = y + gate_cols[:, kslot:kslot + 1] * gbuf[kslot]
    o_ref[...] = _layer_norm(DEEPNORM_ALPHA * h_ref[...] + y, ln2g_ref[...], ln2b_ref[...])


def _combine_call(dest_tiles, h1, gate8, out_rows, lw):
    T = h1.shape[0]
    ts = PREP_ROWS
    full = lambda a: pl.BlockSpec(a.shape, lambda i: (0,) * a.ndim)
    return pl.pallas_call(
        _combine_kernel, grid=(T // ts,),
        in_specs=[pl.BlockSpec((None, 1, TOP_K * ts), lambda i: (i, 0, 0), memory_space=pltpu.SMEM),
                  pl.BlockSpec((ts, D_MODEL), lambda i: (i, 0)),
                  pl.BlockSpec((8, ts), lambda i: (0, i)),
                  pl.BlockSpec(memory_space=pl.ANY),
                  full(lw['ln2_g']), full(lw['ln2_b'])],
        out_specs=pl.BlockSpec((ts, D_MODEL), lambda i: (i, 0)),
        out_shape=jax.ShapeDtypeStruct((T, D_MODEL), F32),
        scratch_shapes=[pltpu.VMEM((TOP_K, ts, D_MODEL), F32), pltpu.SemaphoreType.DMA(())],
        compiler_params=pltpu.CompilerParams(dimension_semantics=("arbitrary",),
                                             vmem_limit_bytes=VMEM_LIMIT),
        name="combine",
    )(dest_tiles, h1, gate8, out_rows, lw['ln2_g'], lw['ln2_b'])


def _block_diag_ones(width):
    hid = jnp.arange(width) // HEAD_DIM
    return (hid[:, None] == hid[None, :]).astype(BF16)


def _layer_params(l, w_in, pool_w, pool_scale, rwkv_mu, rwkv_w0, rwkv_w2, rwkv_a0, rwkv_a2, rwkv_g2,
                  rwkv_kk_scale, rwkv_ka, rwkv_rk, rwkv_lnx_g, rwkv_lnx_b, rwkv_v0, rwkv_v1, rwkv_v2,
                  mlstm_conv_w, mlstm_conv_b, mlstm_b_i, mlstm_b_f, mlstm_norm_g, w_out, ln1_g, ln1_b,
                  router_w, router_b, w_gate_up, b_gate_up, w_down, b_down, ln2_g, ln2_b):
    row = lambda a: a.reshape(1, -1).astype(F32)
    pad_cols = D_IN_PAD - D_IN
    if l > 0:
        extra = jnp.concatenate([rwkv_v1[l - 1], jnp.zeros((D_MODEL, pad_cols - VRES_LORA), F32)], axis=1)
        v0 = row(rwkv_v0[l - 1])
        v2 = jnp.zeros((LANES, RWKV_WIDTH), F32).at[VRES_OFF:VRES_OFF + VRES_LORA].set(rwkv_v2[l - 1])
    else:
        extra = jnp.zeros((D_MODEL, pad_cols), F32)
        v0 = jnp.zeros((1, RWKV_WIDTH), F32)
        v2 = jnp.zeros((LANES, RWKV_WIDTH), F32)
    pw = jnp.zeros((POOL_WIDTH, POOL_WIDTH), F32)
    for gi in range(len(POOL_WINDOWS)):
        sl = slice(gi * POOL_GROUP, (gi + 1) * POOL_GROUP)
        pw = pw.at[sl, sl].set(pool_w[l, gi])
    zero_lora = jnp.zeros((DECAY_LORA, RWKV_WIDTH), F32)
    gate_bias = jnp.zeros((1, LANES), F32).at[0, 0:MLSTM_HEADS].set(mlstm_b_i[l])
    gate_bias = gate_bias.at[0, MLSTM_HEADS:2 * MLSTM_HEADS].set(mlstm_b_f[l])
    return {
        'w_in': jnp.concatenate([w_in[l], extra], axis=1).astype(BF16),
        'pool_w': pw.astype(BF16), 'pool_scale': row(pool_scale[l]), 'mu': row(rwkv_mu[l]),
        'w0': row(rwkv_w0[l]), 'w2': jnp.concatenate([rwkv_w2[l], zero_lora], axis=0).astype(BF16),
        'a0': row(rwkv_a0[l]), 'a2': jnp.concatenate([zero_lora, rwkv_a2[l]], axis=0).astype(BF16),
        'g2': rwkv_g2[l].astype(BF16), 'kk_scale': row(rwkv_kk_scale[l]), 'ka': row(rwkv_ka[l]),
        'rk': row(rwkv_rk[l]), 'v0': v0, 'v2': v2.astype(BF16),
        'conv_w': mlstm_conv_w[l], 'conv_b': row(mlstm_conv_b[l]), 'gate_bias': gate_bias,
        'ones_bd': _block_diag_ones(RWKV_WIDTH),
        'lnx_g': row(rwkv_lnx_g[l]), 'lnx_b': row(rwkv_lnx_b[l]), 'norm_g': row(mlstm_norm_g[l]),
        'w_out': w_out[l].astype(BF16), 'ln1_g': row(ln1_g[l]), 'ln1_b': row(ln1_b[l]),
        'router_wt': router_w[l].T, 'router_b': router_b[l].reshape(N_EXPERTS, 1),
        'layer': l, 'w_gate_up': w_gate_up, 'b_gate_up': b_gate_up.reshape(-1, N_EXPERTS, 1, 2 * D_FF),
        'w_down': w_down, 'b_down': b_down.reshape(-1, N_EXPERTS, 1, D_MODEL),
        'ln2_g': row(ln2_g[l]), 'ln2_b': row(ln2_b[l]),
    }


def _tile_major(a4, ts):
    T = a4.shape[1]
    return a4.reshape(TOP_K, T // ts, ts).transpose(1, 0, 2).reshape(T // ts, 1, TOP_K * ts)


def _layer(h, v_first, lw, *, has_vres):
    B, S, _ = h.shape
    T = B * S
    outs = _prep_call(h, lw, v_first, has_vres=has_vres)
    ypool, r, ld, k, v, kk, b, g, bonus, mq, mk, mv, mo, mg = outs
    yr = _rwkv_call(r, ld, k, v, kk, b)
    hm = _mlstm_call(mq, mk, mv, mg)
    flat = lambda a: a.reshape(T, a.shape[-1])
    h_flat = flat(h)
    h1, gate8, route8, cnt = _post_call(h_flat, flat(ypool), flat(yr), flat(bonus), flat(g), flat(hm),
                                        flat(mo), lw)
    counts = cnt[:, 0].astype(I32)
    rows = EXPERT_ROWS
    padded = (counts + rows - 1) // rows * rows
    pad_end = jnp.cumsum(padded)
    pad_start = pad_end - padded
    n_blocks = T * TOP_K // rows + N_EXPERTS
    n_used = (pad_end[-1] // rows).astype(I32).reshape(1)
    starts = jnp.minimum(jnp.arange(n_blocks, dtype=I32), n_used[0] - 1) * rows
    block_e = jnp.sum((pad_end[None, :] <= starts[:, None]).astype(I32), axis=1)
    expert_ids = jnp.arange(N_EXPERTS, dtype=I32)
    start_of = jnp.sum(jnp.where(route8[0:TOP_K, :, None] == expert_ids, pad_start, 0), axis=-1)
    dest = start_of + route8[TOP_K:2 * TOP_K]
    dest_tiles = _tile_major(dest, PREP_ROWS)
    buf = _dispatch_call(dest_tiles, h1, jnp.zeros((n_blocks * rows, D_MODEL), F32))
    out_rows = _expert_call(block_e, n_used, buf, lw)
    h2 = _combine_call(dest_tiles, h1, gate8, out_rows, lw)
    return h2.reshape(B, S, D_MODEL), v


def kernel(x, w_in, pool_w, pool_scale, rwkv_mu, rwkv_w0, rwkv_w2, rwkv_a0, rwkv_a2, rwkv_g2, rwkv_kk_scale, rwkv_ka, rwkv_rk, rwkv_lnx_g, rwkv_lnx_b, rwkv_v0, rwkv_v1, rwkv_v2, mlstm_conv_w, mlstm_conv_b, mlstm_b_i, mlstm_b_f, mlstm_norm_g, w_out, ln1_g, ln1_b, router_w, router_b, w_gate_up, b_gate_up, w_down, b_down, ln2_g, ln2_b):
    weights = (w_in, pool_w, pool_scale, rwkv_mu, rwkv_w0, rwkv_w2, rwkv_a0, rwkv_a2, rwkv_g2, rwkv_kk_scale,
               rwkv_ka, rwkv_rk, rwkv_lnx_g, rwkv_lnx_b, rwkv_v0, rwkv_v1, rwkv_v2, mlstm_conv_w, mlstm_conv_b,
               mlstm_b_i, mlstm_b_f, mlstm_norm_g, w_out, ln1_g, ln1_b, router_w, router_b, w_gate_up,
               b_gate_up, w_down, b_down, ln2_g, ln2_b)
    h = x
    v_first = jnp.zeros(x.shape[:2] + (RWKV_WIDTH,), F32)
    for l in range(w_in.shape[0]):
        lw = _layer_params(l, *weights)
        h, v_l = _layer(h, v_first, lw, has_vres=l > 0)
        if l == 0:
            v_first = v_l
    return h
```

```python
import functools

import jax
import jax.numpy as jnp
from jax import lax
from jax.experimental import pallas as pl
from jax.experimental.pallas import tpu as pltpu

F32 = jnp.float32
BF16 = jnp.bfloat16
I32 = jnp.int32

D_MODEL = 1024
HEAD_DIM = 64
POOL_WINDOWS = (2, 4, 8, 16)
POOL_WIDTH = 256
POOL_GROUP = 64
RWKV_WIDTH = 384
RWKV_HEADS = 6
DECAY_LORA = 64
ICLR_LORA = 64
GATE_LORA = 128
VRES_LORA = 32
RWKV_GN_EPS = 64e-5
RWKV_COLS = 3 * RWKV_WIDTH + DECAY_LORA + ICLR_LORA + GATE_LORA
MLSTM_WIDTH = 384
MLSTM_HEADS = 6
MLSTM_CONV = 4
MLSTM_COLS = 4 * MLSTM_WIDTH + 2 * MLSTM_HEADS
D_IN = POOL_WIDTH + RWKV_COLS + MLSTM_COLS
N_EXPERTS = 32
TOP_K = 4
D_FF = D_MODEL
SWIGLU_LIMIT = 7.0
SWIGLU_ALPHA = 1.702
LN_EPS = 1e-5
DEPTH = 2
DEEPNORM_ALPHA = (2 * DEPTH) ** 0.25

LANES = 128
D_IN_PAD = 3328
RWKV_OFF = POOL_WIDTH
MLSTM_OFF = POOL_WIDTH + RWKV_COLS
GATE_OFF = MLSTM_OFF + 4 * MLSTM_WIDTH
VRES_OFF = 2 * MLSTM_HEADS
HALO = 16
CHUNK = 64

PREP_ROWS = 256
SEQ_ROWS = 256
EXPERT_ROWS = 512
BUF_WIDTH = D_MODEL + LANES
SUBLANES = 8
RUN_PIECE = SUBLANES
FILL_PIECE = 64
VMEM_LIMIT = 48 * 1024 * 1024


def _dot(a, b):
    return jnp.dot(a.astype(BF16), b.astype(BF16), preferred_element_type=F32)


def _dot_nt(a, b):
    return lax.dot_general(a.astype(BF16), b.astype(BF16), (((1,), (1,)), ((), ())),
                           preferred_element_type=F32)


def _dot_tn(a, b):
    return lax.dot_general(a.astype(BF16), b.astype(BF16), (((0,), (0,)), ((), ())),
                           preferred_element_type=F32)


def _split(x):
    hi = x.astype(BF16)
    lo = (x - hi.astype(F32)).astype(BF16)
    return hi, lo


def _dot_lhs2(a, b_bf16):
    hi, lo = _split(a)
    return (jnp.dot(hi, b_bf16, preferred_element_type=F32)
            + jnp.dot(lo, b_bf16, preferred_element_type=F32))


def _dot_rhs2(a_bf16, b):
    hi, lo = _split(b)
    return (jnp.dot(a_bf16, hi, preferred_element_type=F32)
            + jnp.dot(a_bf16, lo, preferred_element_type=F32))


def _sigmoid(x):
    return 1.0 / (1.0 + jnp.exp(-x))


def _softplus(x):
    return jnp.maximum(x, 0.0) + jnp.log(1.0 + jnp.exp(-jnp.abs(x)))


def _head_norm(y, ones_bd, eps):
    inv = 1.0 / HEAD_DIM
    mean = _dot_lhs2(y, ones_bd) * inv
    d = y - mean
    var = _dot_lhs2(d * d, ones_bd) * inv
    return d * lax.rsqrt(var + eps)


def _prep_kernel(x_ref, w_ref, poolw_ref, pscale_ref, mu_ref, w0_ref, w2_ref, a0_ref, a2_ref, g2_ref,
                 kks_ref, ka_ref, rk_ref, v0_ref, v2_ref, vfirst_ref, cw_ref, cb_ref, gbias_ref, ones_ref,
                 ypool_ref, r_ref, ld_ref, k_ref, v_ref, kk_ref, b_ref, g_ref, bonus_ref,
                 mq_ref, mk_ref, mv_ref, mo_ref, mg_ref,
                 p_scr, *, has_vres):
    i = pl.program_id(1)
    ts = x_ref.shape[0]

    @pl.when(i == 0)
    def _():
        p_scr[0:HALO, :] = jnp.zeros((HALO, D_IN_PAD), F32)

    @pl.when(i > 0)
    def _():
        p_scr[0:HALO, :] = p_scr[ts:ts + HALO, :]

    p_scr[HALO:HALO + ts, :] = jnp.dot(x_ref[...].astype(BF16), w_ref[...], preferred_element_type=F32)

    def rows(shift, c0, c1):
        return p_scr[HALO - shift:HALO - shift + ts, c0:c1]

    u = rows(0, 0, POOL_WIDTH)
    acc = u
    sums = {}
    for s in range(1, POOL_WINDOWS[-1]):
        acc = acc + rows(s, 0, POOL_WIDTH)
        if s + 1 in POOL_WINDOWS:
            sums[s + 1] = acc
    pos = (i * ts + lax.broadcasted_iota(I32, (ts, 1), 0) + 1).astype(F32)
    lane = lax.broadcasted_iota(I32, (ts, POOL_WIDTH), 1)
    d = None
    for gi, win in reversed(list(enumerate(POOL_WINDOWS))):
        dg = sums[win] / jnp.minimum(pos, float(win))
        d = dg if d is None else jnp.where(lane < (gi + 1) * POOL_GROUP, dg, d)
    d = d - u
    ypool_ref[...] = _dot(d, poolw_ref[...]) * pscale_ref[...]

    cur = rows(0, RWKV_OFF, RWKV_OFF + RWKV_COLS)
    prev = rows(1, RWKV_OFF, RWKV_OFF + RWKV_COLS)
    pf = cur + mu_ref[...] * (prev - cur)
    W = RWKV_WIDTH
    r = pf[:, 0:W]
    k = pf[:, W:2 * W]
    v = pf[:, 2 * W:3 * W]
    z = pf[:, 3 * W:3 * W + LANES]
    gd = pf[:, 3 * W + LANES:3 * W + 2 * LANES]
    w_log = -_softplus(-(w0_ref[...] + _dot(jnp.tanh(z), w2_ref[...]))) - 0.5
    ld_ref[...] = -jnp.exp(w_log)
    a = _sigmoid(a0_ref[...] + _dot(z, a2_ref[...]))
    g_ref[...] = _dot(_sigmoid(gd), g2_ref[...])
    gates = rows(0, GATE_OFF, GATE_OFF + LANES)
    if has_vres:
        v_gate = _sigmoid(v0_ref[...] + _dot(gates, v2_ref[...]))
        v = v + (vfirst_ref[...] - v) * v_gate
    ones_bd = ones_ref[...]
    kk = k * kks_ref[...]
    kk = kk / jnp.maximum(jnp.sqrt(_dot_lhs2(kk * kk, ones_bd)), 1e-12)
    k = k * (1.0 + (a - 1.0) * ka_ref[...])
    r_ref[...] = r
    k_ref[...] = k
    v_ref[...] = v
    kk_ref[...] = kk
    b_ref[...] = kk * a
    bonus_ref[...] = _dot_lhs2(r * k * rk_ref[...], ones_bd) * v

    qk = cb_ref[...] + rows(0, MLSTM_OFF, MLSTM_OFF + 2 * MLSTM_WIDTH) * cw_ref[MLSTM_CONV - 1:MLSTM_CONV, :]
    for tap in range(MLSTM_CONV - 1):
        shift = MLSTM_CONV - 1 - tap
        qk = qk + rows(shift, MLSTM_OFF, MLSTM_OFF + 2 * MLSTM_WIDTH) * cw_ref[tap:tap + 1, :]
    qk = qk * _sigmoid(qk)
    mq_ref[...] = qk[:, 0:MLSTM_WIDTH] * (HEAD_DIM ** -0.5)
    mk_ref[...] = qk[:, MLSTM_WIDTH:]
    mv_ref[...] = rows(0, MLSTM_OFF + 2 * MLSTM_WIDTH, MLSTM_OFF + 3 * MLSTM_WIDTH)
    mo_ref[...] = _sigmoid(rows(0, MLSTM_OFF + 3 * MLSTM_WIDTH, MLSTM_OFF + 4 * MLSTM_WIDTH))
    gb = gates + gbias_ref[...]
    glane = lax.broadcasted_iota(I32, (ts, LANES), 1)
    mg_ref[...] = jnp.where(glane < MLSTM_HEADS, gb, -_softplus(-gb))


def _prep_call(x, lw, vfirst, *, has_vres):
    B, S, _ = x.shape
    ts = PREP_ROWS
    grid = (B, S // ts)
    row3 = lambda c: pl.BlockSpec((None, ts, c), lambda b, i: (b, i, 0))
    full = lambda a: pl.BlockSpec(a.shape, lambda b, i: (0,) * a.ndim)
    params = [lw['w_in'], lw['pool_w'], lw['pool_scale'], lw['mu'], lw['w0'], lw['w2'], lw['a0'], lw['a2'],
              lw['g2'], lw['kk_scale'], lw['ka'], lw['rk'], lw['v0'], lw['v2']]
    tail = [lw['conv_w'], lw['conv_b'], lw['gate_bias'], lw['ones_bd']]
    in_specs = ([row3(D_MODEL)] + [full(a) for a in params] + [row3(RWKV_WIDTH)] + [full(a) for a in tail])
    widths = [POOL_WIDTH] + [RWKV_WIDTH] * 8 + [MLSTM_WIDTH] * 4 + [LANES]
    out_shape = [jax.ShapeDtypeStruct((B, S, c), F32) for c in widths]
    out_specs = [row3(c) for c in widths]
    return pl.pallas_call(
        functools.partial(_prep_kernel, has_vres=has_vres),
        grid=grid, in_specs=in_specs, out_specs=out_specs, out_shape=out_shape,
        scratch_shapes=[pltpu.VMEM((HALO + ts, D_IN_PAD), F32)],
        compiler_params=pltpu.CompilerParams(dimension_semantics=("arbitrary", "arbitrary"),
                                             vmem_limit_bytes=VMEM_LIMIT),
        name="prep",
    )(x, *params, vfirst, *tail)


def _rwkv_kernel(r_ref, ld_ref, k_ref, v_ref, kk_ref, b_ref, y_ref, s_scr):
    c = pl.program_id(1)
    L = CHUNK
    n_chunks = r_ref.shape[0] // L

    @pl.when(c == 0)
    def _():
        s_scr[...] = jnp.zeros(s_scr.shape, F32)

    row = lax.broadcasted_iota(I32, (L, L), 0)
    col = lax.broadcasted_iota(I32, (L, L), 1)
    strict = col < row
    incl = col <= row
    tri = incl.astype(BF16)
    eye = (row == col).astype(F32)

    units = [(ci, h) for ci in range(n_chunks) for h in range(RWKV_HEADS)]
    per_chunk = []
    for ci in range(n_chunks):
        sl = pl.ds(ci * L, L)
        ld = ld_ref[sl, :]
        cin = _dot_rhs2(tri, ld)
        c_last = cin[L - 1:L, :]
        e_neg = jnp.exp(-cin)
        e_tail = jnp.exp(c_last - cin)
        kk = kk_ref[sl, :]
        bb = b_ref[sl, :]
        kx = k_ref[sl, :]
        per_chunk.append(dict(
            A=-kk * jnp.exp(cin - ld), R=r_ref[sl, :] * jnp.exp(cin), B=bb * e_neg, K=kx * e_neg,
            Bh=bb * e_tail, Kh=kx * e_tail, V=v_ref[sl, :], g_last=jnp.exp(c_last)))

    def part(name, u):
        ci, h = u
        return per_chunk[ci][name][:, h * HEAD_DIM:(h + 1) * HEAD_DIM]

    M = [_dot_nt(jnp.concatenate([part('A', u), part('R', u)], axis=0),
                 jnp.concatenate([part('B', u), part('K', u)], axis=0)) for u in units]
    m_ab = [jnp.where(strict, m[0:L, 0:L], 0.0) for m in M]
    m_ak = [jnp.where(strict, m[0:L, L:2 * L], 0.0) for m in M]
    m_rb = [jnp.where(incl, m[L:2 * L, 0:L], 0.0) for m in M]
    m_rk = [jnp.where(incl, m[L:2 * L, L:2 * L], 0.0) for m in M]
    MV = [_dot(m, part('V', u)) for m, u in zip(m_ak, units)]
    YK = [_dot(m, part('V', u)) for m, u in zip(m_rk, units)]
    T = [eye + m for m in m_ab]
    pw = m_ab
    for _ in range(5):
        pw = [_dot(p, p) for p in pw]
        T = [t + _dot(t, p) for t, p in zip(T, pw)]
    WU = [_dot(t, jnp.concatenate([part('A', u), mv], axis=1)) for t, u, mv in zip(T, units, MV)]
    GY = [_dot(m, wu) for m, wu in zip(m_rb, WU)]
    G = [part('R', u) + gy[:, 0:L] for u, gy in zip(units, GY)]
    Y0 = [gy[:, L:2 * L] + yk for gy, yk in zip(GY, YK)]
    WB = [_dot_tn(wu, part('Bh', u)) for wu, u in zip(WU, units)]
    VK = [_dot_tn(part('V', u), part('Kh', u)) for u in units]
    P = [wb[0:L, :] + eye * part('g_last', u) for wb, u in zip(WB, units)]
    Q = [wb[L:2 * L, :] + vk for wb, vk in zip(WB, VK)]

    state = [s_scr[h] for h in range(RWKV_HEADS)]
    for ci in range(n_chunks):
        base = ci * RWKV_HEADS
        ys = [_dot_nt(G[base + h], state[h]) + Y0[base + h] for h in range(RWKV_HEADS)]
        state = [_dot(state[h], P[base + h]) + Q[base + h] for h in range(RWKV_HEADS)]
        y_ref[pl.ds(ci * L, L), :] = jnp.concatenate(ys, axis=1)
    for h in range(RWKV_HEADS):
        s_scr[h] = state[h]


def _rwkv_call(r, ld, k, v, kk, b):
    B, S, W = r.shape
    ts = SEQ_ROWS
    spec = pl.BlockSpec((None, ts, W), lambda bi, c: (bi, c, 0))
    return pl.pallas_call(
        _rwkv_kernel, grid=(B, S // ts), in_specs=[spec] * 6, out_specs=spec,
        out_shape=jax.ShapeDtypeStruct((B, S, W), F32),
        scratch_shapes=[pltpu.VMEM((RWKV_HEADS, HEAD_DIM, HEAD_DIM), F32)],
        compiler_params=pltpu.CompilerParams(dimension_semantics=("arbitrary", "arbitrary"),
                                             vmem_limit_bytes=VMEM_LIMIT),
        name="rwkv",
    )(r, ld, k, v, kk, b)


def _mlstm_kernel(q_ref, k_ref, v_ref, g_ref, expand_ref, h_ref, c_scr, n_scr, m_scr):
    c = pl.program_id(1)
    L = CHUNK
    n_chunks = q_ref.shape[0] // L
    H = MLSTM_HEADS

    @pl.when(c == 0)
    def _():
        c_scr[...] = jnp.zeros(c_scr.shape, F32)
        n_scr[...] = jnp.zeros(n_scr.shape, F32)
        m_scr[...] = jnp.zeros(m_scr.shape, F32)

    row = lax.broadcasted_iota(I32, (L, L), 0)
    col = lax.broadcasted_iota(I32, (L, L), 1)
    incl = col <= row
    tri = incl.astype(BF16)

    units = [(ci, h) for ci in range(n_chunks) for h in range(H)]
    ig_rep, g_rep, gates_t, gcum_t = [], [], [], []
    for ci in range(n_chunks):
        gt = g_ref[pl.ds(ci * L, L), :]
        rep = _dot_lhs2(gt, expand_ref[...])
        ig_rep.append(rep[:, 0:MLSTM_WIDTH])
        g_rep.append(_dot_rhs2(tri, rep[:, MLSTM_WIDTH:]))
        gates_t.append(gt.T)
        gcum_t.append(_dot_rhs2(tri, gt).T)

    def part(ref, u):
        ci, h = u
        return ref[pl.ds(ci * L, L), h * HEAD_DIM:(h + 1) * HEAD_DIM]

    def head(x, h):
        return x[:, h * HEAD_DIM:(h + 1) * HEAD_DIM]

    ones = jnp.ones((L, HEAD_DIM), BF16)
    ig_c = [head(ig_rep[ci], h) for ci, h in units]
    g_c = [head(g_rep[ci], h) for ci, h in units]
    g_last = [g[L - 1:L, :] for g in g_c]
    d_log = [jnp.where(incl, g_c[i] - gcum_t[ci][H + h:H + h + 1, :] + gates_t[ci][h:h + 1, :], -jnp.inf)
             for i, (ci, h) in enumerate(units)]
    d_max = [jnp.max(d, axis=-1, keepdims=True) for d in d_log]
    qk = [_dot_nt(part(q_ref, u), part(k_ref, u)) for u in units]
    e = [gl - g + ig for gl, g, ig in zip(g_last, g_c, ig_c)]
    m_loc = [jnp.max(x, axis=0, keepdims=True) for x in e]
    wk = [part(k_ref, u) * jnp.exp(x - m) for u, x, m in zip(units, e, m_loc)]
    kv_loc = [_dot_tn(w, part(v_ref, u)) for w, u in zip(wk, units)]
    n_loc = [_dot_tn(w, ones) for w in wk]

    c_st = [c_scr[h] for h in range(H)]
    n_st = [n_scr[h] for h in range(H)]
    m_st = [m_scr[h:h + 1, 0:HEAD_DIM] for h in range(H)]
    c_prev, n_prev, m_prev = [], [], []
    for i, (ci, h) in enumerate(units):
        c_prev.append(c_st[h])
        n_prev.append(n_st[h])
        m_prev.append(m_st[h])
        m_new = jnp.maximum(g_last[i] + m_st[h], m_loc[i])
        a_old = jnp.exp(g_last[i] + m_st[h] - m_new)
        a_new = jnp.exp(m_loc[i] - m_new)
        c_st[h] = a_old * c_st[h] + a_new * kv_loc[i]
        n_st[h] = a_old * n_st[h] + a_new * n_loc[i]
        m_st[h] = m_new
    for h in range(H):
        c_scr[h] = c_st[h]
        n_scr[h] = n_st[h]
        m_scr[h:h + 1, 0:HEAD_DIM] = m_st[h]

    inter_log = [g + m for g, m in zip(g_c, m_prev)]
    m_j = [jnp.maximum(dm, il) for dm, il in zip(d_max, inter_log)]
    w_intra = [jnp.exp(d - m) * s for d, m, s in zip(d_log, m_j, qk)]
    w_inter = [jnp.exp(il - m) for il, m in zip(inter_log, m_j)]
    intra = [_dot(w, part(v_ref, u)) for w, u in zip(w_intra, units)]
    inter = [_dot(part(q_ref, u), c) for u, c in zip(units, c_prev)]
    den_a = [_dot(w, ones) for w in w_intra]
    den_b = [_dot(part(q_ref, u), n) for u, n in zip(units, n_prev)]
    outs = [(ia + wi * ie) / jnp.maximum(jnp.abs(da + wi * db), jnp.exp(-m))
            for ia, wi, ie, da, db, m in zip(intra, w_inter, inter, den_a, den_b, m_j)]
    for ci in range(n_chunks):
        h_ref[pl.ds(ci * L, L), :] = jnp.concatenate(outs[ci * H:(ci + 1) * H], axis=1)


def _gate_expand_matrix():
    lane = jnp.arange(LANES)[:, None]
    col = jnp.arange(2 * MLSTM_WIDTH)[None, :]
    src = jnp.where(col < MLSTM_WIDTH, col // HEAD_DIM, MLSTM_HEADS + (col - MLSTM_WIDTH) // HEAD_DIM)
    return (lane == src).astype(BF16)


def _mlstm_call(q, k, v, g):
    B, S, W = q.shape
    ts = SEQ_ROWS
    spec = pl.BlockSpec((None, ts, W), lambda bi, c: (bi, c, 0))
    gspec = pl.BlockSpec((None, ts, LANES), lambda bi, c: (bi, c, 0))
    espec = pl.BlockSpec((LANES, 2 * W), lambda bi, c: (0, 0))
    return pl.pallas_call(
        _mlstm_kernel, grid=(B, S // ts), in_specs=[spec, spec, spec, gspec, espec], out_specs=spec,
        out_shape=jax.ShapeDtypeStruct((B, S, W), F32),
        scratch_shapes=[pltpu.VMEM((MLSTM_HEADS, HEAD_DIM, HEAD_DIM), F32),
                        pltpu.VMEM((MLSTM_HEADS, HEAD_DIM, HEAD_DIM), F32),
                        pltpu.VMEM((8, LANES), F32)],
        compiler_params=pltpu.CompilerParams(dimension_semantics=("arbitrary", "arbitrary"),
                                             vmem_limit_bytes=VMEM_LIMIT),
        name="mlstm",
    )(q, k, v, g, _gate_expand_matrix())


def _layer_norm(z, g, b):
    mu = jnp.mean(z, axis=-1, keepdims=True)
    d = z - mu
    var = jnp.mean(d * d, axis=-1, keepdims=True)
    return d * lax.rsqrt(var + LN_EPS) * g + b


def _post_kernel(h_ref, ypool_ref, yr_ref, bonus_ref, g_ref, hm_ref, mo_ref,
                 lnxg_ref, lnxb_ref, ng_ref, wout_ref, ln1g_ref, ln1b_ref, rwt_ref, rb_ref, ones_ref,
                 h1_ref, gate_ref, route_ref, cnt_ref):
    ts = h_ref.shape[0]
    ones_bd = ones_ref[...]
    y_rwkv = (_head_norm(yr_ref[...], ones_bd, RWKV_GN_EPS) * lnxg_ref[...] + lnxb_ref[...]
              + bonus_ref[...]) * g_ref[...]
    y_ml = mo_ref[...] * (_head_norm(hm_ref[...], ones_bd, LN_EPS) * ng_ref[...])
    mix = (_dot(ypool_ref[...], wout_ref[0:POOL_WIDTH, :])
           + _dot(y_rwkv, wout_ref[POOL_WIDTH:POOL_WIDTH + RWKV_WIDTH, :])
           + _dot(y_ml, wout_ref[POOL_WIDTH + RWKV_WIDTH:, :]))
    h1 = _layer_norm(DEEPNORM_ALPHA * h_ref[...] + mix, ln1g_ref[...], ln1b_ref[...])
    h1_ref[...] = h1

    hh, hl = _split(h1)
    wh, wl = _split(rwt_ref[...])
    nt = lambda a, b: lax.dot_general(a, b, (((1,), (1,)), ((), ())), preferred_element_type=F32)
    logits = nt(wh, hh) + nt(wh, hl) + nt(wl, hh) + rb_ref[...]
    eidx = lax.broadcasted_iota(I32, (N_EXPERTS, ts), 0)
    vals = logits
    tops, hots, idxs = [], [], []
    for _ in range(TOP_K):
        mx = jnp.max(vals, axis=0, keepdims=True)
        idx = jnp.min(jnp.where(vals == mx, eidx, N_EXPERTS), axis=0, keepdims=True)
        hot = eidx == idx
        vals = jnp.where(hot, -jnp.inf, vals)
        tops.append(mx)
        hots.append(hot)
        idxs.append(idx)
    exps = [jnp.exp(t - tops[0]) for t in tops]
    denom = exps[0] + exps[1] + exps[2] + exps[3]
    gate_rows = [e / denom for e in exps]
    gate_ref[...] = jnp.concatenate(gate_rows + [jnp.zeros((8 - TOP_K, ts), F32)], axis=0)

    any_hot = (hots[0] | hots[1] | hots[2] | hots[3])
    hot_f = any_hot.astype(F32)
    r_i = lax.broadcasted_iota(I32, (ts, ts), 0)
    c_i = lax.broadcasted_iota(I32, (ts, ts), 1)
    before = (r_i < c_i).astype(BF16)
    within = jnp.dot(hot_f.astype(BF16), before, preferred_element_type=F32)
    n_e = jnp.broadcast_to(jnp.sum(hot_f, axis=1, keepdims=True), (N_EXPERTS, LANES))
    run_len = jnp.floor((n_e + (RUN_PIECE - 1)) * (1.0 / RUN_PIECE)) * RUN_PIECE
    e_r = lax.broadcasted_iota(I32, (N_EXPERTS, N_EXPERTS), 0)
    e_c = lax.broadcasted_iota(I32, (N_EXPERTS, N_EXPERTS), 1)
    lower = (e_c < e_r).astype(BF16)
    run_start = _dot_rhs2(lower, run_len)[:, 0:1]
    slots = [jnp.sum(jnp.where(hot, run_start + within, 0.0), axis=0, keepdims=True) for hot in hots]
    route_ref[...] = jnp.concatenate([p.astype(I32) for p in slots] + [jnp.zeros((8 - TOP_K, ts), I32)], axis=0)
    cnt_ref[...] = n_e


def _post_call(h, ypool, yr, bonus, g, hm, mo, lw):
    T = h.shape[0]
    ts = PREP_ROWS
    row = lambda c: pl.BlockSpec((ts, c), lambda i: (i, 0))
    colb = lambda r: pl.BlockSpec((r, ts), lambda i: (0, i))
    full = lambda a: pl.BlockSpec(a.shape, lambda i: (0,) * a.ndim)
    params = [lw['lnx_g'], lw['lnx_b'], lw['norm_g'], lw['w_out'], lw['ln1_g'], lw['ln1_b'],
              lw['router_wt'], lw['router_b'], lw['ones_bd']]
    in_specs = ([row(D_MODEL), row(POOL_WIDTH)] + [row(RWKV_WIDTH)] * 5 + [full(a) for a in params])
    out_shape = [jax.ShapeDtypeStruct((T, D_MODEL), F32), jax.ShapeDtypeStruct((8, T), F32),
                 jax.ShapeDtypeStruct((8, T), I32), jax.ShapeDtypeStruct((T // ts * N_EXPERTS, LANES), F32)]
    out_specs = [row(D_MODEL), colb(8), colb(8), pl.BlockSpec((N_EXPERTS, LANES), lambda i: (i, 0))]
    return pl.pallas_call(
        _post_kernel, grid=(T // ts,), in_specs=in_specs, out_specs=out_specs, out_shape=out_shape,
        compiler_params=pltpu.CompilerParams(dimension_semantics=("arbitrary",),
                                             vmem_limit_bytes=VMEM_LIMIT),
        name="post",
    )(h, ypool, yr, bonus, g, hm, mo, *params)


def _rows_copy(src_ref, src_row, dst_ref, dst_row, n_rows, sem):
    src_row, dst_row = [r if isinstance(r, int) else pl.multiple_of(r, SUBLANES) for r in (src_row, dst_row)]
    return pltpu.make_async_copy(src_ref.at[pl.ds(src_row, n_rows), :], dst_ref.at[pl.ds(dst_row, n_rows), :], sem)


def _stage_rows(ts):
    n = TOP_K * ts + N_EXPERTS * (RUN_PIECE - 1)
    return -(-n // SUBLANES) * SUBLANES


def _for_each_piece(n_pieces, body):
    lax.fori_loop(0, n_pieces, lambda j, c: (body(j), c)[1], 0)


def _dispatch_kernel(tab_ref, fill_ref, route_ref, gate_ref, h_ref, buf_ref, sorted_scr, zero_scr, sem):
    step = pl.program_id(0)
    ts = h_ref.shape[0]
    n_sorted = sorted_scr.shape[0]

    @pl.when(step == 0)
    def _():
        zero_scr[...] = jnp.zeros(zero_scr.shape, F32)

        def fill(wait):
            for e in range(N_EXPERTS):
                def piece(j, e=e):
                    cp = _rows_copy(zero_scr, 0, buf_ref, fill_ref[0, e] + j * FILL_PIECE, FILL_PIECE, sem)
                    cp.wait() if wait else cp.start()
                _for_each_piece(fill_ref[0, N_EXPERTS + e], piece)

            def tail(j):
                cp = _rows_copy(zero_scr, 0, buf_ref, fill_ref[0, 2 * N_EXPERTS] + j * EXPERT_ROWS,
                                EXPERT_ROWS, sem)
                cp.wait() if wait else cp.start()
            _for_each_piece(fill_ref[0, 2 * N_EXPERTS + 1], tail)

        fill(False)
        fill(True)

    pos = lax.broadcasted_iota(I32, (n_sorted, ts), 0)
    hot = None
    weight = None
    for kslot in range(TOP_K):
        eq = pos == route_ref[kslot:kslot + 1, :]
        w = jnp.where(eq, gate_ref[kslot:kslot + 1, :], 0.0)
        hot = eq if hot is None else (hot | eq)
        weight = w if weight is None else weight + w
    sorted_scr[0:n_sorted, 0:D_MODEL] = jnp.dot(hot.astype(BF16), h_ref[...].astype(BF16),
                                                preferred_element_type=F32)
    w_hi, w_lo = _split(weight)
    w_lo2 = (weight - w_hi.astype(F32) - w_lo.astype(F32)).astype(BF16)
    ones = jnp.ones((ts, LANES), BF16)
    sorted_scr[0:n_sorted, D_MODEL:BUF_WIDTH] = (jnp.dot(w_hi, ones, preferred_element_type=F32)
                                                 + jnp.dot(w_lo, ones, preferred_element_type=F32)
                                                 + jnp.dot(w_lo2, ones, preferred_element_type=F32))

    def runs(wait):
        for e in range(N_EXPERTS):
            n_rows = pl.multiple_of(tab_ref[0, e], SUBLANES)

            @pl.when(n_rows > 0)
            def _(e=e, n_rows=n_rows):
                cp = _rows_copy(sorted_scr, tab_ref[0, N_EXPERTS + e], buf_ref, tab_ref[0, 2 * N_EXPERTS + e],
                                n_rows, sem)
                cp.wait() if wait else cp.start()

    runs(False)
    runs(True)


def _dispatch_call(tab_d, fill_tab, route8, gate8, h1, n_rows):
    T = h1.shape[0]
    ts = PREP_ROWS
    smem_tile = pl.BlockSpec((None, 1, LANES), lambda i: (i, 0, 0), memory_space=pltpu.SMEM)
    return pl.pallas_call(
        _dispatch_kernel, grid=(T // ts,),
        in_specs=[smem_tile,
                  pl.BlockSpec((1, LANES), lambda i: (0, 0), memory_space=pltpu.SMEM),
                  pl.BlockSpec((8, ts), lambda i: (0, i)),
                  pl.BlockSpec((8, ts), lambda i: (0, i)),
                  pl.BlockSpec((ts, D_MODEL), lambda i: (i, 0))],
        out_specs=pl.BlockSpec(memory_space=pl.ANY),
        out_shape=jax.ShapeDtypeStruct((n_rows, BUF_WIDTH), F32),
        scratch_shapes=[pltpu.VMEM((_stage_rows(ts), BUF_WIDTH), F32),
                        pltpu.VMEM((EXPERT_ROWS, BUF_WIDTH), F32),
                        pltpu.SemaphoreType.DMA(())],
        compiler_params=pltpu.CompilerParams(dimension_semantics=("arbitrary",),
                                             vmem_limit_bytes=VMEM_LIMIT),
        name="dispatch",
    )(tab_d, fill_tab, route8, gate8, h1)


def _expert_kernel(be_ref, nb_ref, x_ref, wgu_ref, bgu_ref, wdn_ref, bdn_ref, o_ref, wgu_scr, wdn_scr):
    j = pl.program_id(0)
    used = j < nb_ref[0]
    changed = jnp.logical_or(j == 0, be_ref[j] != be_ref[jnp.maximum(j - 1, 0)])

    @pl.when(jnp.logical_and(used, changed))
    def _():
        wgu_scr[...] = wgu_ref[...].astype(BF16)
        wdn_scr[...] = wdn_ref[...].astype(BF16)

    @pl.when(used)
    def _():
        gu = jnp.dot(x_ref[:, 0:D_MODEL].astype(BF16), wgu_scr[...], preferred_element_type=F32) + bgu_ref[...]
        glu = jnp.minimum(gu[:, 0:D_FF], SWIGLU_LIMIT)
        lin = jnp.clip(gu[:, D_FF:], -SWIGLU_LIMIT, SWIGLU_LIMIT)
        act = glu * _sigmoid(SWIGLU_ALPHA * glu) * (lin + 1.0)
        y = jnp.dot(act.astype(BF16), wdn_scr[...], preferred_element_type=F32) + bdn_ref[...]
        gate = x_ref[:, D_MODEL:BUF_WIDTH]
        o_ref[...] = y * jnp.concatenate([gate] * (D_MODEL // LANES), axis=1)

    @pl.when(jnp.logical_not(used))
    def _():
        o_ref[...] = jnp.zeros(o_ref.shape, F32)


def _expert_call(block_e, n_used, buf, lw):
    n_rows = buf.shape[0]
    rows = EXPERT_ROWS
    n_blocks = n_rows // rows

    def blk(j, be, nb):
        return jnp.minimum(j, nb[0] - 1)

    layer = lw['layer']
    grid_spec = pltpu.PrefetchScalarGridSpec(
        num_scalar_prefetch=2, grid=(n_blocks,),
        in_specs=[pl.BlockSpec((rows, BUF_WIDTH), lambda j, be, nb: (blk(j, be, nb), 0)),
                  pl.BlockSpec((None, None, D_MODEL, 2 * D_FF), lambda j, be, nb: (layer, be[j], 0, 0)),
                  pl.BlockSpec((None, None, 1, 2 * D_FF), lambda j, be, nb: (layer, be[j], 0, 0)),
                  pl.BlockSpec((None, None, D_FF, D_MODEL), lambda j, be, nb: (layer, be[j], 0, 0)),
                  pl.BlockSpec((None, None, 1, D_MODEL), lambda j, be, nb: (layer, be[j], 0, 0))],
        out_specs=pl.BlockSpec((rows, D_MODEL), lambda j, be, nb: (j, 0)),
        scratch_shapes=[pltpu.VMEM((D_MODEL, 2 * D_FF), BF16), pltpu.VMEM((D_FF, D_MODEL), BF16)])
    return pl.pallas_call(
        _expert_kernel, grid_spec=grid_spec,
        out_shape=jax.ShapeDtypeStruct((n_rows, D_MODEL), F32),
        compiler_params=pltpu.CompilerParams(dimension_semantics=("arbitrary",),
                                             vmem_limit_bytes=56 * 1024 * 1024),
        name="expert",
    )(block_e, n_used, buf, lw['w_gate_up'], lw['b_gate_up'], lw['w_down'], lw['b_down'])


def _combine_kernel(tab_ref, route_ref, h_ref, out_hbm_ref, ln2g_ref, ln2b_ref, o_ref, stage_scr, sem):
    step = pl.program_id(0)
    ts = h_ref.shape[0]
    n_stage = stage_scr.shape[0]

    @pl.when(step == 0)
    def _():
        stage_scr[...] = jnp.zeros(stage_scr.shape, F32)

    def runs(wait):
        for e in range(N_EXPERTS):
            n_rows = pl.multiple_of(tab_ref[0, e], SUBLANES)

            @pl.when(n_rows > 0)
            def _(e=e, n_rows=n_rows):
                cp = _rows_copy(out_hbm_ref, tab_ref[0, 2 * N_EXPERTS + e], stage_scr, tab_ref[0, N_EXPERTS + e],
                                n_rows, sem)
                cp.wait() if wait else cp.start()

    runs(False)
    slot_f = route_ref[...].astype(F32)
    slot_cols = jnp.concatenate([slot_f, jnp.zeros((LANES - 8, ts), F32)], axis=0).T
    lane = lax.broadcasted_iota(I32, (ts, n_stage), 1).astype(F32)
    sel = None
    for kslot in range(TOP_K):
        eq = lane == slot_cols[:, kslot:kslot + 1]
        sel = eq if sel is None else (sel | eq)
    runs(True)
    y = jnp.dot(sel.astype(BF16), stage_scr[...].astype(BF16), preferred_element_type=F32)
    o_ref[...] = _layer_norm(DEEPNORM_ALPHA * h_ref[...] + y, ln2g_ref[...], ln2b_ref[...])


def _combine_call(tab_c, route8, h1, out_rows, lw):
    T = h1.shape[0]
    ts = PREP_ROWS
    full = lambda a: pl.BlockSpec(a.shape, lambda i: (0,) * a.ndim)
    n_stage = _stage_rows(ts)
    return pl.pallas_call(
        _combine_kernel, grid=(T // ts,),
        in_specs=[pl.BlockSpec((None, 1, LANES), lambda i: (i, 0, 0), memory_space=pltpu.SMEM),
                  pl.BlockSpec((8, ts), lambda i: (0, i)),
                  pl.BlockSpec((ts, D_MODEL), lambda i: (i, 0)),
                  pl.BlockSpec(memory_space=pl.ANY),
                  full(lw['ln2_g']), full(lw['ln2_b'])],
        out_specs=pl.BlockSpec((ts, D_MODEL), lambda i: (i, 0)),
        out_shape=jax.ShapeDtypeStruct((T, D_MODEL), F32),
        scratch_shapes=[pltpu.VMEM((n_stage, D_MODEL), F32), pltpu.SemaphoreType.DMA(())],
        compiler_params=pltpu.CompilerParams(dimension_semantics=("arbitrary",),
                                             vmem_limit_bytes=VMEM_LIMIT),
        name="combine",
    )(tab_c, route8, h1, out_rows, lw['ln2_g'], lw['ln2_b'])


def _block_diag_ones(width):
    hid = jnp.arange(width) // HEAD_DIM
    return (hid[:, None] == hid[None, :]).astype(BF16)


def _layer_params(l, w_in, pool_w, pool_scale, rwkv_mu, rwkv_w0, rwkv_w2, rwkv_a0, rwkv_a2, rwkv_g2,
                  rwkv_kk_scale, rwkv_ka, rwkv_rk, rwkv_lnx_g, rwkv_lnx_b, rwkv_v0, rwkv_v1, rwkv_v2,
                  mlstm_conv_w, mlstm_conv_b, mlstm_b_i, mlstm_b_f, mlstm_norm_g, w_out, ln1_g, ln1_b,
                  router_w, router_b, w_gate_up, b_gate_up, w_down, b_down, ln2_g, ln2_b):
    row = lambda a: a.reshape(1, -1).astype(F32)
    pad_cols = D_IN_PAD - D_IN
    if l > 0:
        extra = jnp.concatenate([rwkv_v1[l - 1], jnp.zeros((D_MODEL, pad_cols - VRES_LORA), F32)], axis=1)
        v0 = row(rwkv_v0[l - 1])
        v2 = jnp.zeros((LANES, RWKV_WIDTH), F32).at[VRES_OFF:VRES_OFF + VRES_LORA].set(rwkv_v2[l - 1])
    else:
        extra = jnp.zeros((D_MODEL, pad_cols), F32)
        v0 = jnp.zeros((1, RWKV_WIDTH), F32)
        v2 = jnp.zeros((LANES, RWKV_WIDTH), F32)
    pw = jnp.zeros((POOL_WIDTH, POOL_WIDTH), F32)
    for gi in range(len(POOL_WINDOWS)):
        sl = slice(gi * POOL_GROUP, (gi + 1) * POOL_GROUP)
        pw = pw.at[sl, sl].set(pool_w[l, gi])
    zero_lora = jnp.zeros((DECAY_LORA, RWKV_WIDTH), F32)
    gate_bias = jnp.zeros((1, LANES), F32).at[0, 0:MLSTM_HEADS].set(mlstm_b_i[l])
    gate_bias = gate_bias.at[0, MLSTM_HEADS:2 * MLSTM_HEADS].set(mlstm_b_f[l])
    return {
        'w_in': jnp.concatenate([w_in[l], extra], axis=1).astype(BF16),
        'pool_w': pw.astype(BF16), 'pool_scale': row(pool_scale[l]), 'mu': row(rwkv_mu[l]),
        'w0': row(rwkv_w0[l]), 'w2': jnp.concatenate([rwkv_w2[l], zero_lora], axis=0).astype(BF16),
        'a0': row(rwkv_a0[l]), 'a2': jnp.concatenate([zero_lora, rwkv_a2[l]], axis=0).astype(BF16),
        'g2': rwkv_g2[l].astype(BF16), 'kk_scale': row(rwkv_kk_scale[l]), 'ka': row(rwkv_ka[l]),
        'rk': row(rwkv_rk[l]), 'v0': v0, 'v2': v2.astype(BF16),
        'conv_w': mlstm_conv_w[l], 'conv_b': row(mlstm_conv_b[l]), 'gate_bias': gate_bias,
        'ones_bd': _block_diag_ones(RWKV_WIDTH),
        'lnx_g': row(rwkv_lnx_g[l]), 'lnx_b': row(rwkv_lnx_b[l]), 'norm_g': row(mlstm_norm_g[l]),
        'w_out': w_out[l].astype(BF16), 'ln1_g': row(ln1_g[l]), 'ln1_b': row(ln1_b[l]),
        'router_wt': router_w[l].T, 'router_b': router_b[l].reshape(N_EXPERTS, 1),
        'layer': l, 'w_gate_up': w_gate_up, 'b_gate_up': b_gate_up.reshape(-1, N_EXPERTS, 1, 2 * D_FF),
        'w_down': w_down, 'b_down': b_down.reshape(-1, N_EXPERTS, 1, D_MODEL),
        'ln2_g': row(ln2_g[l]), 'ln2_b': row(ln2_b[l]),
    }


def _layer(h, v_first, lw, *, has_vres):
    B, S, _ = h.shape
    T = B * S
    outs = _prep_call(h, lw, v_first, has_vres=has_vres)
    ypool, r, ld, k, v, kk, b, g, bonus, mq, mk, mv, mo, mg = outs
    yr = _rwkv_call(r, ld, k, v, kk, b)
    hm = _mlstm_call(mq, mk, mv, mg)
    flat = lambda a: a.reshape(T, a.shape[-1])
    h_flat = flat(h)
    h1, gate8, route8, cnt = _post_call(h_flat, flat(ypool), flat(yr), flat(bonus), flat(g), flat(hm),
                                        flat(mo), lw)
    n_tiles = T // PREP_ROWS
    cnt = cnt.reshape(n_tiles, N_EXPERTS, LANES)[:, :, 0].astype(I32)
    rows = EXPERT_ROWS
    run_len = (cnt + RUN_PIECE - 1) // RUN_PIECE * RUN_PIECE
    region = jnp.sum(run_len, axis=0)
    padded = (region + rows - 1) // rows * rows
    pad_end = jnp.cumsum(padded)
    pad_start = pad_end - padded
    n_blocks = -(-(T * TOP_K + n_tiles * N_EXPERTS * (RUN_PIECE - 1)) // rows) + N_EXPERTS
    n_used = (pad_end[-1] // rows).astype(I32).reshape(1)
    starts = jnp.minimum(jnp.arange(n_blocks, dtype=I32), n_used[0] - 1) * rows
    block_e = jnp.sum((pad_end[None, :] <= starts[:, None]).astype(I32), axis=1)
    run_src = jnp.cumsum(run_len, axis=1) - run_len
    run_dst = pad_start[None, :] + jnp.cumsum(run_len, axis=0) - run_len
    run_tab = jnp.concatenate([run_len, run_src, run_dst,
                               jnp.zeros((n_tiles, LANES - 3 * N_EXPERTS), I32)], axis=1).reshape(n_tiles, 1, LANES)
    fill_start = (pad_start + region) // FILL_PIECE * FILL_PIECE
    fill_tab = jnp.concatenate([fill_start, (pad_end - fill_start) // FILL_PIECE,
                                pad_end[-1:], n_blocks - n_used,
                                jnp.zeros((LANES - 2 * N_EXPERTS - 2,), I32)]).reshape(1, LANES)
    buf = _dispatch_call(run_tab, fill_tab, route8, gate8, h1, n_blocks * rows)
    out_rows = _expert_call(block_e, n_used, buf, lw)
    h2 = _combine_call(run_tab, route8, h1, out_rows, lw)
    return h2.reshape(B, S, D_MODEL), v


def kernel(x, w_in, pool_w, pool_scale, rwkv_mu, rwkv_w0, rwkv_w2, rwkv_a0, rwkv_a2, rwkv_g2, rwkv_kk_scale, rwkv_ka, rwkv_rk, rwkv_lnx_g, rwkv_lnx_b, rwkv_v0, rwkv_v1, rwkv_v2, mlstm_conv_w, mlstm_conv_b, mlstm_b_i, mlstm_b_f, mlstm_norm_g, w_out, ln1_g, ln1_b, router_w, router_b, w_gate_up, b_gate_up, w_down, b_down, ln2_g, ln2_b):
    weights = (w_in, pool_w, pool_scale, rwkv_mu, rwkv_w0, rwkv_w2, rwkv_a0, rwkv_a2, rwkv_g2, rwkv_kk_scale,
               rwkv_ka, rwkv_rk, rwkv_lnx_g, rwkv_lnx_b, rwkv_v0, rwkv_v1, rwkv_v2, mlstm_conv_w, mlstm_conv_b,
               mlstm_b_i, mlstm_b_f, mlstm_norm_g, w_out, ln1_g, ln1_b, router_w, router_b, w_gate_up,
               b_gate_up, w_down, b_down, ln2_g, ln2_b)
    h = x
    v_first = jnp.zeros(x.shape[:2] + (RWKV_WIDTH,), F32)
    for l in range(w_in.shape[0]):
        lw = _layer_params(l, *weights)
        h, v_l = _layer(h, v_first, lw, has_vres=l > 0)
        if l == 0:
            v_first = v_l
    return h
```

```python
import functools

import jax
import jax.numpy as jnp
from jax import lax
from jax.experimental import pallas as pl
from jax.experimental.pallas import tpu as pltpu

F32 = jnp.float32
BF16 = jnp.bfloat16
I32 = jnp.int32

D_MODEL = 1024
HEAD_DIM = 64
POOL_WINDOWS = (2, 4, 8, 16)
POOL_WIDTH = 256
POOL_GROUP = 64
RWKV_WIDTH = 384
RWKV_HEADS = 6
DECAY_LORA = 64
ICLR_LORA = 64
GATE_LORA = 128
VRES_LORA = 32
RWKV_GN_EPS = 64e-5
RWKV_COLS = 3 * RWKV_WIDTH + DECAY_LORA + ICLR_LORA + GATE_LORA
MLSTM_WIDTH = 384
MLSTM_HEADS = 6
MLSTM_CONV = 4
MLSTM_COLS = 4 * MLSTM_WIDTH + 2 * MLSTM_HEADS
D_IN = POOL_WIDTH + RWKV_COLS + MLSTM_COLS
N_EXPERTS = 32
TOP_K = 4
D_FF = D_MODEL
SWIGLU_LIMIT = 7.0
SWIGLU_ALPHA = 1.702
LN_EPS = 1e-5
DEPTH = 2
DEEPNORM_ALPHA = (2 * DEPTH) ** 0.25

LANES = 128
D_IN_PAD = 3328
RWKV_OFF = POOL_WIDTH
MLSTM_OFF = POOL_WIDTH + RWKV_COLS
GATE_OFF = MLSTM_OFF + 4 * MLSTM_WIDTH
VRES_OFF = 2 * MLSTM_HEADS
HALO = 16
CHUNK = 64

PREP_ROWS = 256
SEQ_ROWS = 256
EXPERT_ROWS = 512
BUF_WIDTH = D_MODEL + LANES
SUBLANES = 8
RUN_PIECE = SUBLANES
FILL_PIECE = 64
VMEM_LIMIT = 48 * 1024 * 1024


def _dot(a, b):
    return jnp.dot(a.astype(BF16), b.astype(BF16), preferred_element_type=F32)


def _dot_nt(a, b):
    return lax.dot_general(a.astype(BF16), b.astype(BF16), (((1,), (1,)), ((), ())),
                           preferred_element_type=F32)


def _dot_tn(a, b):
    return lax.dot_general(a.astype(BF16), b.astype(BF16), (((0,), (0,)), ((), ())),
                           preferred_element_type=F32)


def _split(x):
    hi = x.astype(BF16)
    lo = (x - hi.astype(F32)).astype(BF16)
    return hi, lo


def _dot_lhs2(a, b_bf16):
    hi, lo = _split(a)
    return (jnp.dot(hi, b_bf16, preferred_element_type=F32)
            + jnp.dot(lo, b_bf16, preferred_element_type=F32))


def _dot_rhs2(a_bf16, b):
    hi, lo = _split(b)
    return (jnp.dot(a_bf16, hi, preferred_element_type=F32)
            + jnp.dot(a_bf16, lo, preferred_element_type=F32))


def _sigmoid(x):
    return 1.0 / (1.0 + jnp.exp(-x))


def _softplus(x):
    return jnp.maximum(x, 0.0) + jnp.log(1.0 + jnp.exp(-jnp.abs(x)))


def _head_norm(y, ones_bd, eps):
    inv = 1.0 / HEAD_DIM
    mean = _dot_lhs2(y, ones_bd) * inv
    d = y - mean
    var = _dot_lhs2(d * d, ones_bd) * inv
    return d * lax.rsqrt(var + eps)


def _prep_kernel(x_ref, w_ref, poolw_ref, pscale_ref, mu_ref, w0_ref, w2_ref, a0_ref, a2_ref, g2_ref,
                 kks_ref, ka_ref, rk_ref, v0_ref, v2_ref, vfirst_ref, cw_ref, cb_ref, gbias_ref, ones_ref,
                 ypool_ref, r_ref, ld_ref, k_ref, v_ref, kk_ref, b_ref, g_ref, bonus_ref,
                 mq_ref, mk_ref, mv_ref, mo_ref, mg_ref,
                 p_scr, *, has_vres):
    i = pl.program_id(1)
    ts = x_ref.shape[0]

    @pl.when(i == 0)
    def _():
        p_scr[0:HALO, :] = jnp.zeros((HALO, D_IN_PAD), F32)

    @pl.when(i > 0)
    def _():
        p_scr[0:HALO, :] = p_scr[ts:ts + HALO, :]

    p_scr[HALO:HALO + ts, :] = jnp.dot(x_ref[...].astype(BF16), w_ref[...], preferred_element_type=F32)

    def rows(shift, c0, c1):
        return p_scr[HALO - shift:HALO - shift + ts, c0:c1]

    u = rows(0, 0, POOL_WIDTH)
    acc = u
    sums = {}
    for s in range(1, POOL_WINDOWS[-1]):
        acc = acc + rows(s, 0, POOL_WIDTH)
        if s + 1 in POOL_WINDOWS:
            sums[s + 1] = acc
    pos = (i * ts + lax.broadcasted_iota(I32, (ts, 1), 0) + 1).astype(F32)
    lane = lax.broadcasted_iota(I32, (ts, POOL_WIDTH), 1)
    d = None
    for gi, win in reversed(list(enumerate(POOL_WINDOWS))):
        dg = sums[win] / jnp.minimum(pos, float(win))
        d = dg if d is None else jnp.where(lane < (gi + 1) * POOL_GROUP, dg, d)
    d = d - u
    ypool_ref[...] = _dot(d, poolw_ref[...]) * pscale_ref[...]

    cur = rows(0, RWKV_OFF, RWKV_OFF + RWKV_COLS)
    prev = rows(1, RWKV_OFF, RWKV_OFF + RWKV_COLS)
    pf = cur + mu_ref[...] * (prev - cur)
    W = RWKV_WIDTH
    r = pf[:, 0:W]
    k = pf[:, W:2 * W]
    v = pf[:, 2 * W:3 * W]
    z = pf[:, 3 * W:3 * W + LANES]
    gd = pf[:, 3 * W + LANES:3 * W + 2 * LANES]
    w_log = -_softplus(-(w0_ref[...] + _dot(jnp.tanh(z), w2_ref[...]))) - 0.5
    ld_ref[...] = -jnp.exp(w_log)
    a = _sigmoid(a0_ref[...] + _dot(z, a2_ref[...]))
    g_ref[...] = _dot(_sigmoid(gd), g2_ref[...])
    gates = rows(0, GATE_OFF, GATE_OFF + LANES)
    if has_vres:
        v_gate = _sigmoid(v0_ref[...] + _dot(gates, v2_ref[...]))
        v = v + (vfirst_ref[...] - v) * v_gate
    ones_bd = ones_ref[...]
    kk = k * kks_ref[...]
    kk = kk / jnp.maximum(jnp.sqrt(_dot_lhs2(kk * kk, ones_bd)), 1e-12)
    k = k * (1.0 + (a - 1.0) * ka_ref[...])
    r_ref[...] = r
    k_ref[...] = k
    v_ref[...] = v
    kk_ref[...] = kk
    b_ref[...] = kk * a
    bonus_ref[...] = _dot_lhs2(r * k * rk_ref[...], ones_bd) * v

    qk = cb_ref[...] + rows(0, MLSTM_OFF, MLSTM_OFF + 2 * MLSTM_WIDTH) * cw_ref[MLSTM_CONV - 1:MLSTM_CONV, :]
    for tap in range(MLSTM_CONV - 1):
        shift = MLSTM_CONV - 1 - tap
        qk = qk + rows(shift, MLSTM_OFF, MLSTM_OFF + 2 * MLSTM_WIDTH) * cw_ref[tap:tap + 1, :]
    qk = qk * _sigmoid(qk)
    mq_ref[...] = qk[:, 0:MLSTM_WIDTH] * (HEAD_DIM ** -0.5)
    mk_ref[...] = qk[:, MLSTM_WIDTH:]
    mv_ref[...] = rows(0, MLSTM_OFF + 2 * MLSTM_WIDTH, MLSTM_OFF + 3 * MLSTM_WIDTH)
    mo_ref[...] = _sigmoid(rows(0, MLSTM_OFF + 3 * MLSTM_WIDTH, MLSTM_OFF + 4 * MLSTM_WIDTH))
    gb = gates + gbias_ref[...]
    glane = lax.broadcasted_iota(I32, (ts, LANES), 1)
    mg_ref[...] = jnp.where(glane < MLSTM_HEADS, gb, -_softplus(-gb))


def _prep_call(x, lw, vfirst, *, has_vres):
    B, S, _ = x.shape
    ts = PREP_ROWS
    grid = (B, S // ts)
    row3 = lambda c: pl.BlockSpec((None, ts, c), lambda b, i: (b, i, 0))
    full = lambda a: pl.BlockSpec(a.shape, lambda b, i: (0,) * a.ndim)
    params = [lw['w_in'], lw['pool_w'], lw['pool_scale'], lw['mu'], lw['w0'], lw['w2'], lw['a0'], lw['a2'],
              lw['g2'], lw['kk_scale'], lw['ka'], lw['rk'], lw['v0'], lw['v2']]
    tail = [lw['conv_w'], lw['conv_b'], lw['gate_bias'], lw['ones_bd']]
    in_specs = ([row3(D_MODEL)] + [full(a) for a in params] + [row3(RWKV_WIDTH)] + [full(a) for a in tail])
    widths = [POOL_WIDTH] + [RWKV_WIDTH] * 8 + [MLSTM_WIDTH] * 4 + [LANES]
    out_shape = [jax.ShapeDtypeStruct((B, S, c), F32) for c in widths]
    out_specs = [row3(c) for c in widths]
    return pl.pallas_call(
        functools.partial(_prep_kernel, has_vres=has_vres),
        grid=grid, in_specs=in_specs, out_specs=out_specs, out_shape=out_shape,
        scratch_shapes=[pltpu.VMEM((HALO + ts, D_IN_PAD), F32)],
        compiler_params=pltpu.CompilerParams(dimension_semantics=("arbitrary", "arbitrary"),
                                             vmem_limit_bytes=VMEM_LIMIT),
        name="prep",
    )(x, *params, vfirst, *tail)


def _rwkv_kernel(r_ref, ld_ref, k_ref, v_ref, kk_ref, b_ref, y_ref, s_scr):
    c = pl.program_id(1)
    L = CHUNK
    n_chunks = r_ref.shape[0] // L

    @pl.when(c == 0)
    def _():
        s_scr[...] = jnp.zeros(s_scr.shape, F32)

    row = lax.broadcasted_iota(I32, (L, L), 0)
    col = lax.broadcasted_iota(I32, (L, L), 1)
    strict = col < row
    incl = col <= row
    tri = incl.astype(BF16)
    eye = (row == col).astype(F32)

    units = [(ci, h) for ci in range(n_chunks) for h in range(RWKV_HEADS)]
    per_chunk = []
    for ci in range(n_chunks):
        sl = pl.ds(ci * L, L)
        ld = ld_ref[sl, :]
        cin = _dot_rhs2(tri, ld)
        c_last = cin[L - 1:L, :]
        e_neg = jnp.exp(-cin)
        e_tail = jnp.exp(c_last - cin)
        kk = kk_ref[sl, :]
        bb = b_ref[sl, :]
        kx = k_ref[sl, :]
        per_chunk.append(dict(
            A=-kk * jnp.exp(cin - ld), R=r_ref[sl, :] * jnp.exp(cin), B=bb * e_neg, K=kx * e_neg,
            Bh=bb * e_tail, Kh=kx * e_tail, V=v_ref[sl, :], g_last=jnp.exp(c_last)))

    def part(name, u):
        ci, h = u
        return per_chunk[ci][name][:, h * HEAD_DIM:(h + 1) * HEAD_DIM]

    M = [_dot_nt(jnp.concatenate([part('A', u), part('R', u)], axis=0),
                 jnp.concatenate([part('B', u), part('K', u)], axis=0)) for u in units]
    m_ab = [jnp.where(strict, m[0:L, 0:L], 0.0) for m in M]
    m_ak = [jnp.where(strict, m[0:L, L:2 * L], 0.0) for m in M]
    m_rb = [jnp.where(incl, m[L:2 * L, 0:L], 0.0) for m in M]
    m_rk = [jnp.where(incl, m[L:2 * L, L:2 * L], 0.0) for m in M]
    MV = [_dot(m, part('V', u)) for m, u in zip(m_ak, units)]
    YK = [_dot(m, part('V', u)) for m, u in zip(m_rk, units)]
    T = [eye + m for m in m_ab]
    pw = m_ab
    for _ in range(5):
        pw = [_dot(p, p) for p in pw]
        T = [t + _dot(t, p) for t, p in zip(T, pw)]
    WU = [_dot(t, jnp.concatenate([part('A', u), mv], axis=1)) for t, u, mv in zip(T, units, MV)]
    GY = [_dot(m, wu) for m, wu in zip(m_rb, WU)]
    G = [part('R', u) + gy[:, 0:L] for u, gy in zip(units, GY)]
    Y0 = [gy[:, L:2 * L] + yk for gy, yk in zip(GY, YK)]
    WB = [_dot_tn(wu, part('Bh', u)) for wu, u in zip(WU, units)]
    VK = [_dot_tn(part('V', u), part('Kh', u)) for u in units]
    P = [wb[0:L, :] + eye * part('g_last', u) for wb, u in zip(WB, units)]
    Q = [wb[L:2 * L, :] + vk for wb, vk in zip(WB, VK)]

    state = [s_scr[h] for h in range(RWKV_HEADS)]
    for ci in range(n_chunks):
        base = ci * RWKV_HEADS
        ys = [_dot_nt(G[base + h], state[h]) + Y0[base + h] for h in range(RWKV_HEADS)]
        state = [_dot(state[h], P[base + h]) + Q[base + h] for h in range(RWKV_HEADS)]
        y_ref[pl.ds(ci * L, L), :] = jnp.concatenate(ys, axis=1)
    for h in range(RWKV_HEADS):
        s_scr[h] = state[h]


def _rwkv_call(r, ld, k, v, kk, b):
    B, S, W = r.shape
    ts = SEQ_ROWS
    spec = pl.BlockSpec((None, ts, W), lambda bi, c: (bi, c, 0))
    return pl.pallas_call(
        _rwkv_kernel, grid=(B, S // ts), in_specs=[spec] * 6, out_specs=spec,
        out_shape=jax.ShapeDtypeStruct((B, S, W), F32),
        scratch_shapes=[pltpu.VMEM((RWKV_HEADS, HEAD_DIM, HEAD_DIM), F32)],
        compiler_params=pltpu.CompilerParams(dimension_semantics=("arbitrary", "arbitrary"),
                                             vmem_limit_bytes=VMEM_LIMIT),
        name="rwkv",
    )(r, ld, k, v, kk, b)


def _mlstm_kernel(q_ref, k_ref, v_ref, g_ref, expand_ref, h_ref, c_scr, n_scr, m_scr):
    c = pl.program_id(1)
    L = CHUNK
    n_chunks = q_ref.shape[0] // L
    H = MLSTM_HEADS

    @pl.when(c == 0)
    def _():
        c_scr[...] = jnp.zeros(c_scr.shape, F32)
        n_scr[...] = jnp.zeros(n_scr.shape, F32)
        m_scr[...] = jnp.zeros(m_scr.shape, F32)

    row = lax.broadcasted_iota(I32, (L, L), 0)
    col = lax.broadcasted_iota(I32, (L, L), 1)
    incl = col <= row
    tri = incl.astype(BF16)

    units = [(ci, h) for ci in range(n_chunks) for h in range(H)]
    ig_rep, g_rep, gates_t, gcum_t = [], [], [], []
    for ci in range(n_chunks):
        gt = g_ref[pl.ds(ci * L, L), :]
        rep = _dot_lhs2(gt, expand_ref[...])
        ig_rep.append(rep[:, 0:MLSTM_WIDTH])
        g_rep.append(_dot_rhs2(tri, rep[:, MLSTM_WIDTH:]))
        gates_t.append(gt.T)
        gcum_t.append(_dot_rhs2(tri, gt).T)

    def part(ref, u):
        ci, h = u
        return ref[pl.ds(ci * L, L), h * HEAD_DIM:(h + 1) * HEAD_DIM]

    def head(x, h):
        return x[:, h * HEAD_DIM:(h + 1) * HEAD_DIM]

    ones = jnp.ones((L, HEAD_DIM), BF16)
    ig_c = [head(ig_rep[ci], h) for ci, h in units]
    g_c = [head(g_rep[ci], h) for ci, h in units]
    g_last = [g[L - 1:L, :] for g in g_c]
    d_log = [jnp.where(incl, g_c[i] - gcum_t[ci][H + h:H + h + 1, :] + gates_t[ci][h:h + 1, :], -jnp.inf)
             for i, (ci, h) in enumerate(units)]
    d_max = [jnp.max(d, axis=-1, keepdims=True) for d in d_log]
    qk = [_dot_nt(part(q_ref, u), part(k_ref, u)) for u in units]
    e = [gl - g + ig for gl, g, ig in zip(g_last, g_c, ig_c)]
    m_loc = [jnp.max(x, axis=0, keepdims=True) for x in e]
    wk = [part(k_ref, u) * jnp.exp(x - m) for u, x, m in zip(units, e, m_loc)]
    kv_loc = [_dot_tn(w, part(v_ref, u)) for w, u in zip(wk, units)]
    n_loc = [_dot_tn(w, ones) for w in wk]

    c_st = [c_scr[h] for h in range(H)]
    n_st = [n_scr[h] for h in range(H)]
    m_st = [m_scr[h:h + 1, 0:HEAD_DIM] for h in range(H)]
    c_prev, n_prev, m_prev = [], [], []
    for i, (ci, h) in enumerate(units):
        c_prev.append(c_st[h])
        n_prev.append(n_st[h])
        m_prev.append(m_st[h])
        m_new = jnp.maximum(g_last[i] + m_st[h], m_loc[i])
        a_old = jnp.exp(g_last[i] + m_st[h] - m_new)
        a_new = jnp.exp(m_loc[i] - m_new)
        c_st[h] = a_old * c_st[h] + a_new * kv_loc[i]
        n_st[h] = a_old * n_st[h] + a_new * n_loc[i]
        m_st[h] = m_new
    for h in range(H):
        c_scr[h] = c_st[h]
        n_scr[h] = n_st[h]
        m_scr[h:h + 1, 0:HEAD_DIM] = m_st[h]

    inter_log = [g + m for g, m in zip(g_c, m_prev)]
    m_j = [jnp.maximum(dm, il) for dm, il in zip(d_max, inter_log)]
    w_intra = [jnp.exp(d - m) * s for d, m, s in zip(d_log, m_j, qk)]
    w_inter = [jnp.exp(il - m) for il, m in zip(inter_log, m_j)]
    intra = [_dot(w, part(v_ref, u)) for w, u in zip(w_intra, units)]
    inter = [_dot(part(q_ref, u), c) for u, c in zip(units, c_prev)]
    den_a = [_dot(w, ones) for w in w_intra]
    den_b = [_dot(part(q_ref, u), n) for u, n in zip(units, n_prev)]
    outs = [(ia + wi * ie) / jnp.maximum(jnp.abs(da + wi * db), jnp.exp(-m))
            for ia, wi, ie, da, db, m in zip(intra, w_inter, inter, den_a, den_b, m_j)]
    for ci in range(n_chunks):
        h_ref[pl.ds(ci * L, L), :] = jnp.concatenate(outs[ci * H:(ci + 1) * H], axis=1)


def _gate_expand_matrix():
    lane = jnp.arange(LANES)[:, None]
    col = jnp.arange(2 * MLSTM_WIDTH)[None, :]
    src = jnp.where(col < MLSTM_WIDTH, col // HEAD_DIM, MLSTM_HEADS + (col - MLSTM_WIDTH) // HEAD_DIM)
    return (lane == src).astype(BF16)


def _mlstm_call(q, k, v, g):
    B, S, W = q.shape
    ts = SEQ_ROWS
    spec = pl.BlockSpec((None, ts, W), lambda bi, c: (bi, c, 0))
    gspec = pl.BlockSpec((None, ts, LANES), lambda bi, c: (bi, c, 0))
    espec = pl.BlockSpec((LANES, 2 * W), lambda bi, c: (0, 0))
    return pl.pallas_call(
        _mlstm_kernel, grid=(B, S // ts), in_specs=[spec, spec, spec, gspec, espec], out_specs=spec,
        out_shape=jax.ShapeDtypeStruct((B, S, W), F32),
        scratch_shapes=[pltpu.VMEM((MLSTM_HEADS, HEAD_DIM, HEAD_DIM), F32),
                        pltpu.VMEM((MLSTM_HEADS, HEAD_DIM, HEAD_DIM), F32),
                        pltpu.VMEM((8, LANES), F32)],
        compiler_params=pltpu.CompilerParams(dimension_semantics=("arbitrary", "arbitrary"),
                                             vmem_limit_bytes=VMEM_LIMIT),
        name="mlstm",
    )(q, k, v, g, _gate_expand_matrix())


def _layer_norm(z, g, b):
    mu = jnp.mean(z, axis=-1, keepdims=True)
    d = z - mu
    var = jnp.mean(d * d, axis=-1, keepdims=True)
    return d * lax.rsqrt(var + LN_EPS) * g + b


def _post_kernel(h_ref, ypool_ref, yr_ref, bonus_ref, g_ref, hm_ref, mo_ref,
                 lnxg_ref, lnxb_ref, ng_ref, wout_ref, ln1g_ref, ln1b_ref, rwt_ref, rb_ref, ones_ref,
                 h1_ref, gate_ref, route_ref, cnt_ref):
    ts = h_ref.shape[0]
    ones_bd = ones_ref[...]
    y_rwkv = (_head_norm(yr_ref[...], ones_bd, RWKV_GN_EPS) * lnxg_ref[...] + lnxb_ref[...]
              + bonus_ref[...]) * g_ref[...]
    y_ml = mo_ref[...] * (_head_norm(hm_ref[...], ones_bd, LN_EPS) * ng_ref[...])
    mix = (_dot(ypool_ref[...], wout_ref[0:POOL_WIDTH, :])
           + _dot(y_rwkv, wout_ref[POOL_WIDTH:POOL_WIDTH + RWKV_WIDTH, :])
           + _dot(y_ml, wout_ref[POOL_WIDTH + RWKV_WIDTH:, :]))
    h1 = _layer_norm(DEEPNORM_ALPHA * h_ref[...] + mix, ln1g_ref[...], ln1b_ref[...])
    h1_ref[...] = h1

    hh, hl = _split(h1)
    wh, wl = _split(rwt_ref[...])
    nt = lambda a, b: lax.dot_general(a, b, (((1,), (1,)), ((), ())), preferred_element_type=F32)
    logits = nt(wh, hh) + nt(wh, hl) + nt(wl, hh) + rb_ref[...]
    eidx = lax.broadcasted_iota(I32, (N_EXPERTS, ts), 0)
    vals = logits
    tops, hots, idxs = [], [], []
    for _ in range(TOP_K):
        mx = jnp.max(vals, axis=0, keepdims=True)
        idx = jnp.min(jnp.where(vals == mx, eidx, N_EXPERTS), axis=0, keepdims=True)
        hot = eidx == idx
        vals = jnp.where(hot, -jnp.inf, vals)
        tops.append(mx)
        hots.append(hot)
        idxs.append(idx)
    exps = [jnp.exp(t - tops[0]) for t in tops]
    denom = exps[0] + exps[1] + exps[2] + exps[3]
    gate_rows = [e / denom for e in exps]
    gate_ref[...] = jnp.concatenate(gate_rows + [jnp.zeros((8 - TOP_K, ts), F32)], axis=0)

    any_hot = (hots[0] | hots[1] | hots[2] | hots[3])
    hot_f = any_hot.astype(F32)
    r_i = lax.broadcasted_iota(I32, (ts, ts), 0)
    c_i = lax.broadcasted_iota(I32, (ts, ts), 1)
    before = (r_i < c_i).astype(BF16)
    within = jnp.dot(hot_f.astype(BF16), before, preferred_element_type=F32)
    n_e = jnp.broadcast_to(jnp.sum(hot_f, axis=1, keepdims=True), (N_EXPERTS, LANES))
    run_len = jnp.floor((n_e + (RUN_PIECE - 1)) * (1.0 / RUN_PIECE)) * RUN_PIECE
    e_r = lax.broadcasted_iota(I32, (N_EXPERTS, N_EXPERTS), 0)
    e_c = lax.broadcasted_iota(I32, (N_EXPERTS, N_EXPERTS), 1)
    lower = (e_c < e_r).astype(BF16)
    run_start = _dot_rhs2(lower, run_len)[:, 0:1]
    slots = [jnp.sum(jnp.where(hot, run_start + within, 0.0), axis=0, keepdims=True) for hot in hots]
    route_ref[...] = jnp.concatenate([p.astype(I32) for p in slots] + [jnp.zeros((8 - TOP_K, ts), I32)], axis=0)
    cnt_ref[...] = n_e


def _post_call(h, ypool, yr, bonus, g, hm, mo, lw):
    T = h.shape[0]
    ts = PREP_ROWS
    row = lambda c: pl.BlockSpec((ts, c), lambda i: (i, 0))
    colb = lambda r: pl.BlockSpec((r, ts), lambda i: (0, i))
    full = lambda a: pl.BlockSpec(a.shape, lambda i: (0,) * a.ndim)
    params = [lw['lnx_g'], lw['lnx_b'], lw['norm_g'], lw['w_out'], lw['ln1_g'], lw['ln1_b'],
              lw['router_wt'], lw['router_b'], lw['ones_bd']]
    in_specs = ([row(D_MODEL), row(POOL_WIDTH)] + [row(RWKV_WIDTH)] * 5 + [full(a) for a in params])
    out_shape = [jax.ShapeDtypeStruct((T, D_MODEL), F32), jax.ShapeDtypeStruct((8, T), F32),
                 jax.ShapeDtypeStruct((8, T), I32), jax.ShapeDtypeStruct((T // ts * N_EXPERTS, LANES), F32)]
    out_specs = [row(D_MODEL), colb(8), colb(8), pl.BlockSpec((N_EXPERTS, LANES), lambda i: (i, 0))]
    return pl.pallas_call(
        _post_kernel, grid=(T // ts,), in_specs=in_specs, out_specs=out_specs, out_shape=out_shape,
        compiler_params=pltpu.CompilerParams(dimension_semantics=("arbitrary",),
                                             vmem_limit_bytes=VMEM_LIMIT),
        name="post",
    )(h, ypool, yr, bonus, g, hm, mo, *params)


def _rows_copy(src_ref, src_row, dst_ref, dst_row, n_rows, sem):
    src_row, dst_row = [r if isinstance(r, int) else pl.multiple_of(r, SUBLANES) for r in (src_row, dst_row)]
    return pltpu.make_async_copy(src_ref.at[pl.ds(src_row, n_rows), :], dst_ref.at[pl.ds(dst_row, n_rows), :], sem)


def _stage_rows(ts):
    n = TOP_K * ts + N_EXPERTS * (RUN_PIECE - 1)
    return -(-n // SUBLANES) * SUBLANES


def _for_each_piece(n_pieces, body):
    lax.fori_loop(0, n_pieces, lambda j, c: (body(j), c)[1], 0)


def _run_copies(tab_ref, src_of, dst_of, sem, wait):
    for e in range(N_EXPERTS):
        n_rows = pl.multiple_of(tab_ref[0, e], SUBLANES)

        @pl.when(n_rows > 0)
        def _(e=e, n_rows=n_rows):
            src_ref, src_row = src_of(e)
            dst_ref, dst_row = dst_of(e)
            cp = _rows_copy(src_ref, src_row, dst_ref, dst_row, n_rows, sem)
            cp.wait() if wait else cp.start()


def _dispatch_kernel(tab_ref, prev_tab_ref, fill_ref, route_ref, gate_ref, h_ref, buf_ref,
                     sorted_scr, zero_scr, sems):
    step = pl.program_id(0)
    last = pl.num_programs(0) - 1
    slot = step % 2
    ts = h_ref.shape[0]
    n_sorted = sorted_scr.shape[1]

    @pl.when(step == 0)
    def _():
        zero_scr[...] = jnp.zeros(zero_scr.shape, F32)

        def fill(wait):
            for e in range(N_EXPERTS):
                def piece(j, e=e):
                    cp = _rows_copy(zero_scr, 0, buf_ref, fill_ref[0, e] + j * FILL_PIECE, FILL_PIECE, sems.at[0])
                    cp.wait() if wait else cp.start()
                _for_each_piece(fill_ref[0, N_EXPERTS + e], piece)

            def tail(j):
                cp = _rows_copy(zero_scr, 0, buf_ref, fill_ref[0, 2 * N_EXPERTS] + j * EXPERT_ROWS,
                                EXPERT_ROWS, sems.at[0])
                cp.wait() if wait else cp.start()
            _for_each_piece(fill_ref[0, 2 * N_EXPERTS + 1], tail)

        fill(False)
        fill(True)

    pos = lax.broadcasted_iota(I32, (n_sorted, ts), 0)
    hot = None
    weight = None
    for kslot in range(TOP_K):
        eq = pos == route_ref[kslot:kslot + 1, :]
        w = jnp.where(eq, gate_ref[kslot:kslot + 1, :], 0.0)
        hot = eq if hot is None else (hot | eq)
        weight = w if weight is None else weight + w
    sorted_scr[slot, :, 0:D_MODEL] = jnp.dot(hot.astype(BF16), h_ref[...].astype(BF16),
                                             preferred_element_type=F32)
    w_hi, w_lo = _split(weight)
    w_lo2 = (weight - w_hi.astype(F32) - w_lo.astype(F32)).astype(BF16)
    ones = jnp.ones((ts, LANES), BF16)
    sorted_scr[slot, :, D_MODEL:BUF_WIDTH] = (jnp.dot(w_hi, ones, preferred_element_type=F32)
                                              + jnp.dot(w_lo, ones, preferred_element_type=F32)
                                              + jnp.dot(w_lo2, ones, preferred_element_type=F32))

    def copies(tab, which, wait):
        _run_copies(tab, lambda e: (sorted_scr.at[which], tab[0, N_EXPERTS + e]),
                    lambda e: (buf_ref, tab[0, 2 * N_EXPERTS + e]), sems.at[which], wait)

    @pl.when(step > 0)
    def _():
        copies(prev_tab_ref, 1 - slot, True)

    copies(tab_ref, slot, False)

    @pl.when(step == last)
    def _():
        copies(tab_ref, slot, True)


def _dispatch_call(tab_d, fill_tab, route8, gate8, h1, n_rows):
    T = h1.shape[0]
    ts = PREP_ROWS
    smem_tile = pl.BlockSpec((None, 1, LANES), lambda i: (i, 0, 0), memory_space=pltpu.SMEM)
    smem_prev = pl.BlockSpec((None, 1, LANES), lambda i: (jnp.maximum(i - 1, 0), 0, 0), memory_space=pltpu.SMEM)
    return pl.pallas_call(
        _dispatch_kernel, grid=(T // ts,),
        in_specs=[smem_tile, smem_prev,
                  pl.BlockSpec((1, LANES), lambda i: (0, 0), memory_space=pltpu.SMEM),
                  pl.BlockSpec((8, ts), lambda i: (0, i)),
                  pl.BlockSpec((8, ts), lambda i: (0, i)),
                  pl.BlockSpec((ts, D_MODEL), lambda i: (i, 0))],
        out_specs=pl.BlockSpec(memory_space=pl.ANY),
        out_shape=jax.ShapeDtypeStruct((n_rows, BUF_WIDTH), F32),
        scratch_shapes=[pltpu.VMEM((2, _stage_rows(ts), BUF_WIDTH), F32),
                        pltpu.VMEM((EXPERT_ROWS, BUF_WIDTH), F32),
                        pltpu.SemaphoreType.DMA((2,))],
        compiler_params=pltpu.CompilerParams(dimension_semantics=("arbitrary",),
                                             vmem_limit_bytes=VMEM_LIMIT),
        name="dispatch",
    )(tab_d, tab_d, fill_tab, route8, gate8, h1)


def _expert_kernel(be_ref, nb_ref, run_ref, next_ref, x_ref, wgu_hbm, bgu_ref, wdn_hbm, bdn_ref, o_ref,
                   wgu_f32, wdn_f32, wgu_scr, wdn_scr, sems, *, layer):
    j = pl.program_id(0)
    used = j < nb_ref[0]
    changed = jnp.logical_or(j == 0, be_ref[j] != be_ref[jnp.maximum(j - 1, 0)])

    def fetch(expert, slot):
        return (pltpu.make_async_copy(wgu_hbm.at[layer, expert], wgu_f32.at[slot], sems.at[0, slot]),
                pltpu.make_async_copy(wdn_hbm.at[layer, expert], wdn_f32.at[slot], sems.at[1, slot]))

    @pl.when(j == 0)
    def _():
        for cp in fetch(be_ref[0], 0):
            cp.start()

    @pl.when(jnp.logical_and(used, changed))
    def _():
        slot = run_ref[j] % 2
        for cp in fetch(be_ref[j], slot):
            cp.wait()
        wgu_scr[...] = wgu_f32[slot].astype(BF16)
        wdn_scr[...] = wdn_f32[slot].astype(BF16)

        @pl.when(next_ref[j] >= 0)
        def _():
            for cp in fetch(next_ref[j], 1 - slot):
                cp.start()

    @pl.when(used)
    def _():
        gu = jnp.dot(x_ref[:, 0:D_MODEL].astype(BF16), wgu_scr[...], preferred_element_type=F32) + bgu_ref[...]
        glu = jnp.minimum(gu[:, 0:D_FF], SWIGLU_LIMIT)
        lin = jnp.clip(gu[:, D_FF:], -SWIGLU_LIMIT, SWIGLU_LIMIT)
        act = glu * _sigmoid(SWIGLU_ALPHA * glu) * (lin + 1.0)
        y = jnp.dot(act.astype(BF16), wdn_scr[...], preferred_element_type=F32) + bdn_ref[...]
        gate = x_ref[:, D_MODEL:BUF_WIDTH]
        o_ref[...] = y * jnp.concatenate([gate] * (D_MODEL // LANES), axis=1)

    @pl.when(jnp.logical_not(used))
    def _():
        o_ref[...] = jnp.zeros(o_ref.shape, F32)


def _expert_call(block_e, n_used, buf, lw):
    n_rows = buf.shape[0]
    rows = EXPERT_ROWS
    n_blocks = n_rows // rows
    j = jnp.arange(n_blocks, dtype=I32)
    first = (j < n_used[0]) & ((j == 0) | (block_e != jnp.roll(block_e, 1)))
    run_idx = jnp.cumsum(first.astype(I32)) - 1
    later_first = lax.cummin(jnp.where(first, j, n_blocks)[::-1])[::-1]
    next_start = jnp.concatenate([later_first[1:], jnp.full((1,), n_blocks, I32)])
    next_e = jnp.where(next_start < n_blocks, block_e[jnp.minimum(next_start, n_blocks - 1)], -1).astype(I32)

    def blk(j, be, nb, run, nxt):
        return jnp.minimum(j, nb[0] - 1)

    layer = lw['layer']
    grid_spec = pltpu.PrefetchScalarGridSpec(
        num_scalar_prefetch=4, grid=(n_blocks,),
        in_specs=[pl.BlockSpec((rows, BUF_WIDTH), lambda j, *s: (blk(j, *s), 0)),
                  pl.BlockSpec(memory_space=pl.ANY),
                  pl.BlockSpec((None, None, 1, 2 * D_FF), lambda j, be, *s: (layer, be[j], 0, 0)),
                  pl.BlockSpec(memory_space=pl.ANY),
                  pl.BlockSpec((None, None, 1, D_MODEL), lambda j, be, *s: (layer, be[j], 0, 0))],
        out_specs=pl.BlockSpec((rows, D_MODEL), lambda j, *s: (j, 0)),
        scratch_shapes=[pltpu.VMEM((2, D_MODEL, 2 * D_FF), F32), pltpu.VMEM((2, D_FF, D_MODEL), F32),
                        pltpu.VMEM((D_MODEL, 2 * D_FF), BF16), pltpu.VMEM((D_FF, D_MODEL), BF16),
                        pltpu.SemaphoreType.DMA((2, 2))])
    return pl.pallas_call(
        functools.partial(_expert_kernel, layer=layer), grid_spec=grid_spec,
        out_shape=jax.ShapeDtypeStruct((n_rows, D_MODEL), F32),
        compiler_params=pltpu.CompilerParams(dimension_semantics=("arbitrary",),
                                             vmem_limit_bytes=56 * 1024 * 1024),
        name="expert",
    )(block_e, n_used, run_idx, next_e, buf, lw['w_gate_up'], lw['b_gate_up'], lw['w_down'], lw['b_down'])


def _combine_kernel(tab_ref, next_tab_ref, route_ref, h_ref, out_hbm_ref, ln2g_ref, ln2b_ref, o_ref,
                    stage_scr, sems):
    step = pl.program_id(0)
    last = pl.num_programs(0) - 1
    slot = step % 2
    ts = h_ref.shape[0]
    n_stage = stage_scr.shape[1]

    def copies(tab, which, wait):
        _run_copies(tab, lambda e: (out_hbm_ref, tab[0, 2 * N_EXPERTS + e]),
                    lambda e: (stage_scr.at[which], tab[0, N_EXPERTS + e]), sems.at[which], wait)

    @pl.when(step == 0)
    def _():
        stage_scr[...] = jnp.zeros(stage_scr.shape, F32)
        copies(tab_ref, 0, False)

    @pl.when(step < last)
    def _():
        copies(next_tab_ref, 1 - slot, False)

    slot_f = route_ref[...].astype(F32)
    slot_cols = jnp.concatenate([slot_f, jnp.zeros((LANES - 8, ts), F32)], axis=0).T
    lane = lax.broadcasted_iota(I32, (ts, n_stage), 1).astype(F32)
    sel = None
    for kslot in range(TOP_K):
        eq = lane == slot_cols[:, kslot:kslot + 1]
        sel = eq if sel is None else (sel | eq)
    copies(tab_ref, slot, True)
    y = jnp.dot(sel.astype(BF16), stage_scr[slot].astype(BF16), preferred_element_type=F32)
    o_ref[...] = _layer_norm(DEEPNORM_ALPHA * h_ref[...] + y, ln2g_ref[...], ln2b_ref[...])


def _combine_call(tab_c, route8, h1, out_rows, lw):
    T = h1.shape[0]
    ts = PREP_ROWS
    n_tiles = T // ts
    full = lambda a: pl.BlockSpec(a.shape, lambda i: (0,) * a.ndim)
    n_stage = _stage_rows(ts)
    return pl.pallas_call(
        _combine_kernel, grid=(n_tiles,),
        in_specs=[pl.BlockSpec((None, 1, LANES), lambda i: (i, 0, 0), memory_space=pltpu.SMEM),
                  pl.BlockSpec((None, 1, LANES), lambda i: (jnp.minimum(i + 1, n_tiles - 1), 0, 0),
                               memory_space=pltpu.SMEM),
                  pl.BlockSpec((8, ts), lambda i: (0, i)),
                  pl.BlockSpec((ts, D_MODEL), lambda i: (i, 0)),
                  pl.BlockSpec(memory_space=pl.ANY),
                  full(lw['ln2_g']), full(lw['ln2_b'])],
        out_specs=pl.BlockSpec((ts, D_MODEL), lambda i: (i, 0)),
        out_shape=jax.ShapeDtypeStruct((T, D_MODEL), F32),
        scratch_shapes=[pltpu.VMEM((2, n_stage, D_MODEL), F32), pltpu.SemaphoreType.DMA((2,))],
        compiler_params=pltpu.CompilerParams(dimension_semantics=("arbitrary",),
                                             vmem_limit_bytes=VMEM_LIMIT),
        name="combine",
    )(tab_c, tab_c, route8, h1, out_rows, lw['ln2_g'], lw['ln2_b'])


def _block_diag_ones(width):
    hid = jnp.arange(width) // HEAD_DIM
    return (hid[:, None] == hid[None, :]).astype(BF16)


def _layer_params(l, w_in, pool_w, pool_scale, rwkv_mu, rwkv_w0, rwkv_w2, rwkv_a0, rwkv_a2, rwkv_g2,
                  rwkv_kk_scale, rwkv_ka, rwkv_rk, rwkv_lnx_g, rwkv_lnx_b, rwkv_v0, rwkv_v1, rwkv_v2,
                  mlstm_conv_w, mlstm_conv_b, mlstm_b_i, mlstm_b_f, mlstm_norm_g, w_out, ln1_g, ln1_b,
                  router_w, router_b, w_gate_up, b_gate_up, w_down, b_down, ln2_g, ln2_b):
    row = lambda a: a.reshape(1, -1).astype(F32)
    pad_cols = D_IN_PAD - D_IN
    if l > 0:
        extra = jnp.concatenate([rwkv_v1[l - 1], jnp.zeros((D_MODEL, pad_cols - VRES_LORA), F32)], axis=1)
        v0 = row(rwkv_v0[l - 1])
        v2 = jnp.zeros((LANES, RWKV_WIDTH), F32).at[VRES_OFF:VRES_OFF + VRES_LORA].set(rwkv_v2[l - 1])
    else:
        extra = jnp.zeros((D_MODEL, pad_cols), F32)
        v0 = jnp.zeros((1, RWKV_WIDTH), F32)
        v2 = jnp.zeros((LANES, RWKV_WIDTH), F32)
    pw = jnp.zeros((POOL_WIDTH, POOL_WIDTH), F32)
    for gi in range(len(POOL_WINDOWS)):
        sl = slice(gi * POOL_GROUP, (gi + 1) * POOL_GROUP)
        pw = pw.at[sl, sl].set(pool_w[l, gi])
    zero_lora = jnp.zeros((DECAY_LORA, RWKV_WIDTH), F32)
    gate_bias = jnp.zeros((1, LANES), F32).at[0, 0:MLSTM_HEADS].set(mlstm_b_i[l])
    gate_bias = gate_bias.at[0, MLSTM_HEADS:2 * MLSTM_HEADS].set(mlstm_b_f[l])
    return {
        'w_in': jnp.concatenate([w_in[l], extra], axis=1).astype(BF16),
        'pool_w': pw.astype(BF16), 'pool_scale': row(pool_scale[l]), 'mu': row(rwkv_mu[l]),
        'w0': row(rwkv_w0[l]), 'w2': jnp.concatenate([rwkv_w2[l], zero_lora], axis=0).astype(BF16),
        'a0': row(rwkv_a0[l]), 'a2': jnp.concatenate([zero_lora, rwkv_a2[l]], axis=0).astype(BF16),
        'g2': rwkv_g2[l].astype(BF16), 'kk_scale': row(rwkv_kk_scale[l]), 'ka': row(rwkv_ka[l]),
        'rk': row(rwkv_rk[l]), 'v0': v0, 'v2': v2.astype(BF16),
        'conv_w': mlstm_conv_w[l], 'conv_b': row(mlstm_conv_b[l]), 'gate_bias': gate_bias,
        'ones_bd': _block_diag_ones(RWKV_WIDTH),
        'lnx_g': row(rwkv_lnx_g[l]), 'lnx_b': row(rwkv_lnx_b[l]), 'norm_g': row(mlstm_norm_g[l]),
        'w_out': w_out[l].astype(BF16), 'ln1_g': row(ln1_g[l]), 'ln1_b': row(ln1_b[l]),
        'router_wt': router_w[l].T, 'router_b': router_b[l].reshape(N_EXPERTS, 1),
        'layer': l, 'w_gate_up': w_gate_up, 'b_gate_up': b_gate_up.reshape(-1, N_EXPERTS, 1, 2 * D_FF),
        'w_down': w_down, 'b_down': b_down.reshape(-1, N_EXPERTS, 1, D_MODEL),
        'ln2_g': row(ln2_g[l]), 'ln2_b': row(ln2_b[l]),
    }


def _layer(h, v_first, lw, *, has_vres):
    B, S, _ = h.shape
    T = B * S
    outs = _prep_call(h, lw, v_first, has_vres=has_vres)
    ypool, r, ld, k, v, kk, b, g, bonus, mq, mk, mv, mo, mg = outs
    yr = _rwkv_call(r, ld, k, v, kk, b)
    hm = _mlstm_call(mq, mk, mv, mg)
    flat = lambda a: a.reshape(T, a.shape[-1])
    h_flat = flat(h)
    h1, gate8, route8, cnt = _post_call(h_flat, flat(ypool), flat(yr), flat(bonus), flat(g), flat(hm),
                                        flat(mo), lw)
    n_tiles = T // PREP_ROWS
    cnt = cnt.reshape(n_tiles, N_EXPERTS, LANES)[:, :, 0].astype(I32)
    rows = EXPERT_ROWS
    run_len = (cnt + RUN_PIECE - 1) // RUN_PIECE * RUN_PIECE
    region = jnp.sum(run_len, axis=0)
    padded = (region + rows - 1) // rows * rows
    pad_end = jnp.cumsum(padded)
    pad_start = pad_end - padded
    n_blocks = -(-(T * TOP_K + n_tiles * N_EXPERTS * (RUN_PIECE - 1)) // rows) + N_EXPERTS
    n_used = (pad_end[-1] // rows).astype(I32).reshape(1)
    starts = jnp.minimum(jnp.arange(n_blocks, dtype=I32), n_used[0] - 1) * rows
    block_e = jnp.sum((pad_end[None, :] <= starts[:, None]).astype(I32), axis=1)
    run_src = jnp.cumsum(run_len, axis=1) - run_len
    run_dst = pad_start[None, :] + jnp.cumsum(run_len, axis=0) - run_len
    run_tab = jnp.concatenate([run_len, run_src, run_dst,
                               jnp.zeros((n_tiles, LANES - 3 * N_EXPERTS), I32)], axis=1).reshape(n_tiles, 1, LANES)
    fill_start = (pad_start + region) // FILL_PIECE * FILL_PIECE
    fill_tab = jnp.concatenate([fill_start, (pad_end - fill_start) // FILL_PIECE,
                                pad_end[-1:], n_blocks - n_used,
                                jnp.zeros((LANES - 2 * N_EXPERTS - 2,), I32)]).reshape(1, LANES)
    buf = _dispatch_call(run_tab, fill_tab, route8, gate8, h1, n_blocks * rows)
    out_rows = _expert_call(block_e, n_used, buf, lw)
    h2 = _combine_call(run_tab, route8, h1, out_rows, lw)
    return h2.reshape(B, S, D_MODEL), v


def kernel(x, w_in, pool_w, pool_scale, rwkv_mu, rwkv_w0, rwkv_w2, rwkv_a0, rwkv_a2, rwkv_g2, rwkv_kk_scale, rwkv_ka, rwkv_rk, rwkv_lnx_g, rwkv_lnx_b, rwkv_v0, rwkv_v1, rwkv_v2, mlstm_conv_w, mlstm_conv_b, mlstm_b_i, mlstm_b_f, mlstm_norm_g, w_out, ln1_g, ln1_b, router_w, router_b, w_gate_up, b_gate_up, w_down, b_down, ln2_g, ln2_b):
    weights = (w_in, pool_w, pool_scale, rwkv_mu, rwkv_w0, rwkv_w2, rwkv_a0, rwkv_a2, rwkv_g2, rwkv_kk_scale,
               rwkv_ka, rwkv_rk, rwkv_lnx_g, rwkv_lnx_b, rwkv_v0, rwkv_v1, rwkv_v2, mlstm_conv_w, mlstm_conv_b,
               mlstm_b_i, mlstm_b_f, mlstm_norm_g, w_out, ln1_g, ln1_b, router_w, router_b, w_gate_up,
               b_gate_up, w_down, b_down, ln2_g, ln2_b)
    h = x
    v_first = jnp.zeros(x.shape[:2] + (RWKV_WIDTH,), F32)
    for l in range(w_in.shape[0]):
        lw = _layer_params(l, *weights)
        h, v_l = _layer(h, v_first, lw, has_vres=l > 0)
        if l == 0:
            v_first = v_l
    return h
```

```python
import functools

import jax
import jax.numpy as jnp
from jax import lax
from jax.experimental import pallas as pl
from jax.experimental.pallas import tpu as pltpu

F32 = jnp.float32
BF16 = jnp.bfloat16
I32 = jnp.int32

D_MODEL = 1024
HEAD_DIM = 64
POOL_WINDOWS = (2, 4, 8, 16)
POOL_WIDTH = 256
POOL_GROUP = 64
RWKV_WIDTH = 384
RWKV_HEADS = 6
DECAY_LORA = 64
ICLR_LORA = 64
GATE_LORA = 128
VRES_LORA = 32
RWKV_GN_EPS = 64e-5
RWKV_COLS = 3 * RWKV_WIDTH + DECAY_LORA + ICLR_LORA + GATE_LORA
MLSTM_WIDTH = 384
MLSTM_HEADS = 6
MLSTM_CONV = 4
MLSTM_COLS = 4 * MLSTM_WIDTH + 2 * MLSTM_HEADS
D_IN = POOL_WIDTH + RWKV_COLS + MLSTM_COLS
N_EXPERTS = 32
TOP_K = 4
D_FF = D_MODEL
SWIGLU_LIMIT = 7.0
SWIGLU_ALPHA = 1.702
LN_EPS = 1e-5
DEPTH = 2
DEEPNORM_ALPHA = (2 * DEPTH) ** 0.25

LANES = 128
D_IN_PAD = 3328
RWKV_OFF = POOL_WIDTH
MLSTM_OFF = POOL_WIDTH + RWKV_COLS
GATE_OFF = MLSTM_OFF + 4 * MLSTM_WIDTH
VRES_OFF = 2 * MLSTM_HEADS
HALO = 16
CHUNK = 64

PREP_ROWS = 256
SEQ_ROWS = 512
EXPERT_ROWS = 512
BUF_WIDTH = D_MODEL + LANES
SUBLANES = 8
RUN_PIECE = SUBLANES
FILL_PIECE = 64
VMEM_LIMIT = 48 * 1024 * 1024


def _dot(a, b):
    return jnp.dot(a.astype(BF16), b.astype(BF16), preferred_element_type=F32)


def _dot_nt(a, b):
    return lax.dot_general(a.astype(BF16), b.astype(BF16), (((1,), (1,)), ((), ())),
                           preferred_element_type=F32)


def _dot_tn(a, b):
    return lax.dot_general(a.astype(BF16), b.astype(BF16), (((0,), (0,)), ((), ())),
                           preferred_element_type=F32)


def _split(x):
    hi = x.astype(BF16)
    lo = (x - hi.astype(F32)).astype(BF16)
    return hi, lo


def _dot_lhs2(a, b_bf16):
    hi, lo = _split(a)
    return (jnp.dot(hi, b_bf16, preferred_element_type=F32)
            + jnp.dot(lo, b_bf16, preferred_element_type=F32))


def _dot_rhs2(a_bf16, b):
    hi, lo = _split(b)
    return (jnp.dot(a_bf16, hi, preferred_element_type=F32)
            + jnp.dot(a_bf16, lo, preferred_element_type=F32))


def _sigmoid(x):
    return 1.0 / (1.0 + jnp.exp(-x))


def _softplus(x):
    return jnp.maximum(x, 0.0) + jnp.log(1.0 + jnp.exp(-jnp.abs(x)))


def _head_norm(y, ones_bd, eps):
    inv = 1.0 / HEAD_DIM
    mean = _dot_lhs2(y, ones_bd) * inv
    d = y - mean
    var = _dot_lhs2(d * d, ones_bd) * inv
    return d * lax.rsqrt(var + eps)


def _prep_kernel(x_ref, w_ref, poolw_ref, pscale_ref, mu_ref, w0_ref, w2_ref, a0_ref, a2_ref, g2_ref,
                 kks_ref, ka_ref, rk_ref, v0_ref, v2_ref, vfirst_ref, cw_ref, cb_ref, gbias_ref, ones_ref,
                 ypool_ref, r_ref, ld_ref, k_ref, v_ref, kk_ref, b_ref, g_ref, bonus_ref,
                 mq_ref, mk_ref, mv_ref, mo_ref, mg_ref,
                 p_scr, *, has_vres):
    i = pl.program_id(1)
    ts = x_ref.shape[0]

    @pl.when(i == 0)
    def _():
        p_scr[0:HALO, :] = jnp.zeros((HALO, D_IN_PAD), F32)

    @pl.when(i > 0)
    def _():
        p_scr[0:HALO, :] = p_scr[ts:ts + HALO, :]

    p_scr[HALO:HALO + ts, :] = jnp.dot(x_ref[...].astype(BF16), w_ref[...], preferred_element_type=F32)

    def rows(shift, c0, c1):
        return p_scr[HALO - shift:HALO - shift + ts, c0:c1]

    u = rows(0, 0, POOL_WIDTH)
    acc = u
    sums = {}
    for s in range(1, POOL_WINDOWS[-1]):
        acc = acc + rows(s, 0, POOL_WIDTH)
        if s + 1 in POOL_WINDOWS:
            sums[s + 1] = acc
    pos = (i * ts + lax.broadcasted_iota(I32, (ts, 1), 0) + 1).astype(F32)
    lane = lax.broadcasted_iota(I32, (ts, POOL_WIDTH), 1)
    d = None
    for gi, win in reversed(list(enumerate(POOL_WINDOWS))):
        dg = sums[win] / jnp.minimum(pos, float(win))
        d = dg if d is None else jnp.where(lane < (gi + 1) * POOL_GROUP, dg, d)
    d = d - u
    ypool_ref[...] = _dot(d, poolw_ref[...]) * pscale_ref[...]

    cur = rows(0, RWKV_OFF, RWKV_OFF + RWKV_COLS)
    prev = rows(1, RWKV_OFF, RWKV_OFF + RWKV_COLS)
    pf = cur + mu_ref[...] * (prev - cur)
    W = RWKV_WIDTH
    r = pf[:, 0:W]
    k = pf[:, W:2 * W]
    v = pf[:, 2 * W:3 * W]
    z = pf[:, 3 * W:3 * W + LANES]
    gd = pf[:, 3 * W + LANES:3 * W + 2 * LANES]
    w_log = -_softplus(-(w0_ref[...] + _dot(jnp.tanh(z), w2_ref[...]))) - 0.5
    ld_ref[...] = -jnp.exp(w_log)
    a = _sigmoid(a0_ref[...] + _dot(z, a2_ref[...]))
    g_ref[...] = _dot(_sigmoid(gd), g2_ref[...])
    gates = rows(0, GATE_OFF, GATE_OFF + LANES)
    if has_vres:
        v_gate = _sigmoid(v0_ref[...] + _dot(gates, v2_ref[...]))
        v = v + (vfirst_ref[...] - v) * v_gate
    ones_bd = ones_ref[...]
    kk = k * kks_ref[...]
    kk = kk / jnp.maximum(jnp.sqrt(_dot_lhs2(kk * kk, ones_bd)), 1e-12)
    k = k * (1.0 + (a - 1.0) * ka_ref[...])
    r_ref[...] = r
    k_ref[...] = k
    v_ref[...] = v
    kk_ref[...] = kk
    b_ref[...] = kk * a
    bonus_ref[...] = _dot_lhs2(r * k * rk_ref[...], ones_bd) * v

    qk = cb_ref[...] + rows(0, MLSTM_OFF, MLSTM_OFF + 2 * MLSTM_WIDTH) * cw_ref[MLSTM_CONV - 1:MLSTM_CONV, :]
    for tap in range(MLSTM_CONV - 1):
        shift = MLSTM_CONV - 1 - tap
        qk = qk + rows(shift, MLSTM_OFF, MLSTM_OFF + 2 * MLSTM_WIDTH) * cw_ref[tap:tap + 1, :]
    qk = qk * _sigmoid(qk)
    mq_ref[...] = qk[:, 0:MLSTM_WIDTH] * (HEAD_DIM ** -0.5)
    mk_ref[...] = qk[:, MLSTM_WIDTH:]
    mv_ref[...] = rows(0, MLSTM_OFF + 2 * MLSTM_WIDTH, MLSTM_OFF + 3 * MLSTM_WIDTH)
    mo_ref[...] = _sigmoid(rows(0, MLSTM_OFF + 3 * MLSTM_WIDTH, MLSTM_OFF + 4 * MLSTM_WIDTH))
    gb = gates + gbias_ref[...]
    glane = lax.broadcasted_iota(I32, (ts, LANES), 1)
    mg_ref[...] = jnp.where(glane < MLSTM_HEADS, gb, -_softplus(-gb))


def _prep_call(x, lw, vfirst, *, has_vres):
    B, S, _ = x.shape
    ts = PREP_ROWS
    grid = (B, S // ts)
    row3 = lambda c: pl.BlockSpec((None, ts, c), lambda b, i: (b, i, 0))
    full = lambda a: pl.BlockSpec(a.shape, lambda b, i: (0,) * a.ndim)
    params = [lw['w_in'], lw['pool_w'], lw['pool_scale'], lw['mu'], lw['w0'], lw['w2'], lw['a0'], lw['a2'],
              lw['g2'], lw['kk_scale'], lw['ka'], lw['rk'], lw['v0'], lw['v2']]
    tail = [lw['conv_w'], lw['conv_b'], lw['gate_bias'], lw['ones_bd']]
    in_specs = ([row3(D_MODEL)] + [full(a) for a in params] + [row3(RWKV_WIDTH)] + [full(a) for a in tail])
    widths = [POOL_WIDTH] + [RWKV_WIDTH] * 8 + [MLSTM_WIDTH] * 4 + [LANES]
    out_shape = [jax.ShapeDtypeStruct((B, S, c), F32) for c in widths]
    out_specs = [row3(c) for c in widths]
    return pl.pallas_call(
        functools.partial(_prep_kernel, has_vres=has_vres),
        grid=grid, in_specs=in_specs, out_specs=out_specs, out_shape=out_shape,
        scratch_shapes=[pltpu.VMEM((HALO + ts, D_IN_PAD), F32)],
        compiler_params=pltpu.CompilerParams(dimension_semantics=("arbitrary", "arbitrary"),
                                             vmem_limit_bytes=VMEM_LIMIT),
        name="prep",
    )(x, *params, vfirst, *tail)


def _rwkv_kernel(r_ref, ld_ref, k_ref, v_ref, kk_ref, b_ref, y_ref, s_scr):
    c = pl.program_id(1)
    L = CHUNK
    n_chunks = r_ref.shape[0] // L

    @pl.when(c == 0)
    def _():
        s_scr[...] = jnp.zeros(s_scr.shape, F32)

    row = lax.broadcasted_iota(I32, (L, L), 0)
    col = lax.broadcasted_iota(I32, (L, L), 1)
    tri = (col <= row).astype(BF16)
    row_l = lax.broadcasted_iota(I32, (L, LANES), 0)
    col_l = lax.broadcasted_iota(I32, (L, LANES), 1) % HEAD_DIM
    strict = col_l < row_l
    incl = col_l <= row_l
    eye = (col_l == row_l).astype(F32)
    r2 = lax.broadcasted_iota(I32, (LANES, LANES), 0)
    c2 = lax.broadcasted_iota(I32, (LANES, LANES), 1)
    same_head = (r2 // HEAD_DIM) == (c2 // HEAD_DIM)
    eye2 = (r2 == c2).astype(F32)
    low_lanes = lax.broadcasted_iota(I32, (L, LANES), 1) < HEAD_DIM
    n_pairs = RWKV_HEADS // 2

    def block_diag(x):
        xb = x.astype(BF16)
        return jnp.where(same_head, jnp.concatenate([xb, xb], axis=0), jnp.zeros((), BF16))

    def diag_blocks(z):
        return jnp.where(low_lanes, z[0:L, :], z[L:2 * L, :])

    units = [(ci, p) for ci in range(n_chunks) for p in range(n_pairs)]
    per_chunk = []
    for ci in range(n_chunks):
        sl = pl.ds(ci * L, L)
        ld = ld_ref[sl, :]
        cin = _dot_rhs2(tri, ld)
        c_last = cin[L - 1:L, :]
        e_neg = jnp.exp(-cin)
        e_tail = jnp.exp(c_last - cin)
        kk = kk_ref[sl, :]
        bb = b_ref[sl, :]
        kx = k_ref[sl, :]
        per_chunk.append(dict(
            A=-kk * jnp.exp(cin - ld), R=r_ref[sl, :] * jnp.exp(cin), B=bb * e_neg, K=kx * e_neg,
            Bh=bb * e_tail, Kh=kx * e_tail, V=v_ref[sl, :], g_last=jnp.exp(c_last)))

    def part(name, u):
        ci, p = u
        return per_chunk[ci][name][:, p * LANES:(p + 1) * LANES]

    M = [_dot_nt(jnp.concatenate([part('A', u), part('R', u)], axis=0),
                 jnp.concatenate([block_diag(part('B', u)), block_diag(part('K', u))], axis=0)) for u in units]
    m_ab = [jnp.where(strict, m[0:L, 0:LANES], 0.0) for m in M]
    m_ak = [jnp.where(strict, m[0:L, LANES:], 0.0) for m in M]
    m_rb = [jnp.where(incl, m[L:, 0:LANES], 0.0) for m in M]
    m_rk = [jnp.where(incl, m[L:, LANES:], 0.0) for m in M]
    MVYK = [_dot(jnp.concatenate([ak, rk], axis=0), block_diag(part('V', u)))
            for ak, rk, u in zip(m_ak, m_rk, units)]
    MV = [x[0:L, :] for x in MVYK]
    YK = [x[L:, :] for x in MVYK]
    T = [eye + m for m in m_ab]
    pw = [_dot(m, block_diag(m)) for m in m_ab]
    for _ in range(4):
        both = [_dot(jnp.concatenate([t, p], axis=0), block_diag(p)) for t, p in zip(T, pw)]
        T = [t + x[0:L, :] for t, x in zip(T, both)]
        pw = [x[L:, :] for x in both]
    T = [t + _dot(t, block_diag(p)) for t, p in zip(T, pw)]
    WU = [_dot(t, jnp.concatenate([block_diag(part('A', u)), block_diag(mv)], axis=1))
          for t, u, mv in zip(T, units, MV)]
    GY = [_dot(m, jnp.concatenate([block_diag(wu[:, 0:LANES]), block_diag(wu[:, LANES:])], axis=1))
          for m, wu in zip(m_rb, WU)]
    G = [part('R', u) + gy[:, 0:LANES] for u, gy in zip(units, GY)]
    Y0 = [gy[:, LANES:] + yk for gy, yk in zip(GY, YK)]
    P = [jnp.where(same_head, _dot_tn(wu[:, 0:LANES], part('Bh', u)), 0.0) + eye2 * part('g_last', u)
         for wu, u in zip(WU, units)]
    Q = [diag_blocks(_dot_tn(jnp.concatenate([wu[:, LANES:], part('V', u)], axis=0),
                             jnp.concatenate([part('Bh', u), part('Kh', u)], axis=0)))
         for wu, u in zip(WU, units)]

    state = [s_scr[p] for p in range(n_pairs)]
    for ci in range(n_chunks):
        base = ci * n_pairs
        ys = [_dot_nt(G[base + p], block_diag(state[p])) + Y0[base + p] for p in range(n_pairs)]
        state = [_dot(state[p], P[base + p]) + Q[base + p] for p in range(n_pairs)]
        y_ref[pl.ds(ci * L, L), :] = jnp.concatenate(ys, axis=1)
    for p in range(n_pairs):
        s_scr[p] = state[p]


def _rwkv_call(r, ld, k, v, kk, b):
    B, S, W = r.shape
    ts = SEQ_ROWS
    spec = pl.BlockSpec((None, ts, W), lambda bi, c: (bi, c, 0))
    return pl.pallas_call(
        _rwkv_kernel, grid=(B, S // ts), in_specs=[spec] * 6, out_specs=spec,
        out_shape=jax.ShapeDtypeStruct((B, S, W), F32),
        scratch_shapes=[pltpu.VMEM((RWKV_HEADS // 2, HEAD_DIM, LANES), F32)],
        compiler_params=pltpu.CompilerParams(dimension_semantics=("arbitrary", "arbitrary"),
                                             vmem_limit_bytes=VMEM_LIMIT),
        name="rwkv",
    )(r, ld, k, v, kk, b)


def _mlstm_kernel(q_ref, k_ref, v_ref, g_ref, expand_ref, h_ref, cn_scr, m_scr):
    c = pl.program_id(1)
    L = CHUNK
    n_chunks = q_ref.shape[0] // L
    H = MLSTM_HEADS
    n_pairs = H // 2

    @pl.when(c == 0)
    def _():
        cn_scr[...] = jnp.zeros(cn_scr.shape, F32)
        m_scr[...] = jnp.zeros(m_scr.shape, F32)

    row = lax.broadcasted_iota(I32, (L, L), 0)
    col = lax.broadcasted_iota(I32, (L, L), 1)
    tri = (col <= row).astype(BF16)
    row_l = lax.broadcasted_iota(I32, (L, LANES), 0)
    col_l = lax.broadcasted_iota(I32, (L, LANES), 1)
    incl2 = (col_l % HEAD_DIM) <= row_l
    r2 = lax.broadcasted_iota(I32, (LANES, LANES), 0)
    c2 = lax.broadcasted_iota(I32, (LANES, LANES), 1)
    same_head = (r2 // HEAD_DIM) == (c2 // HEAD_DIM)
    same_head2 = jnp.concatenate([same_head, same_head], axis=1)

    units = [(ci, p) for ci in range(n_chunks) for p in range(n_pairs)]
    ig_rep, g_rep, x_t = [], [], []
    for ci in range(n_chunks):
        gt = g_ref[pl.ds(ci * L, L), :]
        rep = _dot_lhs2(gt, expand_ref[...])
        ig_rep.append(rep[:, 0:MLSTM_WIDTH])
        g_rep.append(_dot_rhs2(tri, rep[:, MLSTM_WIDTH:]))
        x_t.append(gt.T[0:H, :] - _dot_rhs2(tri, gt).T[H:2 * H, :])

    def part(ref, u):
        ci, p = u
        return ref[pl.ds(ci * L, L), p * LANES:(p + 1) * LANES]

    def pair(x, p):
        return x[:, p * LANES:(p + 1) * LANES]

    def block_diag(x):
        return jnp.where(same_head, jnp.concatenate([x, x], axis=0), 0.0)

    ig_c = [pair(ig_rep[ci], p) for ci, p in units]
    g_c = [pair(g_rep[ci], p) for ci, p in units]
    g_last = [g[L - 1:L, :] for g in g_c]
    x_row = [jnp.concatenate([x_t[ci][2 * p:2 * p + 1, :], x_t[ci][2 * p + 1:2 * p + 2, :]], axis=1)
             for ci, p in units]
    d_log = [jnp.where(incl2, g + xr, -jnp.inf) for g, xr in zip(g_c, x_row)]
    x_run = [ig - g for ig, g in zip(ig_c, g_c)]
    shift = 1
    while shift < L:
        x_run = [jnp.maximum(x, jnp.where(row_l >= shift, pltpu.roll(x, shift, 0), -jnp.inf)) for x in x_run]
        shift *= 2
    d_max = [g + x for g, x in zip(g_c, x_run)]
    qk = [_dot_nt(part(q_ref, u), block_diag(part(k_ref, u))) for u in units]
    e = [gl - g + ig for gl, g, ig in zip(g_last, g_c, ig_c)]
    m_loc = [jnp.max(x, axis=0, keepdims=True) for x in e]
    wk = [part(k_ref, u) * jnp.exp(x - m) for u, x, m in zip(units, e, m_loc)]
    ones = jnp.ones((L, LANES), F32)
    kvn_loc = [jnp.where(same_head2, _dot_tn(w, jnp.concatenate([part(v_ref, u), ones], axis=1)), 0.0)
               for w, u in zip(wk, units)]

    cn_st = [cn_scr[p] for p in range(n_pairs)]
    m_st = [m_scr[p:p + 1, :] for p in range(n_pairs)]
    cn_prev, m_prev = [], []
    for i, (ci, p) in enumerate(units):
        cn_prev.append(cn_st[p])
        m_prev.append(m_st[p])
        m_new = jnp.maximum(g_last[i] + m_st[p], m_loc[i])
        a_old = jnp.exp(g_last[i] + m_st[p] - m_new)
        a_new = jnp.exp(m_loc[i] - m_new)
        cn_st[p] = (jnp.concatenate([a_old, a_old], axis=1) * cn_st[p]
                    + jnp.concatenate([a_new, a_new], axis=1) * kvn_loc[i])
        m_st[p] = m_new
    for p in range(n_pairs):
        cn_scr[p] = cn_st[p]
        m_scr[p:p + 1, :] = m_st[p]

    inter_log = [g + m for g, m in zip(g_c, m_prev)]
    m_j = [jnp.maximum(dm, il) for dm, il in zip(d_max, inter_log)]
    w_intra = [jnp.exp(d - m) * s for d, m, s in zip(d_log, m_j, qk)]
    w_inter = [jnp.exp(il - m) for il, m in zip(inter_log, m_j)]
    ones_bd = same_head.astype(F32)
    intra = [_dot(w, jnp.concatenate([block_diag(part(v_ref, u)), ones_bd], axis=1))
             for w, u in zip(w_intra, units)]
    inter = [_dot(part(q_ref, u), cn) for u, cn in zip(units, cn_prev)]
    outs = [(ia[:, 0:LANES] + wi * ie[:, 0:LANES])
            / jnp.maximum(jnp.abs(ia[:, LANES:] + wi * ie[:, LANES:]), jnp.exp(-m))
            for ia, wi, ie, m in zip(intra, w_inter, inter, m_j)]
    for ci in range(n_chunks):
        h_ref[pl.ds(ci * L, L), :] = jnp.concatenate(outs[ci * n_pairs:(ci + 1) * n_pairs], axis=1)


def _gate_expand_matrix():
    lane = jnp.arange(LANES)[:, None]
    col = jnp.arange(2 * MLSTM_WIDTH)[None, :]
    src = jnp.where(col < MLSTM_WIDTH, col // HEAD_DIM, MLSTM_HEADS + (col - MLSTM_WIDTH) // HEAD_DIM)
    return (lane == src).astype(BF16)


def _mlstm_call(q, k, v, g):
    B, S, W = q.shape
    ts = SEQ_ROWS
    spec = pl.BlockSpec((None, ts, W), lambda bi, c: (bi, c, 0))
    gspec = pl.BlockSpec((None, ts, LANES), lambda bi, c: (bi, c, 0))
    espec = pl.BlockSpec((LANES, 2 * W), lambda bi, c: (0, 0))
    return pl.pallas_call(
        _mlstm_kernel, grid=(B, S // ts), in_specs=[spec, spec, spec, gspec, espec], out_specs=spec,
        out_shape=jax.ShapeDtypeStruct((B, S, W), F32),
        scratch_shapes=[pltpu.VMEM((MLSTM_HEADS // 2, LANES, 2 * LANES), F32),
                        pltpu.VMEM((8, LANES), F32)],
        compiler_params=pltpu.CompilerParams(dimension_semantics=("arbitrary", "arbitrary"),
                                             vmem_limit_bytes=VMEM_LIMIT),
        name="mlstm",
    )(q, k, v, g, _gate_expand_matrix())


def _layer_norm(z, g, b):
    mu = jnp.mean(z, axis=-1, keepdims=True)
    d = z - mu
    var = jnp.mean(d * d, axis=-1, keepdims=True)
    return d * lax.rsqrt(var + LN_EPS) * g + b


def _post_kernel(h_ref, ypool_ref, yr_ref, bonus_ref, g_ref, hm_ref, mo_ref,
                 lnxg_ref, lnxb_ref, ng_ref, wout_ref, ln1g_ref, ln1b_ref, rwt_ref, rb_ref, ones_ref,
                 h1_ref, gate_ref, route_ref, cnt_ref):
    ts = h_ref.shape[0]
    ones_bd = ones_ref[...]
    y_rwkv = (_head_norm(yr_ref[...], ones_bd, RWKV_GN_EPS) * lnxg_ref[...] + lnxb_ref[...]
              + bonus_ref[...]) * g_ref[...]
    y_ml = mo_ref[...] * (_head_norm(hm_ref[...], ones_bd, LN_EPS) * ng_ref[...])
    mix = (_dot(ypool_ref[...], wout_ref[0:POOL_WIDTH, :])
           + _dot(y_rwkv, wout_ref[POOL_WIDTH:POOL_WIDTH + RWKV_WIDTH, :])
           + _dot(y_ml, wout_ref[POOL_WIDTH + RWKV_WIDTH:, :]))
    h1 = _layer_norm(DEEPNORM_ALPHA * h_ref[...] + mix, ln1g_ref[...], ln1b_ref[...])
    h1_ref[...] = h1

    hh, hl = _split(h1)
    wh, wl = _split(rwt_ref[...])
    nt = lambda a, b: lax.dot_general(a, b, (((1,), (1,)), ((), ())), preferred_element_type=F32)
    logits = nt(wh, hh) + nt(wh, hl) + nt(wl, hh) + rb_ref[...]
    eidx = lax.broadcasted_iota(I32, (N_EXPERTS, ts), 0)
    vals = logits
    tops, hots, idxs = [], [], []
    for _ in range(TOP_K):
        mx = jnp.max(vals, axis=0, keepdims=True)
        idx = jnp.min(jnp.where(vals == mx, eidx, N_EXPERTS), axis=0, keepdims=True)
        hot = eidx == idx
        vals = jnp.where(hot, -jnp.inf, vals)
        tops.append(mx)
        hots.append(hot)
        idxs.append(idx)
    exps = [jnp.exp(t - tops[0]) for t in tops]
    denom = exps[0] + exps[1] + exps[2] + exps[3]
    gate_rows = [e / denom for e in exps]
    gate_ref[...] = jnp.concatenate(gate_rows + [jnp.zeros((8 - TOP_K, ts), F32)], axis=0)

    any_hot = (hots[0] | hots[1] | hots[2] | hots[3])
    hot_f = any_hot.astype(F32)
    r_i = lax.broadcasted_iota(I32, (ts, ts), 0)
    c_i = lax.broadcasted_iota(I32, (ts, ts), 1)
    before = (r_i < c_i).astype(BF16)
    within = jnp.dot(hot_f.astype(BF16), before, preferred_element_type=F32)
    n_e = jnp.broadcast_to(jnp.sum(hot_f, axis=1, keepdims=True), (N_EXPERTS, LANES))
    run_len = jnp.floor((n_e + (RUN_PIECE - 1)) * (1.0 / RUN_PIECE)) * RUN_PIECE
    e_r = lax.broadcasted_iota(I32, (N_EXPERTS, N_EXPERTS), 0)
    e_c = lax.broadcasted_iota(I32, (N_EXPERTS, N_EXPERTS), 1)
    lower = (e_c < e_r).astype(BF16)
    run_start = _dot_rhs2(lower, run_len)[:, 0:1]
    slots = [jnp.sum(jnp.where(hot, run_start + within, 0.0), axis=0, keepdims=True) for hot in hots]
    route_ref[...] = jnp.concatenate([p.astype(I32) for p in slots] + [jnp.zeros((8 - TOP_K, ts), I32)], axis=0)
    cnt_ref[...] = n_e


def _post_call(h, ypool, yr, bonus, g, hm, mo, lw):
    T = h.shape[0]
    ts = PREP_ROWS
    row = lambda c: pl.BlockSpec((ts, c), lambda i: (i, 0))
    colb = lambda r: pl.BlockSpec((r, ts), lambda i: (0, i))
    full = lambda a: pl.BlockSpec(a.shape, lambda i: (0,) * a.ndim)
    params = [lw['lnx_g'], lw['lnx_b'], lw['norm_g'], lw['w_out'], lw['ln1_g'], lw['ln1_b'],
              lw['router_wt'], lw['router_b'], lw['ones_bd']]
    in_specs = ([row(D_MODEL), row(POOL_WIDTH)] + [row(RWKV_WIDTH)] * 5 + [full(a) for a in params])
    out_shape = [jax.ShapeDtypeStruct((T, D_MODEL), F32), jax.ShapeDtypeStruct((8, T), F32),
                 jax.ShapeDtypeStruct((8, T), I32), jax.ShapeDtypeStruct((T // ts * N_EXPERTS, LANES), F32)]
    out_specs = [row(D_MODEL), colb(8), colb(8), pl.BlockSpec((N_EXPERTS, LANES), lambda i: (i, 0))]
    return pl.pallas_call(
        _post_kernel, grid=(T // ts,), in_specs=in_specs, out_specs=out_specs, out_shape=out_shape,
        compiler_params=pltpu.CompilerParams(dimension_semantics=("arbitrary",),
                                             vmem_limit_bytes=VMEM_LIMIT),
        name="post",
    )(h, ypool, yr, bonus, g, hm, mo, *params)


def _rows_copy(src_ref, src_row, dst_ref, dst_row, n_rows, sem):
    src_row, dst_row = [r if isinstance(r, int) else pl.multiple_of(r, SUBLANES) for r in (src_row, dst_row)]
    return pltpu.make_async_copy(src_ref.at[pl.ds(src_row, n_rows), :], dst_ref.at[pl.ds(dst_row, n_rows), :], sem)


def _stage_rows(ts):
    n = TOP_K * ts + N_EXPERTS * (RUN_PIECE - 1)
    return -(-n // SUBLANES) * SUBLANES


def _for_each_piece(n_pieces, body):
    lax.fori_loop(0, n_pieces, lambda j, c: (body(j), c)[1], 0)


def _run_copies(tab_ref, src_of, dst_of, sem, wait):
    for e in range(N_EXPERTS):
        n_rows = pl.multiple_of(tab_ref[0, e], SUBLANES)

        @pl.when(n_rows > 0)
        def _(e=e, n_rows=n_rows):
            src_ref, src_row = src_of(e)
            dst_ref, dst_row = dst_of(e)
            cp = _rows_copy(src_ref, src_row, dst_ref, dst_row, n_rows, sem)
            cp.wait() if wait else cp.start()


def _dispatch_kernel(tab_ref, prev_tab_ref, fill_ref, route_ref, gate_ref, h_ref, buf_ref,
                     sorted_scr, zero_scr, sems):
    step = pl.program_id(0)
    last = pl.num_programs(0) - 1
    slot = step % 2
    ts = h_ref.shape[0]
    n_sorted = sorted_scr.shape[1]

    @pl.when(step == 0)
    def _():
        zero_scr[...] = jnp.zeros(zero_scr.shape, F32)

        def fill(wait):
            for e in range(N_EXPERTS):
                def piece(j, e=e):
                    cp = _rows_copy(zero_scr, 0, buf_ref, fill_ref[0, e] + j * FILL_PIECE, FILL_PIECE, sems.at[0])
                    cp.wait() if wait else cp.start()
                _for_each_piece(fill_ref[0, N_EXPERTS + e], piece)

            def tail(j):
                cp = _rows_copy(zero_scr, 0, buf_ref, fill_ref[0, 2 * N_EXPERTS] + j * EXPERT_ROWS,
                                EXPERT_ROWS, sems.at[0])
                cp.wait() if wait else cp.start()
            _for_each_piece(fill_ref[0, 2 * N_EXPERTS + 1], tail)

        fill(False)
        fill(True)

    pos = lax.broadcasted_iota(I32, (n_sorted, ts), 0)
    hot = None
    weight = None
    for kslot in range(TOP_K):
        eq = pos == route_ref[kslot:kslot + 1, :]
        w = jnp.where(eq, gate_ref[kslot:kslot + 1, :], 0.0)
        hot = eq if hot is None else (hot | eq)
        weight = w if weight is None else weight + w
    sorted_scr[slot, :, 0:D_MODEL] = jnp.dot(hot.astype(BF16), h_ref[...].astype(BF16),
                                             preferred_element_type=F32)
    w_hi, w_lo = _split(weight)
    w_lo2 = (weight - w_hi.astype(F32) - w_lo.astype(F32)).astype(BF16)
    ones = jnp.ones((ts, LANES), BF16)
    sorted_scr[slot, :, D_MODEL:BUF_WIDTH] = (jnp.dot(w_hi, ones, preferred_element_type=F32)
                                              + jnp.dot(w_lo, ones, preferred_element_type=F32)
                                              + jnp.dot(w_lo2, ones, preferred_element_type=F32))

    def copies(tab, which, wait):
        _run_copies(tab, lambda e: (sorted_scr.at[which], tab[0, N_EXPERTS + e]),
                    lambda e: (buf_ref, tab[0, 2 * N_EXPERTS + e]), sems.at[which], wait)

    @pl.when(step > 0)
    def _():
        copies(prev_tab_ref, 1 - slot, True)

    copies(tab_ref, slot, False)

    @pl.when(step == last)
    def _():
        copies(tab_ref, slot, True)


def _dispatch_call(tab_d, fill_tab, route8, gate8, h1, n_rows):
    T = h1.shape[0]
    ts = PREP_ROWS
    smem_tile = pl.BlockSpec((None, 1, LANES), lambda i: (i, 0, 0), memory_space=pltpu.SMEM)
    smem_prev = pl.BlockSpec((None, 1, LANES), lambda i: (jnp.maximum(i - 1, 0), 0, 0), memory_space=pltpu.SMEM)
    return pl.pallas_call(
        _dispatch_kernel, grid=(T // ts,),
        in_specs=[smem_tile, smem_prev,
                  pl.BlockSpec((1, LANES), lambda i: (0, 0), memory_space=pltpu.SMEM),
                  pl.BlockSpec((8, ts), lambda i: (0, i)),
                  pl.BlockSpec((8, ts), lambda i: (0, i)),
                  pl.BlockSpec((ts, D_MODEL), lambda i: (i, 0))],
        out_specs=pl.BlockSpec(memory_space=pl.ANY),
        out_shape=jax.ShapeDtypeStruct((n_rows, BUF_WIDTH), F32),
        scratch_shapes=[pltpu.VMEM((2, _stage_rows(ts), BUF_WIDTH), F32),
                        pltpu.VMEM((EXPERT_ROWS, BUF_WIDTH), F32),
                        pltpu.SemaphoreType.DMA((2,))],
        compiler_params=pltpu.CompilerParams(dimension_semantics=("arbitrary",),
                                             vmem_limit_bytes=VMEM_LIMIT),
        name="dispatch",
    )(tab_d, tab_d, fill_tab, route8, gate8, h1)


def _expert_kernel(be_ref, nb_ref, run_ref, next_ref, x_ref, wgu_hbm, bgu_ref, wdn_hbm, bdn_ref, o_ref,
                   wgu_f32, wdn_f32, wgu_scr, wdn_scr, sems, *, layer):
    j = pl.program_id(0)
    used = j < nb_ref[0]
    changed = jnp.logical_or(j == 0, be_ref[j] != be_ref[jnp.maximum(j - 1, 0)])

    def fetch(expert, slot):
        return (pltpu.make_async_copy(wgu_hbm.at[layer, expert], wgu_f32.at[slot], sems.at[0, slot]),
                pltpu.make_async_copy(wdn_hbm.at[layer, expert], wdn_f32.at[slot], sems.at[1, slot]))

    @pl.when(j == 0)
    def _():
        for cp in fetch(be_ref[0], 0):
            cp.start()

    @pl.when(jnp.logical_and(used, changed))
    def _():
        slot = run_ref[j] % 2
        for cp in fetch(be_ref[j], slot):
            cp.wait()
        wgu_scr[...] = wgu_f32[slot].astype(BF16)
        wdn_scr[...] = wdn_f32[slot].astype(BF16)

        @pl.when(next_ref[j] >= 0)
        def _():
            for cp in fetch(next_ref[j], 1 - slot):
                cp.start()

    @pl.when(used)
    def _():
        gu = jnp.dot(x_ref[:, 0:D_MODEL].astype(BF16), wgu_scr[...], preferred_element_type=F32) + bgu_ref[...]
        glu = jnp.minimum(gu[:, 0:D_FF], SWIGLU_LIMIT)
        lin = jnp.clip(gu[:, D_FF:], -SWIGLU_LIMIT, SWIGLU_LIMIT)
        act = glu * _sigmoid(SWIGLU_ALPHA * glu) * (lin + 1.0)
        y = jnp.dot(act.astype(BF16), wdn_scr[...], preferred_element_type=F32) + bdn_ref[...]
        gate = x_ref[:, D_MODEL:BUF_WIDTH]
        o_ref[...] = y * jnp.concatenate([gate] * (D_MODEL // LANES), axis=1)

    @pl.when(jnp.logical_not(used))
    def _():
        o_ref[...] = jnp.zeros(o_ref.shape, F32)


def _expert_call(block_e, n_used, buf, lw):
    n_rows = buf.shape[0]
    rows = EXPERT_ROWS
    n_blocks = n_rows // rows
    j = jnp.arange(n_blocks, dtype=I32)
    first = (j < n_used[0]) & ((j == 0) | (block_e != jnp.roll(block_e, 1)))
    run_idx = jnp.cumsum(first.astype(I32)) - 1
    later_first = lax.cummin(jnp.where(first, j, n_blocks)[::-1])[::-1]
    next_start = jnp.concatenate([later_first[1:], jnp.full((1,), n_blocks, I32)])
    next_e = jnp.where(next_start < n_blocks, block_e[jnp.minimum(next_start, n_blocks - 1)], -1).astype(I32)

    def blk(j, be, nb, run, nxt):
        return jnp.minimum(j, nb[0] - 1)

    layer = lw['layer']
    grid_spec = pltpu.PrefetchScalarGridSpec(
        num_scalar_prefetch=4, grid=(n_blocks,),
        in_specs=[pl.BlockSpec((rows, BUF_WIDTH), lambda j, *s: (blk(j, *s), 0)),
                  pl.BlockSpec(memory_space=pl.ANY),
                  pl.BlockSpec((None, None, 1, 2 * D_FF), lambda j, be, *s: (layer, be[j], 0, 0)),
                  pl.BlockSpec(memory_space=pl.ANY),
                  pl.BlockSpec((None, None, 1, D_MODEL), lambda j, be, *s: (layer, be[j], 0, 0))],
        out_specs=pl.BlockSpec((rows, D_MODEL), lambda j, *s: (j, 0)),
        scratch_shapes=[pltpu.VMEM((2, D_MODEL, 2 * D_FF), F32), pltpu.VMEM((2, D_FF, D_MODEL), F32),
                        pltpu.VMEM((D_MODEL, 2 * D_FF), BF16), pltpu.VMEM((D_FF, D_MODEL), BF16),
                        pltpu.SemaphoreType.DMA((2, 2))])
    return pl.pallas_call(
        functools.partial(_expert_kernel, layer=layer), grid_spec=grid_spec,
        out_shape=jax.ShapeDtypeStruct((n_rows, D_MODEL), F32),
        compiler_params=pltpu.CompilerParams(dimension_semantics=("arbitrary",),
                                             vmem_limit_bytes=56 * 1024 * 1024),
        name="expert",
    )(block_e, n_used, run_idx, next_e, buf, lw['w_gate_up'], lw['b_gate_up'], lw['w_down'], lw['b_down'])


def _combine_kernel(tab_ref, next_tab_ref, route_ref, h_ref, out_hbm_ref, ln2g_ref, ln2b_ref, o_ref,
                    stage_scr, sems):
    step = pl.program_id(0)
    last = pl.num_programs(0) - 1
    slot = step % 2
    ts = h_ref.shape[0]
    n_stage = stage_scr.shape[1]

    def copies(tab, which, wait):
        _run_copies(tab, lambda e: (out_hbm_ref, tab[0, 2 * N_EXPERTS + e]),
                    lambda e: (stage_scr.at[which], tab[0, N_EXPERTS + e]), sems.at[which], wait)

    @pl.when(step == 0)
    def _():
        stage_scr[...] = jnp.zeros(stage_scr.shape, F32)
        copies(tab_ref, 0, False)

    @pl.when(step < last)
    def _():
        copies(next_tab_ref, 1 - slot, False)

    slot_f = route_ref[...].astype(F32)
    slot_cols = jnp.concatenate([slot_f, jnp.zeros((LANES - 8, ts), F32)], axis=0).T
    lane = lax.broadcasted_iota(I32, (ts, n_stage), 1).astype(F32)
    sel = None
    for kslot in range(TOP_K):
        eq = lane == slot_cols[:, kslot:kslot + 1]
        sel = eq if sel is None else (sel | eq)
    copies(tab_ref, slot, True)
    y = jnp.dot(sel.astype(BF16), stage_scr[slot].astype(BF16), preferred_element_type=F32)
    o_ref[...] = _layer_norm(DEEPNORM_ALPHA * h_ref[...] + y, ln2g_ref[...], ln2b_ref[...])


def _combine_call(tab_c, route8, h1, out_rows, lw):
    T = h1.shape[0]
    ts = PREP_ROWS
    n_tiles = T // ts
    full = lambda a: pl.BlockSpec(a.shape, lambda i: (0,) * a.ndim)
    n_stage = _stage_rows(ts)
    return pl.pallas_call(
        _combine_kernel, grid=(n_tiles,),
        in_specs=[pl.BlockSpec((None, 1, LANES), lambda i: (i, 0, 0), memory_space=pltpu.SMEM),
                  pl.BlockSpec((None, 1, LANES), lambda i: (jnp.minimum(i + 1, n_tiles - 1), 0, 0),
                               memory_space=pltpu.SMEM),
                  pl.BlockSpec((8, ts), lambda i: (0, i)),
                  pl.BlockSpec((ts, D_MODEL), lambda i: (i, 0)),
                  pl.BlockSpec(memory_space=pl.ANY),
                  full(lw['ln2_g']), full(lw['ln2_b'])],
        out_specs=pl.BlockSpec((ts, D_MODEL), lambda i: (i, 0)),
        out_shape=jax.ShapeDtypeStruct((T, D_MODEL), F32),
        scratch_shapes=[pltpu.VMEM((2, n_stage, D_MODEL), F32), pltpu.SemaphoreType.DMA((2,))],
        compiler_params=pltpu.CompilerParams(dimension_semantics=("arbitrary",),
                                             vmem_limit_bytes=VMEM_LIMIT),
        name="combine",
    )(tab_c, tab_c, route8, h1, out_rows, lw['ln2_g'], lw['ln2_b'])


def _block_diag_ones(width):
    hid = jnp.arange(width) // HEAD_DIM
    return (hid[:, None] == hid[None, :]).astype(BF16)


def _layer_params(l, w_in, pool_w, pool_scale, rwkv_mu, rwkv_w0, rwkv_w2, rwkv_a0, rwkv_a2, rwkv_g2,
                  rwkv_kk_scale, rwkv_ka, rwkv_rk, rwkv_lnx_g, rwkv_lnx_b, rwkv_v0, rwkv_v1, rwkv_v2,
                  mlstm_conv_w, mlstm_conv_b, mlstm_b_i, mlstm_b_f, mlstm_norm_g, w_out, ln1_g, ln1_b,
                  router_w, router_b, w_gate_up, b_gate_up, w_down, b_down, ln2_g, ln2_b):
    row = lambda a: a.reshape(1, -1).astype(F32)
    pad_cols = D_IN_PAD - D_IN
    if l > 0:
        extra = jnp.concatenate([rwkv_v1[l - 1], jnp.zeros((D_MODEL, pad_cols - VRES_LORA), F32)], axis=1)
        v0 = row(rwkv_v0[l - 1])
        v2 = jnp.zeros((LANES, RWKV_WIDTH), F32).at[VRES_OFF:VRES_OFF + VRES_LORA].set(rwkv_v2[l - 1])
    else:
        extra = jnp.zeros((D_MODEL, pad_cols), F32)
        v0 = jnp.zeros((1, RWKV_WIDTH), F32)
        v2 = jnp.zeros((LANES, RWKV_WIDTH), F32)
    pw = jnp.zeros((POOL_WIDTH, POOL_WIDTH), F32)
    for gi in range(len(POOL_WINDOWS)):
        sl = slice(gi * POOL_GROUP, (gi + 1) * POOL_GROUP)
        pw = pw.at[sl, sl].set(pool_w[l, gi])
    zero_lora = jnp.zeros((DECAY_LORA, RWKV_WIDTH), F32)
    gate_bias = jnp.zeros((1, LANES), F32).at[0, 0:MLSTM_HEADS].set(mlstm_b_i[l])
    gate_bias = gate_bias.at[0, MLSTM_HEADS:2 * MLSTM_HEADS].set(mlstm_b_f[l])
    return {
        'w_in': jnp.concatenate([w_in[l], extra], axis=1).astype(BF16),
        'pool_w': pw.astype(BF16), 'pool_scale': row(pool_scale[l]), 'mu': row(rwkv_mu[l]),
        'w0': row(rwkv_w0[l]), 'w2': jnp.concatenate([rwkv_w2[l], zero_lora], axis=0).astype(BF16),
        'a0': row(rwkv_a0[l]), 'a2': jnp.concatenate([zero_lora, rwkv_a2[l]], axis=0).astype(BF16),
        'g2': rwkv_g2[l].astype(BF16), 'kk_scale': row(rwkv_kk_scale[l]), 'ka': row(rwkv_ka[l]),
        'rk': row(rwkv_rk[l]), 'v0': v0, 'v2': v2.astype(BF16),
        'conv_w': mlstm_conv_w[l], 'conv_b': row(mlstm_conv_b[l]), 'gate_bias': gate_bias,
        'ones_bd': _block_diag_ones(RWKV_WIDTH),
        'lnx_g': row(rwkv_lnx_g[l]), 'lnx_b': row(rwkv_lnx_b[l]), 'norm_g': row(mlstm_norm_g[l]),
        'w_out': w_out[l].astype(BF16), 'ln1_g': row(ln1_g[l]), 'ln1_b': row(ln1_b[l]),
        'router_wt': router_w[l].T, 'router_b': router_b[l].reshape(N_EXPERTS, 1),
        'layer': l, 'w_gate_up': w_gate_up, 'b_gate_up': b_gate_up.reshape(-1, N_EXPERTS, 1, 2 * D_FF),
        'w_down': w_down, 'b_down': b_down.reshape(-1, N_EXPERTS, 1, D_MODEL),
        'ln2_g': row(ln2_g[l]), 'ln2_b': row(ln2_b[l]),
    }


def _layer(h, v_first, lw, *, has_vres):
    B, S, _ = h.shape
    T = B * S
    outs = _prep_call(h, lw, v_first, has_vres=has_vres)
    ypool, r, ld, k, v, kk, b, g, bonus, mq, mk, mv, mo, mg = outs
    yr = _rwkv_call(r, ld, k, v, kk, b)
    hm = _mlstm_call(mq, mk, mv, mg)
    flat = lambda a: a.reshape(T, a.shape[-1])
    h_flat = flat(h)
    h1, gate8, route8, cnt = _post_call(h_flat, flat(ypool), flat(yr), flat(bonus), flat(g), flat(hm),
                                        flat(mo), lw)
    n_tiles = T // PREP_ROWS
    cnt = cnt.reshape(n_tiles, N_EXPERTS, LANES)[:, :, 0].astype(I32)
    rows = EXPERT_ROWS
    run_len = (cnt + RUN_PIECE - 1) // RUN_PIECE * RUN_PIECE
    region = jnp.sum(run_len, axis=0)
    padded = (region + rows - 1) // rows * rows
    pad_end = jnp.cumsum(padded)
    pad_start = pad_end - padded
    n_blocks = -(-(T * TOP_K + n_tiles * N_EXPERTS * (RUN_PIECE - 1)) // rows) + N_EXPERTS
    n_used = (pad_end[-1] // rows).astype(I32).reshape(1)
    starts = jnp.minimum(jnp.arange(n_blocks, dtype=I32), n_used[0] - 1) * rows
    block_e = jnp.sum((pad_end[None, :] <= starts[:, None]).astype(I32), axis=1)
    run_src = jnp.cumsum(run_len, axis=1) - run_len
    run_dst = pad_start[None, :] + jnp.cumsum(run_len, axis=0) - run_len
    run_tab = jnp.concatenate([run_len, run_src, run_dst,
                               jnp.zeros((n_tiles, LANES - 3 * N_EXPERTS), I32)], axis=1).reshape(n_tiles, 1, LANES)
    fill_start = (pad_start + region) // FILL_PIECE * FILL_PIECE
    fill_tab = jnp.concatenate([fill_start, (pad_end - fill_start) // FILL_PIECE,
                                pad_end[-1:], n_blocks - n_used,
                                jnp.zeros((LANES - 2 * N_EXPERTS - 2,), I32)]).reshape(1, LANES)
    buf = _dispatch_call(run_tab, fill_tab, route8, gate8, h1, n_blocks * rows)
    out_rows = _expert_call(block_e, n_used, buf, lw)
    h2 = _combine_call(run_tab, route8, h1, out_rows, lw)
    return h2.reshape(B, S, D_MODEL), v


def kernel(x, w_in, pool_w, pool_scale, rwkv_mu, rwkv_w0, rwkv_w2, rwkv_a0, rwkv_a2, rwkv_g2, rwkv_kk_scale, rwkv_ka, rwkv_rk, rwkv_lnx_g, rwkv_lnx_b, rwkv_v0, rwkv_v1, rwkv_v2, mlstm_conv_w, mlstm_conv_b, mlstm_b_i, mlstm_b_f, mlstm_norm_g, w_out, ln1_g, ln1_b, router_w, router_b, w_gate_up, b_gate_up, w_down, b_down, ln2_g, ln2_b):
    weights = (w_in, pool_w, pool_scale, rwkv_mu, rwkv_w0, rwkv_w2, rwkv_a0, rwkv_a2, rwkv_g2, rwkv_kk_scale,
               rwkv_ka, rwkv_rk, rwkv_lnx_g, rwkv_lnx_b, rwkv_v0, rwkv_v1, rwkv_v2, mlstm_conv_w, mlstm_conv_b,
               mlstm_b_i, mlstm_b_f, mlstm_norm_g, w_out, ln1_g, ln1_b, router_w, router_b, w_gate_up,
               b_gate_up, w_down, b_down, ln2_g, ln2_b)
    h = x
    v_first = jnp.zeros(x.shape[:2] + (RWKV_WIDTH,), F32)
    for l in range(w_in.shape[0]):
        lw = _layer_params(l, *weights)
        h, v_l = _layer(h, v_first, lw, has_vres=l > 0)
        if l == 0:
            v_first = v_l
    return h
```

```python
import functools

import jax
import jax.numpy as jnp
from jax import lax
from jax.experimental import pallas as pl
from jax.experimental.pallas import tpu as pltpu

F32 = jnp.float32
BF16 = jnp.bfloat16
I32 = jnp.int32

D_MODEL = 1024
HEAD_DIM = 64
POOL_WINDOWS = (2, 4, 8, 16)
POOL_WIDTH = 256
POOL_GROUP = 64
RWKV_WIDTH = 384
RWKV_HEADS = 6
DECAY_LORA = 64
ICLR_LORA = 64
GATE_LORA = 128
VRES_LORA = 32
RWKV_GN_EPS = 64e-5
RWKV_COLS = 3 * RWKV_WIDTH + DECAY_LORA + ICLR_LORA + GATE_LORA
MLSTM_WIDTH = 384
MLSTM_HEADS = 6
MLSTM_CONV = 4
MLSTM_COLS = 4 * MLSTM_WIDTH + 2 * MLSTM_HEADS
D_IN = POOL_WIDTH + RWKV_COLS + MLSTM_COLS
N_EXPERTS = 32
TOP_K = 4
D_FF = D_MODEL
SWIGLU_LIMIT = 7.0
SWIGLU_ALPHA = 1.702
LN_EPS = 1e-5
DEPTH = 2
DEEPNORM_ALPHA = (2 * DEPTH) ** 0.25

LANES = 128
D_IN_PAD = 3328
RWKV_OFF = POOL_WIDTH
MLSTM_OFF = POOL_WIDTH + RWKV_COLS
GATE_OFF = MLSTM_OFF + 4 * MLSTM_WIDTH
VRES_OFF = 2 * MLSTM_HEADS
HALO = 16
CHUNK = 64

PROJ_ROWS = 512
PREP_ROWS = 256
SEQ_ROWS = 512
EXPERT_ROWS = 512
BUF_WIDTH = D_MODEL + LANES
SUBLANES = 8
RUN_PIECE = SUBLANES
FILL_PIECE = 64
VMEM_LIMIT = 48 * 1024 * 1024


def _dot(a, b):
    return jnp.dot(a.astype(BF16), b.astype(BF16), preferred_element_type=F32)


def _dot_nt(a, b):
    return lax.dot_general(a.astype(BF16), b.astype(BF16), (((1,), (1,)), ((), ())),
                           preferred_element_type=F32)


def _dot_tn(a, b):
    return lax.dot_general(a.astype(BF16), b.astype(BF16), (((0,), (0,)), ((), ())),
                           preferred_element_type=F32)


def _split(x):
    hi = x.astype(BF16)
    lo = (x - hi.astype(F32)).astype(BF16)
    return hi, lo


def _dot_lhs2(a, b_bf16):
    hi, lo = _split(a)
    return (jnp.dot(hi, b_bf16, preferred_element_type=F32)
            + jnp.dot(lo, b_bf16, preferred_element_type=F32))


def _dot_rhs2(a_bf16, b):
    hi, lo = _split(b)
    return (jnp.dot(a_bf16, hi, preferred_element_type=F32)
            + jnp.dot(a_bf16, lo, preferred_element_type=F32))


def _sigmoid(x):
    return 1.0 / (1.0 + jnp.exp(-x))


def _softplus(x):
    return jnp.maximum(x, 0.0) + jnp.log(1.0 + jnp.exp(-jnp.abs(x)))


def _head_norm(y, ones_bd, eps):
    inv = 1.0 / HEAD_DIM
    mean = _dot(y, ones_bd) * inv
    d = y - mean
    var = _dot(d * d, ones_bd) * inv
    return d * lax.rsqrt(var + eps)


def _prep_kernel(x_ref, w_ref, poolw_ref, pscale_ref, mu_ref, w0_ref, w2_ref, a0_ref, a2_ref, g2_ref,
                 kks_ref, ka_ref, rk_ref, v0_ref, v2_ref, vfirst_ref, cw_ref, cb_ref, gbias_ref, ones_ref,
                 ypool_ref, r_ref, ld_ref, k_ref, v_ref, kk_ref, b_ref, g_ref, bonus_ref,
                 mq_ref, mk_ref, mv_ref, mo_ref, mg_ref,
                 p_scr, *, has_vres):
    i = pl.program_id(1)
    ts = x_ref.shape[0]

    @pl.when(i == 0)
    def _():
        p_scr[0:HALO, :] = jnp.zeros((HALO, D_IN_PAD), F32)

    @pl.when(i > 0)
    def _():
        p_scr[0:HALO, :] = p_scr[ts:ts + HALO, :]

    p_scr[HALO:HALO + ts, :] = jnp.dot(x_ref[...].astype(BF16), w_ref[...], preferred_element_type=F32)

    def rows(shift, c0, c1):
        return p_scr[HALO - shift:HALO - shift + ts, c0:c1]

    u = rows(0, 0, POOL_WIDTH)
    acc = u
    sums = {}
    for s in range(1, POOL_WINDOWS[-1]):
        acc = acc + rows(s, 0, POOL_WIDTH)
        if s + 1 in POOL_WINDOWS:
            sums[s + 1] = acc
    pos = (i * ts + lax.broadcasted_iota(I32, (ts, 1), 0) + 1).astype(F32)
    lane = lax.broadcasted_iota(I32, (ts, POOL_WIDTH), 1)
    d = None
    for gi, win in reversed(list(enumerate(POOL_WINDOWS))):
        dg = sums[win] / jnp.minimum(pos, float(win))
        d = dg if d is None else jnp.where(lane < (gi + 1) * POOL_GROUP, dg, d)
    d = d - u
    ypool_ref[...] = _dot(d, poolw_ref[...]) * pscale_ref[...]

    cur = rows(0, RWKV_OFF, RWKV_OFF + RWKV_COLS)
    prev = rows(1, RWKV_OFF, RWKV_OFF + RWKV_COLS)
    pf = cur + mu_ref[...] * (prev - cur)
    W = RWKV_WIDTH
    r = pf[:, 0:W]
    k = pf[:, W:2 * W]
    v = pf[:, 2 * W:3 * W]
    z = pf[:, 3 * W:3 * W + LANES]
    gd = pf[:, 3 * W + LANES:3 * W + 2 * LANES]
    w_log = -_softplus(-(w0_ref[...] + _dot(jnp.tanh(z), w2_ref[...]))) - 0.5
    ld_ref[...] = -jnp.exp(w_log)
    a = _sigmoid(a0_ref[...] + _dot(z, a2_ref[...]))
    g_ref[...] = _dot(_sigmoid(gd), g2_ref[...])
    gates = rows(0, GATE_OFF, GATE_OFF + LANES)
    if has_vres:
        v_gate = _sigmoid(v0_ref[...] + _dot(gates, v2_ref[...]))
        v = v + (vfirst_ref[...] - v) * v_gate
    ones_bd = ones_ref[...]
    kk = k * kks_ref[...]
    kk = kk / jnp.maximum(jnp.sqrt(_dot(kk * kk, ones_bd)), 1e-12)
    k = k * (1.0 + (a - 1.0) * ka_ref[...])
    r_ref[...] = r
    k_ref[...] = k
    v_ref[...] = v
    kk_ref[...] = kk
    b_ref[...] = kk * a
    bonus_ref[...] = _dot(r * k * rk_ref[...], ones_bd) * v

    qk = cb_ref[...] + rows(0, MLSTM_OFF, MLSTM_OFF + 2 * MLSTM_WIDTH) * cw_ref[MLSTM_CONV - 1:MLSTM_CONV, :]
    for tap in range(MLSTM_CONV - 1):
        shift = MLSTM_CONV - 1 - tap
        qk = qk + rows(shift, MLSTM_OFF, MLSTM_OFF + 2 * MLSTM_WIDTH) * cw_ref[tap:tap + 1, :]
    qk = qk * _sigmoid(qk)
    mq_ref[...] = qk[:, 0:MLSTM_WIDTH] * (HEAD_DIM ** -0.5)
    mk_ref[...] = qk[:, MLSTM_WIDTH:]
    mv_ref[...] = rows(0, MLSTM_OFF + 2 * MLSTM_WIDTH, MLSTM_OFF + 3 * MLSTM_WIDTH)
    mo_ref[...] = _sigmoid(rows(0, MLSTM_OFF + 3 * MLSTM_WIDTH, MLSTM_OFF + 4 * MLSTM_WIDTH))
    gb = gates + gbias_ref[...]
    glane = lax.broadcasted_iota(I32, (ts, LANES), 1)
    mg_ref[...] = jnp.where(glane < MLSTM_HEADS, gb, -_softplus(-gb))


def _prep_call(x, lw, vfirst, *, has_vres):
    B, S, _ = x.shape
    ts = PROJ_ROWS
    grid = (B, S // ts)
    row3 = lambda c: pl.BlockSpec((None, ts, c), lambda b, i: (b, i, 0))
    full = lambda a: pl.BlockSpec(a.shape, lambda b, i: (0,) * a.ndim)
    params = [lw['w_in'], lw['pool_w'], lw['pool_scale'], lw['mu'], lw['w0'], lw['w2'], lw['a0'], lw['a2'],
              lw['g2'], lw['kk_scale'], lw['ka'], lw['rk'], lw['v0'], lw['v2']]
    tail = [lw['conv_w'], lw['conv_b'], lw['gate_bias'], lw['ones_bd']]
    in_specs = ([row3(D_MODEL)] + [full(a) for a in params] + [row3(RWKV_WIDTH)] + [full(a) for a in tail])
    widths = [POOL_WIDTH] + [RWKV_WIDTH] * 8 + [MLSTM_WIDTH] * 4 + [LANES]
    out_shape = [jax.ShapeDtypeStruct((B, S, c), F32) for c in widths]
    out_specs = [row3(c) for c in widths]
    return pl.pallas_call(
        functools.partial(_prep_kernel, has_vres=has_vres),
        grid=grid, in_specs=in_specs, out_specs=out_specs, out_shape=out_shape,
        scratch_shapes=[pltpu.VMEM((HALO + ts, D_IN_PAD), F32)],
        compiler_params=pltpu.CompilerParams(dimension_semantics=("arbitrary", "arbitrary"),
                                             vmem_limit_bytes=VMEM_LIMIT),
        name="prep",
    )(x, *params, vfirst, *tail)


def _rwkv_kernel(r_ref, ld_ref, k_ref, v_ref, kk_ref, b_ref, y_ref, s_scr):
    c = pl.program_id(1)
    L = CHUNK
    n_chunks = r_ref.shape[0] // L

    @pl.when(c == 0)
    def _():
        s_scr[...] = jnp.zeros(s_scr.shape, F32)

    row = lax.broadcasted_iota(I32, (L, L), 0)
    col = lax.broadcasted_iota(I32, (L, L), 1)
    tri = (col <= row).astype(BF16)
    row_l = lax.broadcasted_iota(I32, (L, LANES), 0)
    col_l = lax.broadcasted_iota(I32, (L, LANES), 1) % HEAD_DIM
    strict = col_l < row_l
    incl = col_l <= row_l
    eye = (col_l == row_l).astype(F32)
    r2 = lax.broadcasted_iota(I32, (LANES, LANES), 0)
    c2 = lax.broadcasted_iota(I32, (LANES, LANES), 1)
    same_head = (r2 // HEAD_DIM) == (c2 // HEAD_DIM)
    eye2 = (r2 == c2).astype(F32)
    low_lanes = lax.broadcasted_iota(I32, (L, LANES), 1) < HEAD_DIM
    n_pairs = RWKV_HEADS // 2

    def block_diag(x):
        xb = x.astype(BF16)
        return jnp.where(same_head, jnp.concatenate([xb, xb], axis=0), jnp.zeros((), BF16))

    def diag_blocks(z):
        return jnp.where(low_lanes, z[0:L, :], z[L:2 * L, :])

    units = [(ci, p) for ci in range(n_chunks) for p in range(n_pairs)]
    per_chunk = []
    for ci in range(n_chunks):
        sl = pl.ds(ci * L, L)
        ld = ld_ref[sl, :]
        cin = _dot_rhs2(tri, ld)
        c_last = cin[L - 1:L, :]
        e_neg = jnp.exp(-cin)
        e_tail = jnp.exp(c_last - cin)
        kk = kk_ref[sl, :]
        bb = b_ref[sl, :]
        kx = k_ref[sl, :]
        per_chunk.append(dict(
            A=-kk * jnp.exp(cin - ld), R=r_ref[sl, :] * jnp.exp(cin), B=bb * e_neg, K=kx * e_neg,
            Bh=bb * e_tail, Kh=kx * e_tail, V=v_ref[sl, :], g_last=jnp.exp(c_last)))

    def part(name, u):
        ci, p = u
        return per_chunk[ci][name][:, p * LANES:(p + 1) * LANES]

    M = [_dot_nt(jnp.concatenate([part('A', u), part('R', u)], axis=0),
                 jnp.concatenate([block_diag(part('B', u)), block_diag(part('K', u))], axis=0)) for u in units]
    m_ab = [jnp.where(strict, m[0:L, 0:LANES], 0.0) for m in M]
    m_ak = [jnp.where(strict, m[0:L, LANES:], 0.0) for m in M]
    m_rb = [jnp.where(incl, m[L:, 0:LANES], 0.0) for m in M]
    m_rk = [jnp.where(incl, m[L:, LANES:], 0.0) for m in M]
    MVYK = [_dot(jnp.concatenate([ak, rk], axis=0), block_diag(part('V', u)))
            for ak, rk, u in zip(m_ak, m_rk, units)]
    MV = [x[0:L, :] for x in MVYK]
    YK = [x[L:, :] for x in MVYK]
    T = [eye + m for m in m_ab]
    pw = [_dot(m, block_diag(m)) for m in m_ab]
    for _ in range(4):
        both = [_dot(jnp.concatenate([t, p], axis=0), block_diag(p)) for t, p in zip(T, pw)]
        T = [t + x[0:L, :] for t, x in zip(T, both)]
        pw = [x[L:, :] for x in both]
    T = [t + _dot(t, block_diag(p)) for t, p in zip(T, pw)]
    WU = [_dot(t, jnp.concatenate([block_diag(part('A', u)), block_diag(mv)], axis=1))
          for t, u, mv in zip(T, units, MV)]
    GY = [_dot(m, jnp.concatenate([block_diag(wu[:, 0:LANES]), block_diag(wu[:, LANES:])], axis=1))
          for m, wu in zip(m_rb, WU)]
    G = [part('R', u) + gy[:, 0:LANES] for u, gy in zip(units, GY)]
    Y0 = [gy[:, LANES:] + yk for gy, yk in zip(GY, YK)]
    P = [jnp.where(same_head, _dot_tn(wu[:, 0:LANES], part('Bh', u)), 0.0) + eye2 * part('g_last', u)
         for wu, u in zip(WU, units)]
    Q = [diag_blocks(_dot_tn(jnp.concatenate([wu[:, LANES:], part('V', u)], axis=0),
                             jnp.concatenate([part('Bh', u), part('Kh', u)], axis=0)))
         for wu, u in zip(WU, units)]

    state = [s_scr[p] for p in range(n_pairs)]
    for ci in range(n_chunks):
        base = ci * n_pairs
        ys = [_dot_nt(G[base + p], block_diag(state[p])) + Y0[base + p] for p in range(n_pairs)]
        state = [_dot(state[p], P[base + p]) + Q[base + p] for p in range(n_pairs)]
        y_ref[pl.ds(ci * L, L), :] = jnp.concatenate(ys, axis=1)
    for p in range(n_pairs):
        s_scr[p] = state[p]


def _rwkv_call(r, ld, k, v, kk, b):
    B, S, W = r.shape
    ts = SEQ_ROWS
    spec = pl.BlockSpec((None, ts, W), lambda bi, c: (bi, c, 0))
    return pl.pallas_call(
        _rwkv_kernel, grid=(B, S // ts), in_specs=[spec] * 6, out_specs=spec,
        out_shape=jax.ShapeDtypeStruct((B, S, W), F32),
        scratch_shapes=[pltpu.VMEM((RWKV_HEADS // 2, HEAD_DIM, LANES), F32)],
        compiler_params=pltpu.CompilerParams(dimension_semantics=("arbitrary", "arbitrary"),
                                             vmem_limit_bytes=VMEM_LIMIT),
        name="rwkv",
    )(r, ld, k, v, kk, b)


def _mlstm_kernel(q_ref, k_ref, v_ref, g_ref, expand_ref, h_ref, cn_scr, m_scr):
    c = pl.program_id(1)
    L = CHUNK
    n_chunks = q_ref.shape[0] // L
    H = MLSTM_HEADS
    n_pairs = H // 2

    @pl.when(c == 0)
    def _():
        cn_scr[...] = jnp.zeros(cn_scr.shape, F32)
        m_scr[...] = jnp.zeros(m_scr.shape, F32)

    row = lax.broadcasted_iota(I32, (L, L), 0)
    col = lax.broadcasted_iota(I32, (L, L), 1)
    tri = (col <= row).astype(BF16)
    row_l = lax.broadcasted_iota(I32, (L, LANES), 0)
    col_l = lax.broadcasted_iota(I32, (L, LANES), 1)
    incl2 = (col_l % HEAD_DIM) <= row_l
    r2 = lax.broadcasted_iota(I32, (LANES, LANES), 0)
    c2 = lax.broadcasted_iota(I32, (LANES, LANES), 1)
    same_head = (r2 // HEAD_DIM) == (c2 // HEAD_DIM)
    same_head2 = jnp.concatenate([same_head, same_head], axis=1)

    units = [(ci, p) for ci in range(n_chunks) for p in range(n_pairs)]
    ig_rep, g_rep, x_t = [], [], []
    for ci in range(n_chunks):
        gt = g_ref[pl.ds(ci * L, L), :]
        rep = _dot_lhs2(gt, expand_ref[...])
        ig_rep.append(rep[:, 0:MLSTM_WIDTH])
        g_rep.append(_dot_rhs2(tri, rep[:, MLSTM_WIDTH:]))
        x_t.append(gt.T[0:H, :] - _dot_rhs2(tri, gt).T[H:2 * H, :])

    def part(ref, u):
        ci, p = u
        return ref[pl.ds(ci * L, L), p * LANES:(p + 1) * LANES]

    def pair(x, p):
        return x[:, p * LANES:(p + 1) * LANES]

    def block_diag(x):
        return jnp.where(same_head, jnp.concatenate([x, x], axis=0), 0.0)

    ig_c = [pair(ig_rep[ci], p) for ci, p in units]
    g_c = [pair(g_rep[ci], p) for ci, p in units]
    g_last = [g[L - 1:L, :] for g in g_c]
    x_row = [jnp.concatenate([x_t[ci][2 * p:2 * p + 1, :], x_t[ci][2 * p + 1:2 * p + 2, :]], axis=1)
             for ci, p in units]
    d_log = [jnp.where(incl2, g + xr, -jnp.inf) for g, xr in zip(g_c, x_row)]
    x_run = [ig - g for ig, g in zip(ig_c, g_c)]
    shift = 1
    while shift < L:
        x_run = [jnp.maximum(x, jnp.where(row_l >= shift, pltpu.roll(x, shift, 0), -jnp.inf)) for x in x_run]
        shift *= 2
    d_max = [g + x for g, x in zip(g_c, x_run)]
    qk = [_dot_nt(part(q_ref, u), block_diag(part(k_ref, u))) for u in units]
    e = [gl - g + ig for gl, g, ig in zip(g_last, g_c, ig_c)]
    m_loc = [jnp.max(x, axis=0, keepdims=True) for x in e]
    wk = [part(k_ref, u) * jnp.exp(x - m) for u, x, m in zip(units, e, m_loc)]
    ones = jnp.ones((L, LANES), F32)
    kvn_loc = [jnp.where(same_head2, _dot_tn(w, jnp.concatenate([part(v_ref, u), ones], axis=1)), 0.0)
               for w, u in zip(wk, units)]

    cn_st = [cn_scr[p] for p in range(n_pairs)]
    m_st = [m_scr[p:p + 1, :] for p in range(n_pairs)]
    cn_prev, m_prev = [], []
    for i, (ci, p) in enumerate(units):
        cn_prev.append(cn_st[p])
        m_prev.append(m_st[p])
        m_new = jnp.maximum(g_last[i] + m_st[p], m_loc[i])
        a_old = jnp.exp(g_last[i] + m_st[p] - m_new)
        a_new = jnp.exp(m_loc[i] - m_new)
        cn_st[p] = (jnp.concatenate([a_old, a_old], axis=1) * cn_st[p]
                    + jnp.concatenate([a_new, a_new], axis=1) * kvn_loc[i])
        m_st[p] = m_new
    for p in range(n_pairs):
        cn_scr[p] = cn_st[p]
        m_scr[p:p + 1, :] = m_st[p]

    inter_log = [g + m for g, m in zip(g_c, m_prev)]
    m_j = [jnp.maximum(dm, il) for dm, il in zip(d_max, inter_log)]
    w_intra = [jnp.exp(d - m) * s for d, m, s in zip(d_log, m_j, qk)]
    w_inter = [jnp.exp(il - m) for il, m in zip(inter_log, m_j)]
    ones_bd = same_head.astype(F32)
    intra = [_dot(w, jnp.concatenate([block_diag(part(v_ref, u)), ones_bd], axis=1))
             for w, u in zip(w_intra, units)]
    inter = [_dot(part(q_ref, u), cn) for u, cn in zip(units, cn_prev)]
    outs = [(ia[:, 0:LANES] + wi * ie[:, 0:LANES])
            / jnp.maximum(jnp.abs(ia[:, LANES:] + wi * ie[:, LANES:]), jnp.exp(-m))
            for ia, wi, ie, m in zip(intra, w_inter, inter, m_j)]
    for ci in range(n_chunks):
        h_ref[pl.ds(ci * L, L), :] = jnp.concatenate(outs[ci * n_pairs:(ci + 1) * n_pairs], axis=1)


def _gate_expand_matrix():
    lane = jnp.arange(LANES)[:, None]
    col = jnp.arange(2 * MLSTM_WIDTH)[None, :]
    src = jnp.where(col < MLSTM_WIDTH, col // HEAD_DIM, MLSTM_HEADS + (col - MLSTM_WIDTH) // HEAD_DIM)
    return (lane == src).astype(BF16)


def _mlstm_call(q, k, v, g):
    B, S, W = q.shape
    ts = SEQ_ROWS
    spec = pl.BlockSpec((None, ts, W), lambda bi, c: (bi, c, 0))
    gspec = pl.BlockSpec((None, ts, LANES), lambda bi, c: (bi, c, 0))
    espec = pl.BlockSpec((LANES, 2 * W), lambda bi, c: (0, 0))
    return pl.pallas_call(
        _mlstm_kernel, grid=(B, S // ts), in_specs=[spec, spec, spec, gspec, espec], out_specs=spec,
        out_shape=jax.ShapeDtypeStruct((B, S, W), F32),
        scratch_shapes=[pltpu.VMEM((MLSTM_HEADS // 2, LANES, 2 * LANES), F32),
                        pltpu.VMEM((8, LANES), F32)],
        compiler_params=pltpu.CompilerParams(dimension_semantics=("arbitrary", "arbitrary"),
                                             vmem_limit_bytes=VMEM_LIMIT),
        name="mlstm",
    )(q, k, v, g, _gate_expand_matrix())


def _layer_norm(z, g, b):
    mu = jnp.mean(z, axis=-1, keepdims=True)
    d = z - mu
    var = jnp.mean(d * d, axis=-1, keepdims=True)
    return d * lax.rsqrt(var + LN_EPS) * g + b


def _post_kernel(h_ref, ypool_ref, yr_ref, bonus_ref, g_ref, hm_ref, mo_ref,
                 lnxg_ref, lnxb_ref, ng_ref, wout_ref, ln1g_ref, ln1b_ref, rwt_ref, rb_ref, ones_ref,
                 h1_ref, gate_ref, route_ref, cnt_ref):
    ts = h_ref.shape[0]
    ones_bd = ones_ref[...]
    y_rwkv = (_head_norm(yr_ref[...], ones_bd, RWKV_GN_EPS) * lnxg_ref[...] + lnxb_ref[...]
              + bonus_ref[...]) * g_ref[...]
    y_ml = mo_ref[...] * (_head_norm(hm_ref[...], ones_bd, LN_EPS) * ng_ref[...])
    mix = (_dot(ypool_ref[...], wout_ref[0:POOL_WIDTH, :])
           + _dot(y_rwkv, wout_ref[POOL_WIDTH:POOL_WIDTH + RWKV_WIDTH, :])
           + _dot(y_ml, wout_ref[POOL_WIDTH + RWKV_WIDTH:, :]))
    h1 = _layer_norm(DEEPNORM_ALPHA * h_ref[...] + mix, ln1g_ref[...], ln1b_ref[...])
    h1_ref[...] = h1

    hh, hl = _split(h1)
    wh, wl = _split(rwt_ref[...])
    nt = lambda a, b: lax.dot_general(a, b, (((1,), (1,)), ((), ())), preferred_element_type=F32)
    logits = nt(wh, hh) + nt(wh, hl) + nt(wl, hh) + rb_ref[...]
    eidx = lax.broadcasted_iota(I32, (N_EXPERTS, ts), 0)
    vals = logits
    tops, hots, idxs = [], [], []
    for _ in range(TOP_K):
        mx = jnp.max(vals, axis=0, keepdims=True)
        idx = jnp.min(jnp.where(vals == mx, eidx, N_EXPERTS), axis=0, keepdims=True)
        hot = eidx == idx
        vals = jnp.where(hot, -jnp.inf, vals)
        tops.append(mx)
        hots.append(hot)
        idxs.append(idx)
    exps = [jnp.exp(t - tops[0]) for t in tops]
    denom = exps[0] + exps[1] + exps[2] + exps[3]
    gate_rows = [e / denom for e in exps]
    gate_ref[...] = jnp.concatenate(gate_rows + [jnp.zeros((8 - TOP_K, ts), F32)], axis=0)

    any_hot = (hots[0] | hots[1] | hots[2] | hots[3])
    hot_f = any_hot.astype(F32)
    r_i = lax.broadcasted_iota(I32, (ts, ts), 0)
    c_i = lax.broadcasted_iota(I32, (ts, ts), 1)
    before = (r_i < c_i).astype(BF16)
    within = jnp.dot(hot_f.astype(BF16), before, preferred_element_type=F32)
    n_e = jnp.broadcast_to(jnp.sum(hot_f, axis=1, keepdims=True), (N_EXPERTS, LANES))
    run_len = jnp.floor((n_e + (RUN_PIECE - 1)) * (1.0 / RUN_PIECE)) * RUN_PIECE
    e_r = lax.broadcasted_iota(I32, (N_EXPERTS, N_EXPERTS), 0)
    e_c = lax.broadcasted_iota(I32, (N_EXPERTS, N_EXPERTS), 1)
    lower = (e_c < e_r).astype(BF16)
    run_start = _dot_rhs2(lower, run_len)[:, 0:1]
    slots = [jnp.sum(jnp.where(hot, run_start + within, 0.0), axis=0, keepdims=True) for hot in hots]
    route_ref[...] = jnp.concatenate([p.astype(I32) for p in slots] + [jnp.zeros((8 - TOP_K, ts), I32)], axis=0)
    cnt_ref[...] = n_e


def _post_call(h, ypool, yr, bonus, g, hm, mo, lw):
    T = h.shape[0]
    ts = PREP_ROWS
    row = lambda c: pl.BlockSpec((ts, c), lambda i: (i, 0))
    colb = lambda r: pl.BlockSpec((r, ts), lambda i: (0, i))
    full = lambda a: pl.BlockSpec(a.shape, lambda i: (0,) * a.ndim)
    params = [lw['lnx_g'], lw['lnx_b'], lw['norm_g'], lw['w_out'], lw['ln1_g'], lw['ln1_b'],
              lw['router_wt'], lw['router_b'], lw['ones_bd']]
    in_specs = ([row(D_MODEL), row(POOL_WIDTH)] + [row(RWKV_WIDTH)] * 5 + [full(a) for a in params])
    out_shape = [jax.ShapeDtypeStruct((T, D_MODEL), F32), jax.ShapeDtypeStruct((8, T), F32),
                 jax.ShapeDtypeStruct((8, T), I32), jax.ShapeDtypeStruct((T // ts * N_EXPERTS, LANES), F32)]
    out_specs = [row(D_MODEL), colb(8), colb(8), pl.BlockSpec((N_EXPERTS, LANES), lambda i: (i, 0))]
    return pl.pallas_call(
        _post_kernel, grid=(T // ts,), in_specs=in_specs, out_specs=out_specs, out_shape=out_shape,
        compiler_params=pltpu.CompilerParams(dimension_semantics=("arbitrary",),
                                             vmem_limit_bytes=VMEM_LIMIT),
        name="post",
    )(h, ypool, yr, bonus, g, hm, mo, *params)


def _rows_copy(src_ref, src_row, dst_ref, dst_row, n_rows, sem):
    src_row, dst_row = [r if isinstance(r, int) else pl.multiple_of(r, SUBLANES) for r in (src_row, dst_row)]
    return pltpu.make_async_copy(src_ref.at[pl.ds(src_row, n_rows), :], dst_ref.at[pl.ds(dst_row, n_rows), :], sem)


def _stage_rows(ts):
    n = TOP_K * ts + N_EXPERTS * (RUN_PIECE - 1)
    return -(-n // SUBLANES) * SUBLANES


def _for_each_piece(n_pieces, body):
    lax.fori_loop(0, n_pieces, lambda j, c: (body(j), c)[1], 0)


def _run_copies(tab_ref, src_of, dst_of, sem, wait):
    for e in range(N_EXPERTS):
        n_rows = pl.multiple_of(tab_ref[0, e], SUBLANES)

        @pl.when(n_rows > 0)
        def _(e=e, n_rows=n_rows):
            src_ref, src_row = src_of(e)
            dst_ref, dst_row = dst_of(e)
            cp = _rows_copy(src_ref, src_row, dst_ref, dst_row, n_rows, sem)
            cp.wait() if wait else cp.start()


def _dispatch_kernel(tab_ref, prev_tab_ref, fill_ref, route_ref, gate_ref, h_ref, buf_ref,
                     sorted_scr, zero_scr, sems):
    step = pl.program_id(0)
    last = pl.num_programs(0) - 1
    slot = step % 2
    ts = h_ref.shape[0]
    n_sorted = sorted_scr.shape[1]

    @pl.when(step == 0)
    def _():
        zero_scr[...] = jnp.zeros(zero_scr.shape, F32)

        def fill(wait):
            for e in range(N_EXPERTS):
                def piece(j, e=e):
                    cp = _rows_copy(zero_scr, 0, buf_ref, fill_ref[0, e] + j * FILL_PIECE, FILL_PIECE, sems.at[0])
                    cp.wait() if wait else cp.start()
                _for_each_piece(fill_ref[0, N_EXPERTS + e], piece)

            def tail(j):
                cp = _rows_copy(zero_scr, 0, buf_ref, fill_ref[0, 2 * N_EXPERTS] + j * EXPERT_ROWS,
                                EXPERT_ROWS, sems.at[0])
                cp.wait() if wait else cp.start()
            _for_each_piece(fill_ref[0, 2 * N_EXPERTS + 1], tail)

        fill(False)
        fill(True)

    pos = lax.broadcasted_iota(I32, (n_sorted, ts), 0)
    hot = None
    weight = None
    for kslot in range(TOP_K):
        eq = pos == route_ref[kslot:kslot + 1, :]
        w = jnp.where(eq, gate_ref[kslot:kslot + 1, :], 0.0)
        hot = eq if hot is None else (hot | eq)
        weight = w if weight is None else weight + w
    sorted_scr[slot, :, 0:D_MODEL] = jnp.dot(hot.astype(BF16), h_ref[...].astype(BF16),
                                             preferred_element_type=F32)
    sorted_scr[slot, :, D_MODEL:BUF_WIDTH] = _dot_lhs2(weight, jnp.ones((ts, LANES), BF16))

    def copies(tab, which, wait):
        _run_copies(tab, lambda e: (sorted_scr.at[which], tab[0, N_EXPERTS + e]),
                    lambda e: (buf_ref, tab[0, 2 * N_EXPERTS + e]), sems.at[which], wait)

    @pl.when(step > 0)
    def _():
        copies(prev_tab_ref, 1 - slot, True)

    copies(tab_ref, slot, False)

    @pl.when(step == last)
    def _():
        copies(tab_ref, slot, True)


def _dispatch_call(tab_d, fill_tab, route8, gate8, h1, n_rows):
    T = h1.shape[0]
    ts = PREP_ROWS
    smem_tile = pl.BlockSpec((None, 1, LANES), lambda i: (i, 0, 0), memory_space=pltpu.SMEM)
    smem_prev = pl.BlockSpec((None, 1, LANES), lambda i: (jnp.maximum(i - 1, 0), 0, 0), memory_space=pltpu.SMEM)
    return pl.pallas_call(
        _dispatch_kernel, grid=(T // ts,),
        in_specs=[smem_tile, smem_prev,
                  pl.BlockSpec((1, LANES), lambda i: (0, 0), memory_space=pltpu.SMEM),
                  pl.BlockSpec((8, ts), lambda i: (0, i)),
                  pl.BlockSpec((8, ts), lambda i: (0, i)),
                  pl.BlockSpec((ts, D_MODEL), lambda i: (i, 0))],
        out_specs=pl.BlockSpec(memory_space=pl.ANY),
        out_shape=jax.ShapeDtypeStruct((n_rows, BUF_WIDTH), F32),
        scratch_shapes=[pltpu.VMEM((2, _stage_rows(ts), BUF_WIDTH), F32),
                        pltpu.VMEM((EXPERT_ROWS, BUF_WIDTH), F32),
                        pltpu.SemaphoreType.DMA((2,))],
        compiler_params=pltpu.CompilerParams(dimension_semantics=("arbitrary",),
                                             vmem_limit_bytes=VMEM_LIMIT),
        name="dispatch",
    )(tab_d, tab_d, fill_tab, route8, gate8, h1)


def _expert_kernel(be_ref, nb_ref, run_ref, next_ref, x_ref, wgu_hbm, bgu_ref, wdn_hbm, bdn_ref, o_ref,
                   wgu_f32, wdn_f32, wgu_scr, wdn_scr, sems, *, layer):
    j = pl.program_id(0)
    used = j < nb_ref[0]
    changed = jnp.logical_or(j == 0, be_ref[j] != be_ref[jnp.maximum(j - 1, 0)])

    def fetch(expert, slot):
        return (pltpu.make_async_copy(wgu_hbm.at[layer, expert], wgu_f32.at[slot], sems.at[0, slot]),
                pltpu.make_async_copy(wdn_hbm.at[layer, expert], wdn_f32.at[slot], sems.at[1, slot]))

    @pl.when(j == 0)
    def _():
        for cp in fetch(be_ref[0], 0):
            cp.start()

    @pl.when(jnp.logical_and(used, changed))
    def _():
        slot = run_ref[j] % 2
        for cp in fetch(be_ref[j], slot):
            cp.wait()
        wgu_scr[...] = wgu_f32[slot].astype(BF16)
        wdn_scr[...] = wdn_f32[slot].astype(BF16)

        @pl.when(next_ref[j] >= 0)
        def _():
            for cp in fetch(next_ref[j], 1 - slot):
                cp.start()

    @pl.when(used)
    def _():
        gu = jnp.dot(x_ref[:, 0:D_MODEL].astype(BF16), wgu_scr[...], preferred_element_type=F32) + bgu_ref[...]
        glu = jnp.minimum(gu[:, 0:D_FF], SWIGLU_LIMIT)
        lin = jnp.clip(gu[:, D_FF:], -SWIGLU_LIMIT, SWIGLU_LIMIT)
        act = glu * _sigmoid(SWIGLU_ALPHA * glu) * (lin + 1.0)
        y = jnp.dot(act.astype(BF16), wdn_scr[...], preferred_element_type=F32) + bdn_ref[...]
        gate = x_ref[:, D_MODEL:BUF_WIDTH]
        o_ref[...] = y * jnp.concatenate([gate] * (D_MODEL // LANES), axis=1)

    @pl.when(jnp.logical_not(used))
    def _():
        o_ref[...] = jnp.zeros(o_ref.shape, F32)


def _expert_call(block_e, n_used, buf, lw):
    n_rows = buf.shape[0]
    rows = EXPERT_ROWS
    n_blocks = n_rows // rows
    j = jnp.arange(n_blocks, dtype=I32)
    first = (j < n_used[0]) & ((j == 0) | (block_e != jnp.roll(block_e, 1)))
    run_idx = jnp.cumsum(first.astype(I32)) - 1
    later_first = lax.cummin(jnp.where(first, j, n_blocks)[::-1])[::-1]
    next_start = jnp.concatenate([later_first[1:], jnp.full((1,), n_blocks, I32)])
    next_e = jnp.where(next_start < n_blocks, block_e[jnp.minimum(next_start, n_blocks - 1)], -1).astype(I32)

    def blk(j, be, nb, run, nxt):
        return jnp.minimum(j, nb[0] - 1)

    layer = lw['layer']
    grid_spec = pltpu.PrefetchScalarGridSpec(
        num_scalar_prefetch=4, grid=(n_blocks,),
        in_specs=[pl.BlockSpec((rows, BUF_WIDTH), lambda j, *s: (blk(j, *s), 0)),
                  pl.BlockSpec(memory_space=pl.ANY),
                  pl.BlockSpec((None, None, 1, 2 * D_FF), lambda j, be, *s: (layer, be[j], 0, 0)),
                  pl.BlockSpec(memory_space=pl.ANY),
                  pl.BlockSpec((None, None, 1, D_MODEL), lambda j, be, *s: (layer, be[j], 0, 0))],
        out_specs=pl.BlockSpec((rows, D_MODEL), lambda j, *s: (j, 0)),
        scratch_shapes=[pltpu.VMEM((2, D_MODEL, 2 * D_FF), F32), pltpu.VMEM((2, D_FF, D_MODEL), F32),
                        pltpu.VMEM((D_MODEL, 2 * D_FF), BF16), pltpu.VMEM((D_FF, D_MODEL), BF16),
                        pltpu.SemaphoreType.DMA((2, 2))])
    return pl.pallas_call(
        functools.partial(_expert_kernel, layer=layer), grid_spec=grid_spec,
        out_shape=jax.ShapeDtypeStruct((n_rows, D_MODEL), F32),
        compiler_params=pltpu.CompilerParams(dimension_semantics=("arbitrary",),
                                             vmem_limit_bytes=56 * 1024 * 1024),
        name="expert",
    )(block_e, n_used, run_idx, next_e, buf, lw['w_gate_up'], lw['b_gate_up'], lw['w_down'], lw['b_down'])


def _combine_kernel(tab_ref, next_tab_ref, route_ref, h_ref, out_hbm_ref, ln2g_ref, ln2b_ref, o_ref,
                    stage_scr, sems):
    step = pl.program_id(0)
    last = pl.num_programs(0) - 1
    slot = step % 2
    ts = h_ref.shape[0]
    n_stage = stage_scr.shape[1]

    def copies(tab, which, wait):
        _run_copies(tab, lambda e: (out_hbm_ref, tab[0, 2 * N_EXPERTS + e]),
                    lambda e: (stage_scr.at[which], tab[0, N_EXPERTS + e]), sems.at[which], wait)

    @pl.when(step == 0)
    def _():
        stage_scr[...] = jnp.zeros(stage_scr.shape, F32)
        copies(tab_ref, 0, False)

    @pl.when(step < last)
    def _():
        copies(next_tab_ref, 1 - slot, False)

    slot_f = route_ref[...].astype(F32)
    slot_cols = jnp.concatenate([slot_f, jnp.zeros((LANES - 8, ts), F32)], axis=0).T
    lane = lax.broadcasted_iota(I32, (ts, n_stage), 1).astype(F32)
    sel = None
    for kslot in range(TOP_K):
        eq = lane == slot_cols[:, kslot:kslot + 1]
        sel = eq if sel is None else (sel | eq)
    copies(tab_ref, slot, True)
    y = jnp.dot(sel.astype(BF16), stage_scr[slot].astype(BF16), preferred_element_type=F32)
    o_ref[...] = _layer_norm(DEEPNORM_ALPHA * h_ref[...] + y, ln2g_ref[...], ln2b_ref[...])


def _combine_call(tab_c, route8, h1, out_rows, lw):
    T = h1.shape[0]
    ts = PREP_ROWS
    n_tiles = T // ts
    full = lambda a: pl.BlockSpec(a.shape, lambda i: (0,) * a.ndim)
    n_stage = _stage_rows(ts)
    return pl.pallas_call(
        _combine_kernel, grid=(n_tiles,),
        in_specs=[pl.BlockSpec((None, 1, LANES), lambda i: (i, 0, 0), memory_space=pltpu.SMEM),
                  pl.BlockSpec((None, 1, LANES), lambda i: (jnp.minimum(i + 1, n_tiles - 1), 0, 0),
                               memory_space=pltpu.SMEM),
                  pl.BlockSpec((8, ts), lambda i: (0, i)),
                  pl.BlockSpec((ts, D_MODEL), lambda i: (i, 0)),
                  pl.BlockSpec(memory_space=pl.ANY),
                  full(lw['ln2_g']), full(lw['ln2_b'])],
        out_specs=pl.BlockSpec((ts, D_MODEL), lambda i: (i, 0)),
        out_shape=jax.ShapeDtypeStruct((T, D_MODEL), F32),
        scratch_shapes=[pltpu.VMEM((2, n_stage, D_MODEL), F32), pltpu.SemaphoreType.DMA((2,))],
        compiler_params=pltpu.CompilerParams(dimension_semantics=("arbitrary",),
                                             vmem_limit_bytes=VMEM_LIMIT),
        name="combine",
    )(tab_c, tab_c, route8, h1, out_rows, lw['ln2_g'], lw['ln2_b'])


def _block_diag_ones(width):
    hid = jnp.arange(width) // HEAD_DIM
    return (hid[:, None] == hid[None, :]).astype(BF16)


def _layer_params(l, w_in, pool_w, pool_scale, rwkv_mu, rwkv_w0, rwkv_w2, rwkv_a0, rwkv_a2, rwkv_g2,
                  rwkv_kk_scale, rwkv_ka, rwkv_rk, rwkv_lnx_g, rwkv_lnx_b, rwkv_v0, rwkv_v1, rwkv_v2,
                  mlstm_conv_w, mlstm_conv_b, mlstm_b_i, mlstm_b_f, mlstm_norm_g, w_out, ln1_g, ln1_b,
                  router_w, router_b, w_gate_up, b_gate_up, w_down, b_down, ln2_g, ln2_b):
    row = lambda a: a.reshape(1, -1).astype(F32)
    pad_cols = D_IN_PAD - D_IN
    if l > 0:
        extra = jnp.concatenate([rwkv_v1[l - 1], jnp.zeros((D_MODEL, pad_cols - VRES_LORA), F32)], axis=1)
        v0 = row(rwkv_v0[l - 1])
        v2 = jnp.zeros((LANES, RWKV_WIDTH), F32).at[VRES_OFF:VRES_OFF + VRES_LORA].set(rwkv_v2[l - 1])
    else:
        extra = jnp.zeros((D_MODEL, pad_cols), F32)
        v0 = jnp.zeros((1, RWKV_WIDTH), F32)
        v2 = jnp.zeros((LANES, RWKV_WIDTH), F32)
    pw = jnp.zeros((POOL_WIDTH, POOL_WIDTH), F32)
    for gi in range(len(POOL_WINDOWS)):
        sl = slice(gi * POOL_GROUP, (gi + 1) * POOL_GROUP)
        pw = pw.at[sl, sl].set(pool_w[l, gi])
    zero_lora = jnp.zeros((DECAY_LORA, RWKV_WIDTH), F32)
    gate_bias = jnp.zeros((1, LANES), F32).at[0, 0:MLSTM_HEADS].set(mlstm_b_i[l])
    gate_bias = gate_bias.at[0, MLSTM_HEADS:2 * MLSTM_HEADS].set(mlstm_b_f[l])
    return {
        'w_in': jnp.concatenate([w_in[l], extra], axis=1).astype(BF16),
        'pool_w': pw.astype(BF16), 'pool_scale': row(pool_scale[l]), 'mu': row(rwkv_mu[l]),
        'w0': row(rwkv_w0[l]), 'w2': jnp.concatenate([rwkv_w2[l], zero_lora], axis=0).astype(BF16),
        'a0': row(rwkv_a0[l]), 'a2': jnp.concatenate([zero_lora, rwkv_a2[l]], axis=0).astype(BF16),
        'g2': rwkv_g2[l].astype(BF16), 'kk_scale': row(rwkv_kk_scale[l]), 'ka': row(rwkv_ka[l]),
        'rk': row(rwkv_rk[l]), 'v0': v0, 'v2': v2.astype(BF16),
        'conv_w': mlstm_conv_w[l], 'conv_b': row(mlstm_conv_b[l]), 'gate_bias': gate_bias,
        'ones_bd': _block_diag_ones(RWKV_WIDTH),
        'lnx_g': row(rwkv_lnx_g[l]), 'lnx_b': row(rwkv_lnx_b[l]), 'norm_g': row(mlstm_norm_g[l]),
        'w_out': w_out[l].astype(BF16), 'ln1_g': row(ln1_g[l]), 'ln1_b': row(ln1_b[l]),
        'router_wt': router_w[l].T, 'router_b': router_b[l].reshape(N_EXPERTS, 1),
        'layer': l, 'w_gate_up': w_gate_up, 'b_gate_up': b_gate_up.reshape(-1, N_EXPERTS, 1, 2 * D_FF),
        'w_down': w_down, 'b_down': b_down.reshape(-1, N_EXPERTS, 1, D_MODEL),
        'ln2_g': row(ln2_g[l]), 'ln2_b': row(ln2_b[l]),
    }


def _layer(h, v_first, lw, *, has_vres):
    B, S, _ = h.shape
    T = B * S
    outs = _prep_call(h, lw, v_first, has_vres=has_vres)
    ypool, r, ld, k, v, kk, b, g, bonus, mq, mk, mv, mo, mg = outs
    yr = _rwkv_call(r, ld, k, v, kk, b)
    hm = _mlstm_call(mq, mk, mv, mg)
    flat = lambda a: a.reshape(T, a.shape[-1])
    h_flat = flat(h)
    h1, gate8, route8, cnt = _post_call(h_flat, flat(ypool), flat(yr), flat(bonus), flat(g), flat(hm),
                                        flat(mo), lw)
    n_tiles = T // PREP_ROWS
    cnt = cnt.reshape(n_tiles, N_EXPERTS, LANES)[:, :, 0].astype(I32)
    rows = EXPERT_ROWS
    run_len = (cnt + RUN_PIECE - 1) // RUN_PIECE * RUN_PIECE
    region = jnp.sum(run_len, axis=0)
    padded = (region + rows - 1) // rows * rows
    pad_end = jnp.cumsum(padded)
    pad_start = pad_end - padded
    n_blocks = -(-(T * TOP_K + n_tiles * N_EXPERTS * (RUN_PIECE - 1)) // rows) + N_EXPERTS
    n_used = (pad_end[-1] // rows).astype(I32).reshape(1)
    starts = jnp.minimum(jnp.arange(n_blocks, dtype=I32), n_used[0] - 1) * rows
    block_e = jnp.sum((pad_end[None, :] <= starts[:, None]).astype(I32), axis=1)
    run_src = jnp.cumsum(run_len, axis=1) - run_len
    run_dst = pad_start[None, :] + jnp.cumsum(run_len, axis=0) - run_len
    run_tab = jnp.concatenate([run_len, run_src, run_dst,
                               jnp.zeros((n_tiles, LANES - 3 * N_EXPERTS), I32)], axis=1).reshape(n_tiles, 1, LANES)
    fill_start = (pad_start + region) // FILL_PIECE * FILL_PIECE
    fill_tab = jnp.concatenate([fill_start, (pad_end - fill_start) // FILL_PIECE,
                                pad_end[-1:], n_blocks - n_used,
                                jnp.zeros((LANES - 2 * N_EXPERTS - 2,), I32)]).reshape(1, LANES)
    buf = _dispatch_call(run_tab, fill_tab, route8, gate8, h1, n_blocks * rows)
    out_rows = _expert_call(block_e, n_used, buf, lw)
    h2 = _combine_call(run_tab, route8, h1, out_rows, lw)
    return h2.reshape(B, S, D_MODEL), v


def kernel(x, w_in, pool_w, pool_scale, rwkv_mu, rwkv_w0, rwkv_w2, rwkv_a0, rwkv_a2, rwkv_g2, rwkv_kk_scale, rwkv_ka, rwkv_rk, rwkv_lnx_g, rwkv_lnx_b, rwkv_v0, rwkv_v1, rwkv_v2, mlstm_conv_w, mlstm_conv_b, mlstm_b_i, mlstm_b_f, mlstm_norm_g, w_out, ln1_g, ln1_b, router_w, router_b, w_gate_up, b_gate_up, w_down, b_down, ln2_g, ln2_b):
    weights = (w_in, pool_w, pool_scale, rwkv_mu, rwkv_w0, rwkv_w2, rwkv_a0, rwkv_a2, rwkv_g2, rwkv_kk_scale,
               rwkv_ka, rwkv_rk, rwkv_lnx_g, rwkv_lnx_b, rwkv_v0, rwkv_v1, rwkv_v2, mlstm_conv_w, mlstm_conv_b,
               mlstm_b_i, mlstm_b_f, mlstm_norm_g, w_out, ln1_g, ln1_b, router_w, router_b, w_gate_up,
               b_gate_up, w_down, b_down, ln2_g, ln2_b)
    h = x
    v_first = jnp.zeros(x.shape[:2] + (RWKV_WIDTH,), F32)
    for l in range(w_in.shape[0]):
        lw = _layer_params(l, *weights)
        h, v_l = _layer(h, v_first, lw, has_vres=l > 0)
        if l == 0:
            v_first = v_l
    return h
```

```python
import functools

import jax
import jax.numpy as jnp
from jax import lax
from jax.experimental import pallas as pl
from jax.experimental.pallas import tpu as pltpu

F32 = jnp.float32
BF16 = jnp.bfloat16
I32 = jnp.int32

D_MODEL = 1024
HEAD_DIM = 64
POOL_WINDOWS = (2, 4, 8, 16)
POOL_WIDTH = 256
POOL_GROUP = 64
RWKV_WIDTH = 384
RWKV_HEADS = 6
DECAY_LORA = 64
ICLR_LORA = 64
GATE_LORA = 128
VRES_LORA = 32
RWKV_GN_EPS = 64e-5
RWKV_COLS = 3 * RWKV_WIDTH + DECAY_LORA + ICLR_LORA + GATE_LORA
MLSTM_WIDTH = 384
MLSTM_HEADS = 6
MLSTM_CONV = 4
MLSTM_COLS = 4 * MLSTM_WIDTH + 2 * MLSTM_HEADS
D_IN = POOL_WIDTH + RWKV_COLS + MLSTM_COLS
N_EXPERTS = 32
TOP_K = 4
D_FF = D_MODEL
SWIGLU_LIMIT = 7.0
SWIGLU_ALPHA = 1.702
LN_EPS = 1e-5
DEPTH = 2
DEEPNORM_ALPHA = (2 * DEPTH) ** 0.25

LANES = 128
D_IN_PAD = 3328
RWKV_OFF = POOL_WIDTH
MLSTM_OFF = POOL_WIDTH + RWKV_COLS
GATE_OFF = MLSTM_OFF + 4 * MLSTM_WIDTH
VRES_OFF = 2 * MLSTM_HEADS
HALO = 16
CHUNK = 64

PROJ_ROWS = 512
PREP_ROWS = 256
POST_TILES = 4
SEQ_ROWS = 512
EXPERT_ROWS = 512
BUF_WIDTH = D_MODEL + LANES
SUBLANES = 8
RUN_PIECE = SUBLANES
FILL_PIECE = 64
VMEM_LIMIT = 48 * 1024 * 1024


def _dot(a, b):
    return jnp.dot(a.astype(BF16), b.astype(BF16), preferred_element_type=F32)


def _dot_nt(a, b):
    return lax.dot_general(a.astype(BF16), b.astype(BF16), (((1,), (1,)), ((), ())),
                           preferred_element_type=F32)


def _dot_tn(a, b):
    return lax.dot_general(a.astype(BF16), b.astype(BF16), (((0,), (0,)), ((), ())),
                           preferred_element_type=F32)


def _split(x):
    hi = x.astype(BF16)
    lo = (x - hi.astype(F32)).astype(BF16)
    return hi, lo


def _dot_lhs2(a, b_bf16):
    hi, lo = _split(a)
    return (jnp.dot(hi, b_bf16, preferred_element_type=F32)
            + jnp.dot(lo, b_bf16, preferred_element_type=F32))


def _dot_rhs2(a_bf16, b):
    hi, lo = _split(b)
    return (jnp.dot(a_bf16, hi, preferred_element_type=F32)
            + jnp.dot(a_bf16, lo, preferred_element_type=F32))


def _sigmoid(x):
    return 1.0 / (1.0 + jnp.exp(-x))


def _softplus(x):
    return jnp.maximum(x, 0.0) + jnp.log(1.0 + jnp.exp(-jnp.abs(x)))


def _head_norm(y, ones_bd, eps):
    inv = 1.0 / HEAD_DIM
    mean = _dot(y, ones_bd) * inv
    d = y - mean
    var = _dot(d * d, ones_bd) * inv
    return d * lax.rsqrt(var + eps)


def _prep_kernel(x_ref, w_ref, poolw_ref, pscale_ref, mu_ref, w0_ref, w2_ref, a0_ref, a2_ref, g2_ref,
                 kks_ref, ka_ref, rk_ref, v0_ref, v2_ref, vfirst_ref, cw_ref, cb_ref, gbias_ref, ones_ref,
                 ypool_ref, r_ref, ld_ref, k_ref, v_ref, kk_ref, b_ref, g_ref, bonus_ref,
                 mq_ref, mk_ref, mv_ref, mo_ref, mg_ref,
                 p_scr, *, has_vres):
    i = pl.program_id(1)
    ts = x_ref.shape[0]

    @pl.when(i == 0)
    def _():
        p_scr[0:HALO, :] = jnp.zeros((HALO, D_IN_PAD), F32)

    @pl.when(i > 0)
    def _():
        p_scr[0:HALO, :] = p_scr[ts:ts + HALO, :]

    p_scr[HALO:HALO + ts, :] = jnp.dot(x_ref[...].astype(BF16), w_ref[...], preferred_element_type=F32)

    def rows(shift, c0, c1):
        return p_scr[HALO - shift:HALO - shift + ts, c0:c1]

    u = rows(0, 0, POOL_WIDTH)
    acc = u
    sums = {}
    for s in range(1, POOL_WINDOWS[-1]):
        acc = acc + rows(s, 0, POOL_WIDTH)
        if s + 1 in POOL_WINDOWS:
            sums[s + 1] = acc
    pos = (i * ts + lax.broadcasted_iota(I32, (ts, 1), 0) + 1).astype(F32)
    lane = lax.broadcasted_iota(I32, (ts, POOL_WIDTH), 1)
    d = None
    for gi, win in reversed(list(enumerate(POOL_WINDOWS))):
        dg = sums[win] / jnp.minimum(pos, float(win))
        d = dg if d is None else jnp.where(lane < (gi + 1) * POOL_GROUP, dg, d)
    d = d - u
    ypool_ref[...] = _dot(d, poolw_ref[...]) * pscale_ref[...]

    cur = rows(0, RWKV_OFF, RWKV_OFF + RWKV_COLS)
    prev = rows(1, RWKV_OFF, RWKV_OFF + RWKV_COLS)
    pf = cur + mu_ref[...] * (prev - cur)
    W = RWKV_WIDTH
    r = pf[:, 0:W]
    k = pf[:, W:2 * W]
    v = pf[:, 2 * W:3 * W]
    z = pf[:, 3 * W:3 * W + LANES]
    gd = pf[:, 3 * W + LANES:3 * W + 2 * LANES]
    w_log = -_softplus(-(w0_ref[...] + _dot(jnp.tanh(z), w2_ref[...]))) - 0.5
    ld_ref[...] = -jnp.exp(w_log)
    a = _sigmoid(a0_ref[...] + _dot(z, a2_ref[...]))
    g_ref[...] = _dot(_sigmoid(gd), g2_ref[...])
    gates = rows(0, GATE_OFF, GATE_OFF + LANES)
    if has_vres:
        v_gate = _sigmoid(v0_ref[...] + _dot(gates, v2_ref[...]))
        v = v + (vfirst_ref[...] - v) * v_gate
    ones_bd = ones_ref[...]
    kk = k * kks_ref[...]
    kk = kk / jnp.maximum(jnp.sqrt(_dot(kk * kk, ones_bd)), 1e-12)
    k = k * (1.0 + (a - 1.0) * ka_ref[...])
    r_ref[...] = r
    k_ref[...] = k
    v_ref[...] = v
    kk_ref[...] = kk
    b_ref[...] = kk * a
    bonus_ref[...] = _dot(r * k * rk_ref[...], ones_bd) * v

    qk = cb_ref[...] + rows(0, MLSTM_OFF, MLSTM_OFF + 2 * MLSTM_WIDTH) * cw_ref[MLSTM_CONV - 1:MLSTM_CONV, :]
    for tap in range(MLSTM_CONV - 1):
        shift = MLSTM_CONV - 1 - tap
        qk = qk + rows(shift, MLSTM_OFF, MLSTM_OFF + 2 * MLSTM_WIDTH) * cw_ref[tap:tap + 1, :]
    qk = qk * _sigmoid(qk)
    mq_ref[...] = qk[:, 0:MLSTM_WIDTH] * (HEAD_DIM ** -0.5)
    mk_ref[...] = qk[:, MLSTM_WIDTH:]
    mv_ref[...] = rows(0, MLSTM_OFF + 2 * MLSTM_WIDTH, MLSTM_OFF + 3 * MLSTM_WIDTH)
    mo_ref[...] = _sigmoid(rows(0, MLSTM_OFF + 3 * MLSTM_WIDTH, MLSTM_OFF + 4 * MLSTM_WIDTH))
    gb = gates + gbias_ref[...]
    glane = lax.broadcasted_iota(I32, (ts, LANES), 1)
    mg_ref[...] = jnp.where(glane < MLSTM_HEADS, gb, -_softplus(-gb))


def _prep_call(x, lw, vfirst, *, has_vres):
    B, S, _ = x.shape
    ts = PROJ_ROWS
    grid = (B, S // ts)
    row3 = lambda c: pl.BlockSpec((None, ts, c), lambda b, i: (b, i, 0))
    full = lambda a: pl.BlockSpec(a.shape, lambda b, i: (0,) * a.ndim)
    params = [lw['w_in'], lw['pool_w'], lw['pool_scale'], lw['mu'], lw['w0'], lw['w2'], lw['a0'], lw['a2'],
              lw['g2'], lw['kk_scale'], lw['ka'], lw['rk'], lw['v0'], lw['v2']]
    tail = [lw['conv_w'], lw['conv_b'], lw['gate_bias'], lw['ones_bd']]
    in_specs = ([row3(D_MODEL)] + [full(a) for a in params] + [row3(RWKV_WIDTH)] + [full(a) for a in tail])
    widths = [POOL_WIDTH] + [RWKV_WIDTH] * 8 + [MLSTM_WIDTH] * 4 + [LANES]
    out_shape = [jax.ShapeDtypeStruct((B, S, c), F32) for c in widths]
    out_specs = [row3(c) for c in widths]
    return pl.pallas_call(
        functools.partial(_prep_kernel, has_vres=has_vres),
        grid=grid, in_specs=in_specs, out_specs=out_specs, out_shape=out_shape,
        scratch_shapes=[pltpu.VMEM((HALO + ts, D_IN_PAD), F32)],
        compiler_params=pltpu.CompilerParams(dimension_semantics=("arbitrary", "arbitrary"),
                                             vmem_limit_bytes=VMEM_LIMIT),
        name="prep",
    )(x, *params, vfirst, *tail)


def _rwkv_kernel(r_ref, ld_ref, k_ref, v_ref, kk_ref, b_ref, y_ref, s_scr):
    c = pl.program_id(1)
    L = CHUNK
    n_chunks = r_ref.shape[0] // L

    @pl.when(c == 0)
    def _():
        s_scr[...] = jnp.zeros(s_scr.shape, F32)

    row = lax.broadcasted_iota(I32, (L, L), 0)
    col = lax.broadcasted_iota(I32, (L, L), 1)
    tri = (col <= row).astype(BF16)
    row_l = lax.broadcasted_iota(I32, (L, LANES), 0)
    col_l = lax.broadcasted_iota(I32, (L, LANES), 1) % HEAD_DIM
    strict = col_l < row_l
    incl = col_l <= row_l
    eye = (col_l == row_l).astype(F32)
    r2 = lax.broadcasted_iota(I32, (LANES, LANES), 0)
    c2 = lax.broadcasted_iota(I32, (LANES, LANES), 1)
    same_head = (r2 // HEAD_DIM) == (c2 // HEAD_DIM)
    eye2 = (r2 == c2).astype(F32)
    low_lanes = lax.broadcasted_iota(I32, (L, LANES), 1) < HEAD_DIM
    n_pairs = RWKV_HEADS // 2

    def block_diag(x):
        xb = x.astype(BF16)
        return jnp.where(same_head, jnp.concatenate([xb, xb], axis=0), jnp.zeros((), BF16))

    def diag_blocks(z):
        return jnp.where(low_lanes, z[0:L, :], z[L:2 * L, :])

    units = [(ci, p) for ci in range(n_chunks) for p in range(n_pairs)]
    per_chunk = []
    for ci in range(n_chunks):
        sl = pl.ds(ci * L, L)
        ld = ld_ref[sl, :]
        cin = _dot_rhs2(tri, ld)
        c_last = cin[L - 1:L, :]
        e_neg = jnp.exp(-cin)
        e_tail = jnp.exp(c_last - cin)
        kk = kk_ref[sl, :]
        bb = b_ref[sl, :]
        kx = k_ref[sl, :]
        per_chunk.append(dict(
            A=-kk * jnp.exp(cin - ld), R=r_ref[sl, :] * jnp.exp(cin), B=bb * e_neg, K=kx * e_neg,
            Bh=bb * e_tail, Kh=kx * e_tail, V=v_ref[sl, :], g_last=jnp.exp(c_last)))

    def part(name, u):
        ci, p = u
        return per_chunk[ci][name][:, p * LANES:(p + 1) * LANES]

    M = [_dot_nt(jnp.concatenate([part('A', u), part('R', u)], axis=0),
                 jnp.concatenate([block_diag(part('B', u)), block_diag(part('K', u))], axis=0)) for u in units]
    m_ab = [jnp.where(strict, m[0:L, 0:LANES], 0.0) for m in M]
    m_ak = [jnp.where(strict, m[0:L, LANES:], 0.0) for m in M]
    m_rb = [jnp.where(incl, m[L:, 0:LANES], 0.0) for m in M]
    m_rk = [jnp.where(incl, m[L:, LANES:], 0.0) for m in M]
    MVYK = [_dot(jnp.concatenate([ak, rk], axis=0), block_diag(part('V', u)))
            for ak, rk, u in zip(m_ak, m_rk, units)]
    MV = [x[0:L, :] for x in MVYK]
    YK = [x[L:, :] for x in MVYK]
    T = [eye + m for m in m_ab]
    pw = [_dot(m, block_diag(m)) for m in m_ab]
    for _ in range(4):
        both = [_dot(jnp.concatenate([t, p], axis=0), block_diag(p)) for t, p in zip(T, pw)]
        T = [t + x[0:L, :] for t, x in zip(T, both)]
        pw = [x[L:, :] for x in both]
    T = [t + _dot(t, block_diag(p)) for t, p in zip(T, pw)]
    WU = [_dot(t, jnp.concatenate([block_diag(part('A', u)), block_diag(mv)], axis=1))
          for t, u, mv in zip(T, units, MV)]
    GY = [_dot(m, jnp.concatenate([block_diag(wu[:, 0:LANES]), block_diag(wu[:, LANES:])], axis=1))
          for m, wu in zip(m_rb, WU)]
    G = [part('R', u) + gy[:, 0:LANES] for u, gy in zip(units, GY)]
    Y0 = [gy[:, LANES:] + yk for gy, yk in zip(GY, YK)]
    P = [jnp.where(same_head, _dot_tn(wu[:, 0:LANES], part('Bh', u)), 0.0) + eye2 * part('g_last', u)
         for wu, u in zip(WU, units)]
    Q = [diag_blocks(_dot_tn(jnp.concatenate([wu[:, LANES:], part('V', u)], axis=0),
                             jnp.concatenate([part('Bh', u), part('Kh', u)], axis=0)))
         for wu, u in zip(WU, units)]

    state = [s_scr[p] for p in range(n_pairs)]
    for ci in range(n_chunks):
        base = ci * n_pairs
        ys = [_dot_nt(G[base + p], block_diag(state[p])) + Y0[base + p] for p in range(n_pairs)]
        state = [_dot(state[p], P[base + p]) + Q[base + p] for p in range(n_pairs)]
        y_ref[pl.ds(ci * L, L), :] = jnp.concatenate(ys, axis=1)
    for p in range(n_pairs):
        s_scr[p] = state[p]


def _rwkv_call(r, ld, k, v, kk, b):
    B, S, W = r.shape
    ts = SEQ_ROWS
    spec = pl.BlockSpec((None, ts, W), lambda bi, c: (bi, c, 0))
    return pl.pallas_call(
        _rwkv_kernel, grid=(B, S // ts), in_specs=[spec] * 6, out_specs=spec,
        out_shape=jax.ShapeDtypeStruct((B, S, W), F32),
        scratch_shapes=[pltpu.VMEM((RWKV_HEADS // 2, HEAD_DIM, LANES), F32)],
        compiler_params=pltpu.CompilerParams(dimension_semantics=("arbitrary", "arbitrary"),
                                             vmem_limit_bytes=VMEM_LIMIT),
        name="rwkv",
    )(r, ld, k, v, kk, b)


def _mlstm_kernel(q_ref, k_ref, v_ref, g_ref, expand_ref, h_ref, cn_scr, m_scr):
    c = pl.program_id(1)
    L = CHUNK
    n_chunks = q_ref.shape[0] // L
    H = MLSTM_HEADS
    n_pairs = H // 2

    @pl.when(c == 0)
    def _():
        cn_scr[...] = jnp.zeros(cn_scr.shape, F32)
        m_scr[...] = jnp.zeros(m_scr.shape, F32)

    row = lax.broadcasted_iota(I32, (L, L), 0)
    col = lax.broadcasted_iota(I32, (L, L), 1)
    tri = (col <= row).astype(BF16)
    row_l = lax.broadcasted_iota(I32, (L, LANES), 0)
    col_l = lax.broadcasted_iota(I32, (L, LANES), 1)
    incl2 = (col_l % HEAD_DIM) <= row_l
    r2 = lax.broadcasted_iota(I32, (LANES, LANES), 0)
    c2 = lax.broadcasted_iota(I32, (LANES, LANES), 1)
    same_head = (r2 // HEAD_DIM) == (c2 // HEAD_DIM)
    same_head2 = jnp.concatenate([same_head, same_head], axis=1)

    units = [(ci, p) for ci in range(n_chunks) for p in range(n_pairs)]
    ig_rep, g_rep, x_t = [], [], []
    for ci in range(n_chunks):
        gt = g_ref[pl.ds(ci * L, L), :]
        rep = _dot_lhs2(gt, expand_ref[...])
        ig_rep.append(rep[:, 0:MLSTM_WIDTH])
        g_rep.append(_dot_rhs2(tri, rep[:, MLSTM_WIDTH:]))
        x_t.append(gt.T[0:H, :] - _dot_rhs2(tri, gt).T[H:2 * H, :])

    def part(ref, u):
        ci, p = u
        return ref[pl.ds(ci * L, L), p * LANES:(p + 1) * LANES]

    def pair(x, p):
        return x[:, p * LANES:(p + 1) * LANES]

    def block_diag(x):
        return jnp.where(same_head, jnp.concatenate([x, x], axis=0), 0.0)

    ig_c = [pair(ig_rep[ci], p) for ci, p in units]
    g_c = [pair(g_rep[ci], p) for ci, p in units]
    g_last = [g[L - 1:L, :] for g in g_c]
    x_row = [jnp.concatenate([x_t[ci][2 * p:2 * p + 1, :], x_t[ci][2 * p + 1:2 * p + 2, :]], axis=1)
             for ci, p in units]
    d_log = [jnp.where(incl2, g + xr, -jnp.inf) for g, xr in zip(g_c, x_row)]
    x_run = [ig - g for ig, g in zip(ig_c, g_c)]
    shift = 1
    while shift < L:
        x_run = [jnp.maximum(x, jnp.where(row_l >= shift, pltpu.roll(x, shift, 0), -jnp.inf)) for x in x_run]
        shift *= 2
    d_max = [g + x for g, x in zip(g_c, x_run)]
    qk = [_dot_nt(part(q_ref, u), block_diag(part(k_ref, u))) for u in units]
    e = [gl - g + ig for gl, g, ig in zip(g_last, g_c, ig_c)]
    m_loc = [jnp.max(x, axis=0, keepdims=True) for x in e]
    wk = [part(k_ref, u) * jnp.exp(x - m) for u, x, m in zip(units, e, m_loc)]
    ones = jnp.ones((L, LANES), F32)
    kvn_loc = [jnp.where(same_head2, _dot_tn(w, jnp.concatenate([part(v_ref, u), ones], axis=1)), 0.0)
               for w, u in zip(wk, units)]

    cn_st = [cn_scr[p] for p in range(n_pairs)]
    m_st = [m_scr[p:p + 1, :] for p in range(n_pairs)]
    cn_prev, m_prev = [], []
    for i, (ci, p) in enumerate(units):
        cn_prev.append(cn_st[p])
        m_prev.append(m_st[p])
        m_new = jnp.maximum(g_last[i] + m_st[p], m_loc[i])
        a_old = jnp.exp(g_last[i] + m_st[p] - m_new)
        a_new = jnp.exp(m_loc[i] - m_new)
        cn_st[p] = (jnp.concatenate([a_old, a_old], axis=1) * cn_st[p]
                    + jnp.concatenate([a_new, a_new], axis=1) * kvn_loc[i])
        m_st[p] = m_new
    for p in range(n_pairs):
        cn_scr[p] = cn_st[p]
        m_scr[p:p + 1, :] = m_st[p]

    inter_log = [g + m for g, m in zip(g_c, m_prev)]
    m_j = [jnp.maximum(dm, il) for dm, il in zip(d_max, inter_log)]
    w_intra = [jnp.exp(d - m) * s for d, m, s in zip(d_log, m_j, qk)]
    w_inter = [jnp.exp(il - m) for il, m in zip(inter_log, m_j)]
    ones_bd = same_head.astype(F32)
    intra = [_dot(w, jnp.concatenate([block_diag(part(v_ref, u)), ones_bd], axis=1))
             for w, u in zip(w_intra, units)]
    inter = [_dot(part(q_ref, u), cn) for u, cn in zip(units, cn_prev)]
    outs = [(ia[:, 0:LANES] + wi * ie[:, 0:LANES])
            / jnp.maximum(jnp.abs(ia[:, LANES:] + wi * ie[:, LANES:]), jnp.exp(-m))
            for ia, wi, ie, m in zip(intra, w_inter, inter, m_j)]
    for ci in range(n_chunks):
        h_ref[pl.ds(ci * L, L), :] = jnp.concatenate(outs[ci * n_pairs:(ci + 1) * n_pairs], axis=1)


def _gate_expand_matrix():
    lane = jnp.arange(LANES)[:, None]
    col = jnp.arange(2 * MLSTM_WIDTH)[None, :]
    src = jnp.where(col < MLSTM_WIDTH, col // HEAD_DIM, MLSTM_HEADS + (col - MLSTM_WIDTH) // HEAD_DIM)
    return (lane == src).astype(BF16)


def _mlstm_call(q, k, v, g):
    B, S, W = q.shape
    ts = SEQ_ROWS
    spec = pl.BlockSpec((None, ts, W), lambda bi, c: (bi, c, 0))
    gspec = pl.BlockSpec((None, ts, LANES), lambda bi, c: (bi, c, 0))
    espec = pl.BlockSpec((LANES, 2 * W), lambda bi, c: (0, 0))
    return pl.pallas_call(
        _mlstm_kernel, grid=(B, S // ts), in_specs=[spec, spec, spec, gspec, espec], out_specs=spec,
        out_shape=jax.ShapeDtypeStruct((B, S, W), F32),
        scratch_shapes=[pltpu.VMEM((MLSTM_HEADS // 2, LANES, 2 * LANES), F32),
                        pltpu.VMEM((8, LANES), F32)],
        compiler_params=pltpu.CompilerParams(dimension_semantics=("arbitrary", "arbitrary"),
                                             vmem_limit_bytes=VMEM_LIMIT),
        name="mlstm",
    )(q, k, v, g, _gate_expand_matrix())


def _layer_norm(z, g, b):
    mu = jnp.mean(z, axis=-1, keepdims=True)
    d = z - mu
    var = jnp.mean(d * d, axis=-1, keepdims=True)
    return d * lax.rsqrt(var + LN_EPS) * g + b


def _post_kernel(h_ref, ypool_ref, yr_ref, bonus_ref, g_ref, hm_ref, mo_ref,
                 lnxg_ref, lnxb_ref, ng_ref, wout_ref, ln1g_ref, ln1b_ref, rwt_ref, rb_ref, ones_ref,
                 h1_ref, gate_ref, route_ref, cnt_ref):
    ts = PREP_ROWS
    tiles = range(h_ref.shape[0] // ts)
    rows = lambda ref, i: ref[pl.ds(i * ts, ts), :]
    ones_bd = ones_ref[...]
    nrm_r = [_head_norm(rows(yr_ref, i), ones_bd, RWKV_GN_EPS) for i in tiles]
    nrm_m = [_head_norm(rows(hm_ref, i), ones_bd, LN_EPS) for i in tiles]
    y_rwkv = [(n * lnxg_ref[...] + lnxb_ref[...] + rows(bonus_ref, i)) * rows(g_ref, i) for i, n in zip(tiles, nrm_r)]
    y_ml = [rows(mo_ref, i) * (n * ng_ref[...]) for i, n in zip(tiles, nrm_m)]
    mix = [_dot(rows(ypool_ref, i), wout_ref[0:POOL_WIDTH, :])
           + _dot(y_rwkv[i], wout_ref[POOL_WIDTH:POOL_WIDTH + RWKV_WIDTH, :])
           + _dot(y_ml[i], wout_ref[POOL_WIDTH + RWKV_WIDTH:, :]) for i in tiles]
    h1 = [_layer_norm(DEEPNORM_ALPHA * rows(h_ref, i) + mix[i], ln1g_ref[...], ln1b_ref[...]) for i in tiles]
    for i in tiles:
        h1_ref[pl.ds(i * ts, ts), :] = h1[i]

    wh, wl = _split(rwt_ref[...])
    nt = lambda a, b: lax.dot_general(a, b, (((1,), (1,)), ((), ())), preferred_element_type=F32)
    parts = [_split(x) for x in h1]
    vals = [nt(wh, hh) + nt(wh, hl) + nt(wl, hh) + rb_ref[...] for hh, hl in parts]
    eidx = lax.broadcasted_iota(I32, (N_EXPERTS, ts), 0)
    tops, hots = [[] for _ in tiles], [[] for _ in tiles]
    for _ in range(TOP_K):
        mx = [jnp.max(v, axis=0, keepdims=True) for v in vals]
        idx = [jnp.min(jnp.where(v == m, eidx, N_EXPERTS), axis=0, keepdims=True) for v, m in zip(vals, mx)]
        hot = [eidx == ix for ix in idx]
        vals = [jnp.where(h, -jnp.inf, v) for h, v in zip(hot, vals)]
        for i in tiles:
            tops[i].append(mx[i])
            hots[i].append(hot[i])
    for i in tiles:
        exps = [jnp.exp(t - tops[i][0]) for t in tops[i]]
        denom = exps[0] + exps[1] + exps[2] + exps[3]
        gate_ref[:, pl.ds(i * ts, ts)] = jnp.concatenate([e / denom for e in exps]
                                                         + [jnp.zeros((8 - TOP_K, ts), F32)], axis=0)

    hot_f = [(h[0] | h[1] | h[2] | h[3]).astype(F32) for h in hots]
    r_i = lax.broadcasted_iota(I32, (ts, ts), 0)
    c_i = lax.broadcasted_iota(I32, (ts, ts), 1)
    before = (r_i < c_i).astype(BF16)
    within = [jnp.dot(hf.astype(BF16), before, preferred_element_type=F32) for hf in hot_f]
    n_e = [jnp.broadcast_to(jnp.sum(hf, axis=1, keepdims=True), (N_EXPERTS, LANES)) for hf in hot_f]
    run_len = [jnp.floor((n + (RUN_PIECE - 1)) * (1.0 / RUN_PIECE)) * RUN_PIECE for n in n_e]
    e_r = lax.broadcasted_iota(I32, (N_EXPERTS, N_EXPERTS), 0)
    e_c = lax.broadcasted_iota(I32, (N_EXPERTS, N_EXPERTS), 1)
    lower = (e_c < e_r).astype(BF16)
    run_start = [_dot_rhs2(lower, rl)[:, 0:1] for rl in run_len]
    for i in tiles:
        slots = [jnp.sum(jnp.where(hot, run_start[i] + within[i], 0.0), axis=0, keepdims=True) for hot in hots[i]]
        route_ref[:, pl.ds(i * ts, ts)] = jnp.concatenate([p.astype(I32) for p in slots]
                                                          + [jnp.zeros((8 - TOP_K, ts), I32)], axis=0)
        cnt_ref[pl.ds(i * N_EXPERTS, N_EXPERTS), :] = n_e[i]


def _post_call(h, ypool, yr, bonus, g, hm, mo, lw):
    T = h.shape[0]
    ts = POST_TILES * PREP_ROWS
    row = lambda c: pl.BlockSpec((ts, c), lambda i: (i, 0))
    colb = lambda r: pl.BlockSpec((r, ts), lambda i: (0, i))
    full = lambda a: pl.BlockSpec(a.shape, lambda i: (0,) * a.ndim)
    params = [lw['lnx_g'], lw['lnx_b'], lw['norm_g'], lw['w_out'], lw['ln1_g'], lw['ln1_b'],
              lw['router_wt'], lw['router_b'], lw['ones_bd']]
    in_specs = ([row(D_MODEL), row(POOL_WIDTH)] + [row(RWKV_WIDTH)] * 5 + [full(a) for a in params])
    out_shape = [jax.ShapeDtypeStruct((T, D_MODEL), F32), jax.ShapeDtypeStruct((8, T), F32),
                 jax.ShapeDtypeStruct((8, T), I32),
                 jax.ShapeDtypeStruct((T // PREP_ROWS * N_EXPERTS, LANES), F32)]
    out_specs = [row(D_MODEL), colb(8), colb(8), pl.BlockSpec((POST_TILES * N_EXPERTS, LANES), lambda i: (i, 0))]
    return pl.pallas_call(
        _post_kernel, grid=(T // ts,), in_specs=in_specs, out_specs=out_specs, out_shape=out_shape,
        compiler_params=pltpu.CompilerParams(dimension_semantics=("arbitrary",),
                                             vmem_limit_bytes=VMEM_LIMIT),
        name="post",
    )(h, ypool, yr, bonus, g, hm, mo, *params)


def _rows_copy(src_ref, src_row, dst_ref, dst_row, n_rows, sem):
    src_row, dst_row = [r if isinstance(r, int) else pl.multiple_of(r, SUBLANES) for r in (src_row, dst_row)]
    return pltpu.make_async_copy(src_ref.at[pl.ds(src_row, n_rows), :], dst_ref.at[pl.ds(dst_row, n_rows), :], sem)


def _stage_rows(ts):
    n = TOP_K * ts + N_EXPERTS * (RUN_PIECE - 1)
    return -(-n // SUBLANES) * SUBLANES


def _for_each_piece(n_pieces, body):
    lax.fori_loop(0, n_pieces, lambda j, c: (body(j), c)[1], 0)


def _run_copies(tab_ref, src_of, dst_of, sem, wait):
    for e in range(N_EXPERTS):
        n_rows = pl.multiple_of(tab_ref[0, e], SUBLANES)

        @pl.when(n_rows > 0)
        def _(e=e, n_rows=n_rows):
            src_ref, src_row = src_of(e)
            dst_ref, dst_row = dst_of(e)
            cp = _rows_copy(src_ref, src_row, dst_ref, dst_row, n_rows, sem)
            cp.wait() if wait else cp.start()


def _dispatch_kernel(tab_ref, prev_tab_ref, fill_ref, route_ref, gate_ref, h_ref, buf_ref,
                     sorted_scr, zero_scr, sems):
    step = pl.program_id(0)
    last = pl.num_programs(0) - 1
    slot = step % 2
    ts = h_ref.shape[0]
    n_sorted = sorted_scr.shape[1]

    @pl.when(step == 0)
    def _():
        zero_scr[...] = jnp.zeros(zero_scr.shape, F32)

        def fill(wait):
            for e in range(N_EXPERTS):
                def piece(j, e=e):
                    cp = _rows_copy(zero_scr, 0, buf_ref, fill_ref[0, e] + j * FILL_PIECE, FILL_PIECE, sems.at[0])
                    cp.wait() if wait else cp.start()
                _for_each_piece(fill_ref[0, N_EXPERTS + e], piece)

            def tail(j):
                cp = _rows_copy(zero_scr, 0, buf_ref, fill_ref[0, 2 * N_EXPERTS] + j * EXPERT_ROWS,
                                EXPERT_ROWS, sems.at[0])
                cp.wait() if wait else cp.start()
            _for_each_piece(fill_ref[0, 2 * N_EXPERTS + 1], tail)

        fill(False)
        fill(True)

    pos = lax.broadcasted_iota(I32, (n_sorted, ts), 0)
    hot = None
    weight = None
    for kslot in range(TOP_K):
        eq = pos == route_ref[kslot:kslot + 1, :]
        w = jnp.where(eq, gate_ref[kslot:kslot + 1, :], 0.0)
        hot = eq if hot is None else (hot | eq)
        weight = w if weight is None else weight + w
    sorted_scr[slot, :, 0:D_MODEL] = jnp.dot(hot.astype(BF16), h_ref[...].astype(BF16),
                                             preferred_element_type=F32)
    sorted_scr[slot, :, D_MODEL:BUF_WIDTH] = _dot_lhs2(weight, jnp.ones((ts, LANES), BF16))

    def copies(tab, which, wait):
        _run_copies(tab, lambda e: (sorted_scr.at[which], tab[0, N_EXPERTS + e]),
                    lambda e: (buf_ref, tab[0, 2 * N_EXPERTS + e]), sems.at[which], wait)

    @pl.when(step > 0)
    def _():
        copies(prev_tab_ref, 1 - slot, True)

    copies(tab_ref, slot, False)

    @pl.when(step == last)
    def _():
        copies(tab_ref, slot, True)


def _dispatch_call(tab_d, fill_tab, route8, gate8, h1, n_rows):
    T = h1.shape[0]
    ts = PREP_ROWS
    smem_tile = pl.BlockSpec((None, 1, LANES), lambda i: (i, 0, 0), memory_space=pltpu.SMEM)
    smem_prev = pl.BlockSpec((None, 1, LANES), lambda i: (jnp.maximum(i - 1, 0), 0, 0), memory_space=pltpu.SMEM)
    return pl.pallas_call(
        _dispatch_kernel, grid=(T // ts,),
        in_specs=[smem_tile, smem_prev,
                  pl.BlockSpec((1, LANES), lambda i: (0, 0), memory_space=pltpu.SMEM),
                  pl.BlockSpec((8, ts), lambda i: (0, i)),
                  pl.BlockSpec((8, ts), lambda i: (0, i)),
                  pl.BlockSpec((ts, D_MODEL), lambda i: (i, 0))],
        out_specs=pl.BlockSpec(memory_space=pl.ANY),
        out_shape=jax.ShapeDtypeStruct((n_rows, BUF_WIDTH), F32),
        scratch_shapes=[pltpu.VMEM((2, _stage_rows(ts), BUF_WIDTH), F32),
                        pltpu.VMEM((EXPERT_ROWS, BUF_WIDTH), F32),
                        pltpu.SemaphoreType.DMA((2,))],
        compiler_params=pltpu.CompilerParams(dimension_semantics=("arbitrary",),
                                             vmem_limit_bytes=VMEM_LIMIT),
        name="dispatch",
    )(tab_d, tab_d, fill_tab, route8, gate8, h1)


def _expert_kernel(be_ref, nb_ref, run_ref, next_ref, x_ref, wgu_hbm, bgu_ref, wdn_hbm, bdn_ref, o_ref,
                   wgu_f32, wdn_f32, wgu_scr, wdn_scr, sems, *, layer):
    j = pl.program_id(0)
    used = j < nb_ref[0]
    changed = jnp.logical_or(j == 0, be_ref[j] != be_ref[jnp.maximum(j - 1, 0)])

    def fetch(expert, slot):
        return (pltpu.make_async_copy(wgu_hbm.at[layer, expert], wgu_f32.at[slot], sems.at[0, slot]),
                pltpu.make_async_copy(wdn_hbm.at[layer, expert], wdn_f32.at[slot], sems.at[1, slot]))

    @pl.when(j == 0)
    def _():
        for cp in fetch(be_ref[0], 0):
            cp.start()

    @pl.when(jnp.logical_and(used, changed))
    def _():
        slot = run_ref[j] % 2
        for cp in fetch(be_ref[j], slot):
            cp.wait()
        wgu_scr[...] = wgu_f32[slot].astype(BF16)
        wdn_scr[...] = wdn_f32[slot].astype(BF16)

        @pl.when(next_ref[j] >= 0)
        def _():
            for cp in fetch(next_ref[j], 1 - slot):
                cp.start()

    @pl.when(used)
    def _():
        gu = jnp.dot(x_ref[:, 0:D_MODEL].astype(BF16), wgu_scr[...], preferred_element_type=F32) + bgu_ref[...]
        glu = jnp.minimum(gu[:, 0:D_FF], SWIGLU_LIMIT)
        lin = jnp.clip(gu[:, D_FF:], -SWIGLU_LIMIT, SWIGLU_LIMIT)
        act = glu * _sigmoid(SWIGLU_ALPHA * glu) * (lin + 1.0)
        y = jnp.dot(act.astype(BF16), wdn_scr[...], preferred_element_type=F32) + bdn_ref[...]
        gate = x_ref[:, D_MODEL:BUF_WIDTH]
        o_ref[...] = y * jnp.concatenate([gate] * (D_MODEL // LANES), axis=1)

    @pl.when(jnp.logical_not(used))
    def _():
        o_ref[...] = jnp.zeros(o_ref.shape, F32)


def _expert_call(block_e, n_used, buf, lw):
    n_rows = buf.shape[0]
    rows = EXPERT_ROWS
    n_blocks = n_rows // rows
    j = jnp.arange(n_blocks, dtype=I32)
    first = (j < n_used[0]) & ((j == 0) | (block_e != jnp.roll(block_e, 1)))
    run_idx = jnp.cumsum(first.astype(I32)) - 1
    later_first = lax.cummin(jnp.where(first, j, n_blocks)[::-1])[::-1]
    next_start = jnp.concatenate([later_first[1:], jnp.full((1,), n_blocks, I32)])
    next_e = jnp.where(next_start < n_blocks, block_e[jnp.minimum(next_start, n_blocks - 1)], -1).astype(I32)

    def blk(j, be, nb, run, nxt):
        return jnp.minimum(j, nb[0] - 1)

    layer = lw['layer']
    grid_spec = pltpu.PrefetchScalarGridSpec(
        num_scalar_prefetch=4, grid=(n_blocks,),
        in_specs=[pl.BlockSpec((rows, BUF_WIDTH), lambda j, *s: (blk(j, *s), 0)),
                  pl.BlockSpec(memory_space=pl.ANY),
                  pl.BlockSpec((None, None, 1, 2 * D_FF), lambda j, be, *s: (layer, be[j], 0, 0)),
                  pl.BlockSpec(memory_space=pl.ANY),
                  pl.BlockSpec((None, None, 1, D_MODEL), lambda j, be, *s: (layer, be[j], 0, 0))],
        out_specs=pl.BlockSpec((rows, D_MODEL), lambda j, *s: (j, 0)),
        scratch_shapes=[pltpu.VMEM((2, D_MODEL, 2 * D_FF), F32), pltpu.VMEM((2, D_FF, D_MODEL), F32),
                        pltpu.VMEM((D_MODEL, 2 * D_FF), BF16), pltpu.VMEM((D_FF, D_MODEL), BF16),
                        pltpu.SemaphoreType.DMA((2, 2))])
    return pl.pallas_call(
        functools.partial(_expert_kernel, layer=layer), grid_spec=grid_spec,
        out_shape=jax.ShapeDtypeStruct((n_rows, D_MODEL), F32),
        compiler_params=pltpu.CompilerParams(dimension_semantics=("arbitrary",),
                                             vmem_limit_bytes=56 * 1024 * 1024),
        name="expert",
    )(block_e, n_used, run_idx, next_e, buf, lw['w_gate_up'], lw['b_gate_up'], lw['w_down'], lw['b_down'])


def _combine_kernel(tab_ref, next_tab_ref, route_ref, h_ref, out_hbm_ref, ln2g_ref, ln2b_ref, o_ref,
                    stage_scr, sems):
    step = pl.program_id(0)
    last = pl.num_programs(0) - 1
    slot = step % 2
    ts = h_ref.shape[0]
    n_stage = stage_scr.shape[1]

    def copies(tab, which, wait):
        _run_copies(tab, lambda e: (out_hbm_ref, tab[0, 2 * N_EXPERTS + e]),
                    lambda e: (stage_scr.at[which], tab[0, N_EXPERTS + e]), sems.at[which], wait)

    @pl.when(step == 0)
    def _():
        stage_scr[...] = jnp.zeros(stage_scr.shape, F32)
        copies(tab_ref, 0, False)

    @pl.when(step < last)
    def _():
        copies(next_tab_ref, 1 - slot, False)

    slot_f = route_ref[...].astype(F32)
    slot_cols = jnp.concatenate([slot_f, jnp.zeros((LANES - 8, ts), F32)], axis=0).T
    lane = lax.broadcasted_iota(I32, (ts, n_stage), 1).astype(F32)
    sel = None
    for kslot in range(TOP_K):
        eq = lane == slot_cols[:, kslot:kslot + 1]
        sel = eq if sel is None else (sel | eq)
    copies(tab_ref, slot, True)
    y = jnp.dot(sel.astype(BF16), stage_scr[slot].astype(BF16), preferred_element_type=F32)
    o_ref[...] = _layer_norm(DEEPNORM_ALPHA * h_ref[...] + y, ln2g_ref[...], ln2b_ref[...])


def _combine_call(tab_c, route8, h1, out_rows, lw):
    T = h1.shape[0]
    ts = PREP_ROWS
    n_tiles = T // ts
    full = lambda a: pl.BlockSpec(a.shape, lambda i: (0,) * a.ndim)
    n_stage = _stage_rows(ts)
    return pl.pallas_call(
        _combine_kernel, grid=(n_tiles,),
        in_specs=[pl.BlockSpec((None, 1, LANES), lambda i: (i, 0, 0), memory_space=pltpu.SMEM),
                  pl.BlockSpec((None, 1, LANES), lambda i: (jnp.minimum(i + 1, n_tiles - 1), 0, 0),
                               memory_space=pltpu.SMEM),
                  pl.BlockSpec((8, ts), lambda i: (0, i)),
                  pl.BlockSpec((ts, D_MODEL), lambda i: (i, 0)),
                  pl.BlockSpec(memory_space=pl.ANY),
                  full(lw['ln2_g']), full(lw['ln2_b'])],
        out_specs=pl.BlockSpec((ts, D_MODEL), lambda i: (i, 0)),
        out_shape=jax.ShapeDtypeStruct((T, D_MODEL), F32),
        scratch_shapes=[pltpu.VMEM((2, n_stage, D_MODEL), F32), pltpu.SemaphoreType.DMA((2,))],
        compiler_params=pltpu.CompilerParams(dimension_semantics=("arbitrary",),
                                             vmem_limit_bytes=VMEM_LIMIT),
        name="combine",
    )(tab_c, tab_c, route8, h1, out_rows, lw['ln2_g'], lw['ln2_b'])


def _block_diag_ones(width):
    hid = jnp.arange(width) // HEAD_DIM
    return (hid[:, None] == hid[None, :]).astype(BF16)


def _layer_params(l, w_in, pool_w, pool_scale, rwkv_mu, rwkv_w0, rwkv_w2, rwkv_a0, rwkv_a2, rwkv_g2,
                  rwkv_kk_scale, rwkv_ka, rwkv_rk, rwkv_lnx_g, rwkv_lnx_b, rwkv_v0, rwkv_v1, rwkv_v2,
                  mlstm_conv_w, mlstm_conv_b, mlstm_b_i, mlstm_b_f, mlstm_norm_g, w_out, ln1_g, ln1_b,
                  router_w, router_b, w_gate_up, b_gate_up, w_down, b_down, ln2_g, ln2_b):
    row = lambda a: a.reshape(1, -1).astype(F32)
    pad_cols = D_IN_PAD - D_IN
    if l > 0:
        extra = jnp.concatenate([rwkv_v1[l - 1], jnp.zeros((D_MODEL, pad_cols - VRES_LORA), F32)], axis=1)
        v0 = row(rwkv_v0[l - 1])
        v2 = jnp.zeros((LANES, RWKV_WIDTH), F32).at[VRES_OFF:VRES_OFF + VRES_LORA].set(rwkv_v2[l - 1])
    else:
        extra = jnp.zeros((D_MODEL, pad_cols), F32)
        v0 = jnp.zeros((1, RWKV_WIDTH), F32)
        v2 = jnp.zeros((LANES, RWKV_WIDTH), F32)
    pw = jnp.zeros((POOL_WIDTH, POOL_WIDTH), F32)
    for gi in range(len(POOL_WINDOWS)):
        sl = slice(gi * POOL_GROUP, (gi + 1) * POOL_GROUP)
        pw = pw.at[sl, sl].set(pool_w[l, gi])
    zero_lora = jnp.zeros((DECAY_LORA, RWKV_WIDTH), F32)
    gate_bias = jnp.zeros((1, LANES), F32).at[0, 0:MLSTM_HEADS].set(mlstm_b_i[l])
    gate_bias = gate_bias.at[0, MLSTM_HEADS:2 * MLSTM_HEADS].set(mlstm_b_f[l])
    return {
        'w_in': jnp.concatenate([w_in[l], extra], axis=1).astype(BF16),
        'pool_w': pw.astype(BF16), 'pool_scale': row(pool_scale[l]), 'mu': row(rwkv_mu[l]),
        'w0': row(rwkv_w0[l]), 'w2': jnp.concatenate([rwkv_w2[l], zero_lora], axis=0).astype(BF16),
        'a0': row(rwkv_a0[l]), 'a2': jnp.concatenate([zero_lora, rwkv_a2[l]], axis=0).astype(BF16),
        'g2': rwkv_g2[l].astype(BF16), 'kk_scale': row(rwkv_kk_scale[l]), 'ka': row(rwkv_ka[l]),
        'rk': row(rwkv_rk[l]), 'v0': v0, 'v2': v2.astype(BF16),
        'conv_w': mlstm_conv_w[l], 'conv_b': row(mlstm_conv_b[l]), 'gate_bias': gate_bias,
        'ones_bd': _block_diag_ones(RWKV_WIDTH),
        'lnx_g': row(rwkv_lnx_g[l]), 'lnx_b': row(rwkv_lnx_b[l]), 'norm_g': row(mlstm_norm_g[l]),
        'w_out': w_out[l].astype(BF16), 'ln1_g': row(ln1_g[l]), 'ln1_b': row(ln1_b[l]),
        'router_wt': router_w[l].T, 'router_b': router_b[l].reshape(N_EXPERTS, 1),
        'layer': l, 'w_gate_up': w_gate_up, 'b_gate_up': b_gate_up.reshape(-1, N_EXPERTS, 1, 2 * D_FF),
        'w_down': w_down, 'b_down': b_down.reshape(-1, N_EXPERTS, 1, D_MODEL),
        'ln2_g': row(ln2_g[l]), 'ln2_b': row(ln2_b[l]),
    }


def _layer(h, v_first, lw, *, has_vres):
    B, S, _ = h.shape
    T = B * S
    outs = _prep_call(h, lw, v_first, has_vres=has_vres)
    ypool, r, ld, k, v, kk, b, g, bonus, mq, mk, mv, mo, mg = outs
    yr = _rwkv_call(r, ld, k, v, kk, b)
    hm = _mlstm_call(mq, mk, mv, mg)
    flat = lambda a: a.reshape(T, a.shape[-1])
    h_flat = flat(h)
    h1, gate8, route8, cnt = _post_call(h_flat, flat(ypool), flat(yr), flat(bonus), flat(g), flat(hm),
                                        flat(mo), lw)
    n_tiles = T // PREP_ROWS
    cnt = cnt.reshape(n_tiles, N_EXPERTS, LANES)[:, :, 0].astype(I32)
    rows = EXPERT_ROWS
    run_len = (cnt + RUN_PIECE - 1) // RUN_PIECE * RUN_PIECE
    region = jnp.sum(run_len, axis=0)
    padded = (region + rows - 1) // rows * rows
    pad_end = jnp.cumsum(padded)
    pad_start = pad_end - padded
    n_blocks = -(-(T * TOP_K + n_tiles * N_EXPERTS * (RUN_PIECE - 1)) // rows) + N_EXPERTS
    n_used = (pad_end[-1] // rows).astype(I32).reshape(1)
    starts = jnp.minimum(jnp.arange(n_blocks, dtype=I32), n_used[0] - 1) * rows
    block_e = jnp.sum((pad_end[None, :] <= starts[:, None]).astype(I32), axis=1)
    run_src = jnp.cumsum(run_len, axis=1) - run_len
    run_dst = pad_start[None, :] + jnp.cumsum(run_len, axis=0) - run_len
    run_tab = jnp.concatenate([run_len, run_src, run_dst,
                               jnp.zeros((n_tiles, LANES - 3 * N_EXPERTS), I32)], axis=1).reshape(n_tiles, 1, LANES)
    fill_start = (pad_start + region) // FILL_PIECE * FILL_PIECE
    fill_tab = jnp.concatenate([fill_start, (pad_end - fill_start) // FILL_PIECE,
                                pad_end[-1:], n_blocks - n_used,
                                jnp.zeros((LANES - 2 * N_EXPERTS - 2,), I32)]).reshape(1, LANES)
    buf = _dispatch_call(run_tab, fill_tab, route8, gate8, h1, n_blocks * rows)
    out_rows = _expert_call(block_e, n_used, buf, lw)
    h2 = _combine_call(run_tab, route8, h1, out_rows, lw)
    return h2.reshape(B, S, D_MODEL), v


def kernel(x, w_in, pool_w, pool_scale, rwkv_mu, rwkv_w0, rwkv_w2, rwkv_a0, rwkv_a2, rwkv_g2, rwkv_kk_scale, rwkv_ka, rwkv_rk, rwkv_lnx_g, rwkv_lnx_b, rwkv_v0, rwkv_v1, rwkv_v2, mlstm_conv_w, mlstm_conv_b, mlstm_b_i, mlstm_b_f, mlstm_norm_g, w_out, ln1_g, ln1_b, router_w, router_b, w_gate_up, b_gate_up, w_down, b_down, ln2_g, ln2_b):
    weights = (w_in, pool_w, pool_scale, rwkv_mu, rwkv_w0, rwkv_w2, rwkv_a0, rwkv_a2, rwkv_g2, rwkv_kk_scale,
               rwkv_ka, rwkv_rk, rwkv_lnx_g, rwkv_lnx_b, rwkv_v0, rwkv_v1, rwkv_v2, mlstm_conv_w, mlstm_conv_b,
               mlstm_b_i, mlstm_b_f, mlstm_norm_g, w_out, ln1_g, ln1_b, router_w, router_b, w_gate_up,
               b_gate_up, w_down, b_down, ln2_g, ln2_b)
    h = x
    v_first = jnp.zeros(x.shape[:2] + (RWKV_WIDTH,), F32)
    for l in range(w_in.shape[0]):
        lw = _layer_params(l, *weights)
        h, v_l = _layer(h, v_first, lw, has_vres=l > 0)
        if l == 0:
            v_first = v_l
    return h
```

```python
import functools

import jax
import jax.numpy as jnp
from jax import lax
from jax.experimental import pallas as pl
from jax.experimental.pallas import tpu as pltpu

F32 = jnp.float32
BF16 = jnp.bfloat16
I32 = jnp.int32

D_MODEL = 1024
HEAD_DIM = 64
POOL_WINDOWS = (2, 4, 8, 16)
POOL_WIDTH = 256
POOL_GROUP = 64
RWKV_WIDTH = 384
RWKV_HEADS = 6
DECAY_LORA = 64
ICLR_LORA = 64
GATE_LORA = 128
VRES_LORA = 32
RWKV_GN_EPS = 64e-5
RWKV_COLS = 3 * RWKV_WIDTH + DECAY_LORA + ICLR_LORA + GATE_LORA
MLSTM_WIDTH = 384
MLSTM_HEADS = 6
MLSTM_CONV = 4
MLSTM_COLS = 4 * MLSTM_WIDTH + 2 * MLSTM_HEADS
D_IN = POOL_WIDTH + RWKV_COLS + MLSTM_COLS
N_EXPERTS = 32
TOP_K = 4
D_FF = D_MODEL
SWIGLU_LIMIT = 7.0
SWIGLU_ALPHA = 1.702
LN_EPS = 1e-5
DEPTH = 2
DEEPNORM_ALPHA = (2 * DEPTH) ** 0.25

LANES = 128
D_IN_PAD = 3328
RWKV_OFF = POOL_WIDTH
MLSTM_OFF = POOL_WIDTH + RWKV_COLS
GATE_OFF = MLSTM_OFF + 4 * MLSTM_WIDTH
VRES_OFF = 2 * MLSTM_HEADS
HALO = 16
CHUNK = 64

PROJ_ROWS = 512
PREP_ROWS = 256
POST_TILES = 4
MOVE_TILES = 2
SEQ_ROWS = 512
EXPERT_ROWS = 512
BUF_WIDTH = D_MODEL + LANES
SUBLANES = 8
RUN_PIECE = SUBLANES
FILL_PIECE = 64
VMEM_LIMIT = 48 * 1024 * 1024


def _dot(a, b):
    return jnp.dot(a.astype(BF16), b.astype(BF16), preferred_element_type=F32)


def _dot_nt(a, b):
    return lax.dot_general(a.astype(BF16), b.astype(BF16), (((1,), (1,)), ((), ())),
                           preferred_element_type=F32)


def _dot_tn(a, b):
    return lax.dot_general(a.astype(BF16), b.astype(BF16), (((0,), (0,)), ((), ())),
                           preferred_element_type=F32)


def _split(x):
    hi = x.astype(BF16)
    lo = (x - hi.astype(F32)).astype(BF16)
    return hi, lo


def _dot_lhs2(a, b_bf16):
    hi, lo = _split(a)
    return (jnp.dot(hi, b_bf16, preferred_element_type=F32)
            + jnp.dot(lo, b_bf16, preferred_element_type=F32))


def _dot_rhs2(a_bf16, b):
    hi, lo = _split(b)
    return (jnp.dot(a_bf16, hi, preferred_element_type=F32)
            + jnp.dot(a_bf16, lo, preferred_element_type=F32))


def _sigmoid(x):
    return 1.0 / (1.0 + jnp.exp(-x))


def _softplus(x):
    return jnp.maximum(x, 0.0) + jnp.log(1.0 + jnp.exp(-jnp.abs(x)))


def _head_norm(y, ones_bd, eps):
    inv = 1.0 / HEAD_DIM
    mean = _dot(y, ones_bd) * inv
    d = y - mean
    var = _dot(d * d, ones_bd) * inv
    return d * lax.rsqrt(var + eps)


def _prep_kernel(x_ref, w_ref, poolw_ref, pscale_ref, mu_ref, w0_ref, w2_ref, a0_ref, a2_ref, g2_ref,
                 kks_ref, ka_ref, rk_ref, v0_ref, v2_ref, vfirst_ref, cw_ref, cb_ref, gbias_ref, ones_ref,
                 ypool_ref, r_ref, ld_ref, k_ref, v_ref, kk_ref, b_ref, g_ref, bonus_ref,
                 mq_ref, mk_ref, mv_ref, mo_ref, mg_ref,
                 p_scr, *, has_vres):
    i = pl.program_id(1)
    ts = x_ref.shape[0]

    @pl.when(i == 0)
    def _():
        p_scr[0:HALO, :] = jnp.zeros((HALO, D_IN_PAD), F32)

    @pl.when(i > 0)
    def _():
        p_scr[0:HALO, :] = p_scr[ts:ts + HALO, :]

    p_scr[HALO:HALO + ts, :] = jnp.dot(x_ref[...].astype(BF16), w_ref[...], preferred_element_type=F32)

    def rows(shift, c0, c1):
        return p_scr[HALO - shift:HALO - shift + ts, c0:c1]

    u = rows(0, 0, POOL_WIDTH)
    acc = u
    sums = {}
    for s in range(1, POOL_WINDOWS[-1]):
        acc = acc + rows(s, 0, POOL_WIDTH)
        if s + 1 in POOL_WINDOWS:
            sums[s + 1] = acc
    pos = (i * ts + lax.broadcasted_iota(I32, (ts, 1), 0) + 1).astype(F32)
    lane = lax.broadcasted_iota(I32, (ts, POOL_WIDTH), 1)
    d = None
    for gi, win in reversed(list(enumerate(POOL_WINDOWS))):
        dg = sums[win] / jnp.minimum(pos, float(win))
        d = dg if d is None else jnp.where(lane < (gi + 1) * POOL_GROUP, dg, d)
    d = d - u
    ypool_ref[...] = _dot(d, poolw_ref[...]) * pscale_ref[...]

    cur = rows(0, RWKV_OFF, RWKV_OFF + RWKV_COLS)
    prev = rows(1, RWKV_OFF, RWKV_OFF + RWKV_COLS)
    pf = cur + mu_ref[...] * (prev - cur)
    W = RWKV_WIDTH
    r = pf[:, 0:W]
    k = pf[:, W:2 * W]
    v = pf[:, 2 * W:3 * W]
    z = pf[:, 3 * W:3 * W + LANES]
    gd = pf[:, 3 * W + LANES:3 * W + 2 * LANES]
    w_log = -_softplus(-(w0_ref[...] + _dot(jnp.tanh(z), w2_ref[...]))) - 0.5
    ld_ref[...] = -jnp.exp(w_log)
    a = _sigmoid(a0_ref[...] + _dot(z, a2_ref[...]))
    g_ref[...] = _dot(_sigmoid(gd), g2_ref[...])
    gates = rows(0, GATE_OFF, GATE_OFF + LANES)
    if has_vres:
        v_gate = _sigmoid(v0_ref[...] + _dot(gates, v2_ref[...]))
        v = v + (vfirst_ref[...] - v) * v_gate
    ones_bd = ones_ref[...]
    kk = k * kks_ref[...]
    kk = kk / jnp.maximum(jnp.sqrt(_dot(kk * kk, ones_bd)), 1e-12)
    k = k * (1.0 + (a - 1.0) * ka_ref[...])
    r_ref[...] = r
    k_ref[...] = k
    v_ref[...] = v
    kk_ref[...] = kk
    b_ref[...] = kk * a
    bonus_ref[...] = _dot(r * k * rk_ref[...], ones_bd) * v

    qk = cb_ref[...] + rows(0, MLSTM_OFF, MLSTM_OFF + 2 * MLSTM_WIDTH) * cw_ref[MLSTM_CONV - 1:MLSTM_CONV, :]
    for tap in range(MLSTM_CONV - 1):
        shift = MLSTM_CONV - 1 - tap
        qk = qk + rows(shift, MLSTM_OFF, MLSTM_OFF + 2 * MLSTM_WIDTH) * cw_ref[tap:tap + 1, :]
    qk = qk * _sigmoid(qk)
    mq_ref[...] = qk[:, 0:MLSTM_WIDTH] * (HEAD_DIM ** -0.5)
    mk_ref[...] = qk[:, MLSTM_WIDTH:]
    mv_ref[...] = rows(0, MLSTM_OFF + 2 * MLSTM_WIDTH, MLSTM_OFF + 3 * MLSTM_WIDTH)
    mo_ref[...] = _sigmoid(rows(0, MLSTM_OFF + 3 * MLSTM_WIDTH, MLSTM_OFF + 4 * MLSTM_WIDTH))
    gb = gates + gbias_ref[...]
    glane = lax.broadcasted_iota(I32, (ts, LANES), 1)
    mg_ref[...] = jnp.where(glane < MLSTM_HEADS, gb, -_softplus(-gb))


def _prep_call(x, lw, vfirst, *, has_vres):
    B, S, _ = x.shape
    ts = PROJ_ROWS
    grid = (B, S // ts)
    row3 = lambda c: pl.BlockSpec((None, ts, c), lambda b, i: (b, i, 0))
    full = lambda a: pl.BlockSpec(a.shape, lambda b, i: (0,) * a.ndim)
    params = [lw['w_in'], lw['pool_w'], lw['pool_scale'], lw['mu'], lw['w0'], lw['w2'], lw['a0'], lw['a2'],
              lw['g2'], lw['kk_scale'], lw['ka'], lw['rk'], lw['v0'], lw['v2']]
    tail = [lw['conv_w'], lw['conv_b'], lw['gate_bias'], lw['ones_bd']]
    in_specs = ([row3(D_MODEL)] + [full(a) for a in params] + [row3(RWKV_WIDTH)] + [full(a) for a in tail])
    widths = [POOL_WIDTH] + [RWKV_WIDTH] * 8 + [MLSTM_WIDTH] * 4 + [LANES]
    out_shape = [jax.ShapeDtypeStruct((B, S, c), F32) for c in widths]
    out_specs = [row3(c) for c in widths]
    return pl.pallas_call(
        functools.partial(_prep_kernel, has_vres=has_vres),
        grid=grid, in_specs=in_specs, out_specs=out_specs, out_shape=out_shape,
        scratch_shapes=[pltpu.VMEM((HALO + ts, D_IN_PAD), F32)],
        compiler_params=pltpu.CompilerParams(dimension_semantics=("arbitrary", "arbitrary"),
                                             vmem_limit_bytes=VMEM_LIMIT),
        name="prep",
    )(x, *params, vfirst, *tail)


def _rwkv_kernel(r_ref, ld_ref, k_ref, v_ref, kk_ref, b_ref, y_ref, s_scr):
    c = pl.program_id(1)
    L = CHUNK
    n_chunks = r_ref.shape[0] // L

    @pl.when(c == 0)
    def _():
        s_scr[...] = jnp.zeros(s_scr.shape, F32)

    row = lax.broadcasted_iota(I32, (L, L), 0)
    col = lax.broadcasted_iota(I32, (L, L), 1)
    tri = (col <= row).astype(BF16)
    row_l = lax.broadcasted_iota(I32, (L, LANES), 0)
    col_l = lax.broadcasted_iota(I32, (L, LANES), 1) % HEAD_DIM
    strict = col_l < row_l
    incl = col_l <= row_l
    eye = (col_l == row_l).astype(F32)
    r2 = lax.broadcasted_iota(I32, (LANES, LANES), 0)
    c2 = lax.broadcasted_iota(I32, (LANES, LANES), 1)
    same_head = (r2 // HEAD_DIM) == (c2 // HEAD_DIM)
    eye2 = (r2 == c2).astype(F32)
    low_lanes = lax.broadcasted_iota(I32, (L, LANES), 1) < HEAD_DIM
    n_pairs = RWKV_HEADS // 2

    def block_diag(x):
        xb = x.astype(BF16)
        return jnp.where(same_head, jnp.concatenate([xb, xb], axis=0), jnp.zeros((), BF16))

    def diag_blocks(z):
        return jnp.where(low_lanes, z[0:L, :], z[L:2 * L, :])

    units = [(ci, p) for ci in range(n_chunks) for p in range(n_pairs)]
    per_chunk = []
    for ci in range(n_chunks):
        sl = pl.ds(ci * L, L)
        ld = ld_ref[sl, :]
        cin = _dot_rhs2(tri, ld)
        c_last = cin[L - 1:L, :]
        e_neg = jnp.exp(-cin)
        e_tail = jnp.exp(c_last - cin)
        kk = kk_ref[sl, :]
        bb = b_ref[sl, :]
        kx = k_ref[sl, :]
        per_chunk.append(dict(
            A=-kk * jnp.exp(cin - ld), R=r_ref[sl, :] * jnp.exp(cin), B=bb * e_neg, K=kx * e_neg,
            Bh=bb * e_tail, Kh=kx * e_tail, V=v_ref[sl, :], g_last=jnp.exp(c_last)))

    def part(name, u):
        ci, p = u
        return per_chunk[ci][name][:, p * LANES:(p + 1) * LANES]

    M = [_dot_nt(jnp.concatenate([part('A', u), part('R', u)], axis=0),
                 jnp.concatenate([block_diag(part('B', u)), block_diag(part('K', u))], axis=0)) for u in units]
    m_ab = [jnp.where(strict, m[0:L, 0:LANES], 0.0) for m in M]
    m_ak = [jnp.where(strict, m[0:L, LANES:], 0.0) for m in M]
    m_rb = [jnp.where(incl, m[L:, 0:LANES], 0.0) for m in M]
    m_rk = [jnp.where(incl, m[L:, LANES:], 0.0) for m in M]
    MVYK = [_dot(jnp.concatenate([ak, rk], axis=0), block_diag(part('V', u)))
            for ak, rk, u in zip(m_ak, m_rk, units)]
    MV = [x[0:L, :] for x in MVYK]
    YK = [x[L:, :] for x in MVYK]
    T = [eye + m for m in m_ab]
    pw = [_dot(m, block_diag(m)) for m in m_ab]
    for _ in range(4):
        both = [_dot(jnp.concatenate([t, p], axis=0), block_diag(p)) for t, p in zip(T, pw)]
        T = [t + x[0:L, :] for t, x in zip(T, both)]
        pw = [x[L:, :] for x in both]
    T = [t + _dot(t, block_diag(p)) for t, p in zip(T, pw)]
    WU = [_dot(t, jnp.concatenate([block_diag(part('A', u)), block_diag(mv)], axis=1))
          for t, u, mv in zip(T, units, MV)]
    GY = [_dot(m, jnp.concatenate([block_diag(wu[:, 0:LANES]), block_diag(wu[:, LANES:])], axis=1))
          for m, wu in zip(m_rb, WU)]
    G = [part('R', u) + gy[:, 0:LANES] for u, gy in zip(units, GY)]
    Y0 = [gy[:, LANES:] + yk for gy, yk in zip(GY, YK)]
    P = [jnp.where(same_head, _dot_tn(wu[:, 0:LANES], part('Bh', u)), 0.0) + eye2 * part('g_last', u)
         for wu, u in zip(WU, units)]
    Q = [diag_blocks(_dot_tn(jnp.concatenate([wu[:, LANES:], part('V', u)], axis=0),
                             jnp.concatenate([part('Bh', u), part('Kh', u)], axis=0)))
         for wu, u in zip(WU, units)]

    state = [s_scr[p] for p in range(n_pairs)]
    for ci in range(n_chunks):
        base = ci * n_pairs
        ys = [_dot_nt(G[base + p], block_diag(state[p])) + Y0[base + p] for p in range(n_pairs)]
        state = [_dot(state[p], P[base + p]) + Q[base + p] for p in range(n_pairs)]
        y_ref[pl.ds(ci * L, L), :] = jnp.concatenate(ys, axis=1)
    for p in range(n_pairs):
        s_scr[p] = state[p]


def _rwkv_call(r, ld, k, v, kk, b):
    B, S, W = r.shape
    ts = SEQ_ROWS
    spec = pl.BlockSpec((None, ts, W), lambda bi, c: (bi, c, 0))
    return pl.pallas_call(
        _rwkv_kernel, grid=(B, S // ts), in_specs=[spec] * 6, out_specs=spec,
        out_shape=jax.ShapeDtypeStruct((B, S, W), F32),
        scratch_shapes=[pltpu.VMEM((RWKV_HEADS // 2, HEAD_DIM, LANES), F32)],
        compiler_params=pltpu.CompilerParams(dimension_semantics=("arbitrary", "arbitrary"),
                                             vmem_limit_bytes=VMEM_LIMIT),
        name="rwkv",
    )(r, ld, k, v, kk, b)


def _mlstm_kernel(q_ref, k_ref, v_ref, g_ref, expand_ref, h_ref, cn_scr, m_scr):
    c = pl.program_id(1)
    L = CHUNK
    n_chunks = q_ref.shape[0] // L
    H = MLSTM_HEADS
    n_pairs = H // 2

    @pl.when(c == 0)
    def _():
        cn_scr[...] = jnp.zeros(cn_scr.shape, F32)
        m_scr[...] = jnp.zeros(m_scr.shape, F32)

    row = lax.broadcasted_iota(I32, (L, L), 0)
    col = lax.broadcasted_iota(I32, (L, L), 1)
    tri = (col <= row).astype(BF16)
    row_l = lax.broadcasted_iota(I32, (L, LANES), 0)
    col_l = lax.broadcasted_iota(I32, (L, LANES), 1)
    incl2 = (col_l % HEAD_DIM) <= row_l
    r2 = lax.broadcasted_iota(I32, (LANES, LANES), 0)
    c2 = lax.broadcasted_iota(I32, (LANES, LANES), 1)
    same_head = (r2 // HEAD_DIM) == (c2 // HEAD_DIM)
    same_head2 = jnp.concatenate([same_head, same_head], axis=1)

    units = [(ci, p) for ci in range(n_chunks) for p in range(n_pairs)]
    ig_rep, g_rep, x_t = [], [], []
    for ci in range(n_chunks):
        gt = g_ref[pl.ds(ci * L, L), :]
        rep = _dot_lhs2(gt, expand_ref[...])
        ig_rep.append(rep[:, 0:MLSTM_WIDTH])
        g_rep.append(_dot_rhs2(tri, rep[:, MLSTM_WIDTH:]))
        x_t.append(gt.T[0:H, :] - _dot_rhs2(tri, gt).T[H:2 * H, :])

    def part(ref, u):
        ci, p = u
        return ref[pl.ds(ci * L, L), p * LANES:(p + 1) * LANES]

    def pair(x, p):
        return x[:, p * LANES:(p + 1) * LANES]

    def block_diag(x):
        return jnp.where(same_head, jnp.concatenate([x, x], axis=0), 0.0)

    ig_c = [pair(ig_rep[ci], p) for ci, p in units]
    g_c = [pair(g_rep[ci], p) for ci, p in units]
    g_last = [g[L - 1:L, :] for g in g_c]
    x_row = [jnp.concatenate([x_t[ci][2 * p:2 * p + 1, :], x_t[ci][2 * p + 1:2 * p + 2, :]], axis=1)
             for ci, p in units]
    d_log = [jnp.where(incl2, g + xr, -jnp.inf) for g, xr in zip(g_c, x_row)]
    x_run = [ig - g for ig, g in zip(ig_c, g_c)]
    shift = 1
    while shift < L:
        x_run = [jnp.maximum(x, jnp.where(row_l >= shift, pltpu.roll(x, shift, 0), -jnp.inf)) for x in x_run]
        shift *= 2
    d_max = [g + x for g, x in zip(g_c, x_run)]
    qk = [_dot_nt(part(q_ref, u), block_diag(part(k_ref, u))) for u in units]
    e = [gl - g + ig for gl, g, ig in zip(g_last, g_c, ig_c)]
    m_loc = [jnp.max(x, axis=0, keepdims=True) for x in e]
    wk = [part(k_ref, u) * jnp.exp(x - m) for u, x, m in zip(units, e, m_loc)]
    ones = jnp.ones((L, LANES), F32)
    kvn_loc = [jnp.where(same_head2, _dot_tn(w, jnp.concatenate([part(v_ref, u), ones], axis=1)), 0.0)
               for w, u in zip(wk, units)]

    cn_st = [cn_scr[p] for p in range(n_pairs)]
    m_st = [m_scr[p:p + 1, :] for p in range(n_pairs)]
    cn_prev, m_prev = [], []
    for i, (ci, p) in enumerate(units):
        cn_prev.append(cn_st[p])
        m_prev.append(m_st[p])
        m_new = jnp.maximum(g_last[i] + m_st[p], m_loc[i])
        a_old = jnp.exp(g_last[i] + m_st[p] - m_new)
        a_new = jnp.exp(m_loc[i] - m_new)
        cn_st[p] = (jnp.concatenate([a_old, a_old], axis=1) * cn_st[p]
                    + jnp.concatenate([a_new, a_new], axis=1) * kvn_loc[i])
        m_st[p] = m_new
    for p in range(n_pairs):
        cn_scr[p] = cn_st[p]
        m_scr[p:p + 1, :] = m_st[p]

    inter_log = [g + m for g, m in zip(g_c, m_prev)]
    m_j = [jnp.maximum(dm, il) for dm, il in zip(d_max, inter_log)]
    w_intra = [jnp.exp(d - m) * s for d, m, s in zip(d_log, m_j, qk)]
    w_inter = [jnp.exp(il - m) for il, m in zip(inter_log, m_j)]
    ones_bd = same_head.astype(F32)
    intra = [_dot(w, jnp.concatenate([block_diag(part(v_ref, u)), ones_bd], axis=1))
             for w, u in zip(w_intra, units)]
    inter = [_dot(part(q_ref, u), cn) for u, cn in zip(units, cn_prev)]
    outs = [(ia[:, 0:LANES] + wi * ie[:, 0:LANES])
            / jnp.maximum(jnp.abs(ia[:, LANES:] + wi * ie[:, LANES:]), jnp.exp(-m))
            for ia, wi, ie, m in zip(intra, w_inter, inter, m_j)]
    for ci in range(n_chunks):
        h_ref[pl.ds(ci * L, L), :] = jnp.concatenate(outs[ci * n_pairs:(ci + 1) * n_pairs], axis=1)


def _gate_expand_matrix():
    lane = jnp.arange(LANES)[:, None]
    col = jnp.arange(2 * MLSTM_WIDTH)[None, :]
    src = jnp.where(col < MLSTM_WIDTH, col // HEAD_DIM, MLSTM_HEADS + (col - MLSTM_WIDTH) // HEAD_DIM)
    return (lane == src).astype(BF16)


def _mlstm_call(q, k, v, g):
    B, S, W = q.shape
    ts = SEQ_ROWS
    spec = pl.BlockSpec((None, ts, W), lambda bi, c: (bi, c, 0))
    gspec = pl.BlockSpec((None, ts, LANES), lambda bi, c: (bi, c, 0))
    espec = pl.BlockSpec((LANES, 2 * W), lambda bi, c: (0, 0))
    return pl.pallas_call(
        _mlstm_kernel, grid=(B, S // ts), in_specs=[spec, spec, spec, gspec, espec], out_specs=spec,
        out_shape=jax.ShapeDtypeStruct((B, S, W), F32),
        scratch_shapes=[pltpu.VMEM((MLSTM_HEADS // 2, LANES, 2 * LANES), F32),
                        pltpu.VMEM((8, LANES), F32)],
        compiler_params=pltpu.CompilerParams(dimension_semantics=("arbitrary", "arbitrary"),
                                             vmem_limit_bytes=VMEM_LIMIT),
        name="mlstm",
    )(q, k, v, g, _gate_expand_matrix())


def _layer_norm(z, g, b):
    mu = jnp.mean(z, axis=-1, keepdims=True)
    d = z - mu
    var = jnp.mean(d * d, axis=-1, keepdims=True)
    return d * lax.rsqrt(var + LN_EPS) * g + b


def _post_kernel(h_ref, ypool_ref, yr_ref, bonus_ref, g_ref, hm_ref, mo_ref,
                 lnxg_ref, lnxb_ref, ng_ref, wout_ref, ln1g_ref, ln1b_ref, rwt_ref, rb_ref, ones_ref,
                 h1_ref, gate_ref, route_ref, cnt_ref):
    ts = PREP_ROWS
    tiles = range(h_ref.shape[0] // ts)
    rows = lambda ref, i: ref[pl.ds(i * ts, ts), :]
    ones_bd = ones_ref[...]
    nrm_r = [_head_norm(rows(yr_ref, i), ones_bd, RWKV_GN_EPS) for i in tiles]
    nrm_m = [_head_norm(rows(hm_ref, i), ones_bd, LN_EPS) for i in tiles]
    y_rwkv = [(n * lnxg_ref[...] + lnxb_ref[...] + rows(bonus_ref, i)) * rows(g_ref, i) for i, n in zip(tiles, nrm_r)]
    y_ml = [rows(mo_ref, i) * (n * ng_ref[...]) for i, n in zip(tiles, nrm_m)]
    mix = [_dot(rows(ypool_ref, i), wout_ref[0:POOL_WIDTH, :])
           + _dot(y_rwkv[i], wout_ref[POOL_WIDTH:POOL_WIDTH + RWKV_WIDTH, :])
           + _dot(y_ml[i], wout_ref[POOL_WIDTH + RWKV_WIDTH:, :]) for i in tiles]
    h1 = [_layer_norm(DEEPNORM_ALPHA * rows(h_ref, i) + mix[i], ln1g_ref[...], ln1b_ref[...]) for i in tiles]
    for i in tiles:
        h1_ref[pl.ds(i * ts, ts), :] = h1[i]

    wh, wl = _split(rwt_ref[...])
    nt = lambda a, b: lax.dot_general(a, b, (((1,), (1,)), ((), ())), preferred_element_type=F32)
    parts = [_split(x) for x in h1]
    vals = [nt(wh, hh) + nt(wh, hl) + nt(wl, hh) + rb_ref[...] for hh, hl in parts]
    eidx = lax.broadcasted_iota(I32, (N_EXPERTS, ts), 0)
    tops, hots = [[] for _ in tiles], [[] for _ in tiles]
    for _ in range(TOP_K):
        mx = [jnp.max(v, axis=0, keepdims=True) for v in vals]
        idx = [jnp.min(jnp.where(v == m, eidx, N_EXPERTS), axis=0, keepdims=True) for v, m in zip(vals, mx)]
        hot = [eidx == ix for ix in idx]
        vals = [jnp.where(h, -jnp.inf, v) for h, v in zip(hot, vals)]
        for i in tiles:
            tops[i].append(mx[i])
            hots[i].append(hot[i])
    for i in tiles:
        exps = [jnp.exp(t - tops[i][0]) for t in tops[i]]
        denom = exps[0] + exps[1] + exps[2] + exps[3]
        gate_ref[:, pl.ds(i * ts, ts)] = jnp.concatenate([e / denom for e in exps]
                                                         + [jnp.zeros((8 - TOP_K, ts), F32)], axis=0)

    hot_f = [(h[0] | h[1] | h[2] | h[3]).astype(F32) for h in hots]
    r_i = lax.broadcasted_iota(I32, (ts, ts), 0)
    c_i = lax.broadcasted_iota(I32, (ts, ts), 1)
    before = (r_i < c_i).astype(BF16)
    within = [jnp.dot(hf.astype(BF16), before, preferred_element_type=F32) for hf in hot_f]
    n_e = [jnp.broadcast_to(jnp.sum(hf, axis=1, keepdims=True), (N_EXPERTS, LANES)) for hf in hot_f]
    run_len = [jnp.floor((n + (RUN_PIECE - 1)) * (1.0 / RUN_PIECE)) * RUN_PIECE for n in n_e]
    e_r = lax.broadcasted_iota(I32, (N_EXPERTS, N_EXPERTS), 0)
    e_c = lax.broadcasted_iota(I32, (N_EXPERTS, N_EXPERTS), 1)
    lower = (e_c < e_r).astype(BF16)
    run_start = [_dot_rhs2(lower, rl)[:, 0:1] for rl in run_len]
    for i in tiles:
        slots = [jnp.sum(jnp.where(hot, run_start[i] + within[i], 0.0), axis=0, keepdims=True) for hot in hots[i]]
        route_ref[:, pl.ds(i * ts, ts)] = jnp.concatenate([p.astype(I32) for p in slots]
                                                          + [jnp.zeros((8 - TOP_K, ts), I32)], axis=0)
        cnt_ref[pl.ds(i * N_EXPERTS, N_EXPERTS), :] = n_e[i]


def _post_call(h, ypool, yr, bonus, g, hm, mo, lw):
    T = h.shape[0]
    ts = POST_TILES * PREP_ROWS
    row = lambda c: pl.BlockSpec((ts, c), lambda i: (i, 0))
    colb = lambda r: pl.BlockSpec((r, ts), lambda i: (0, i))
    full = lambda a: pl.BlockSpec(a.shape, lambda i: (0,) * a.ndim)
    params = [lw['lnx_g'], lw['lnx_b'], lw['norm_g'], lw['w_out'], lw['ln1_g'], lw['ln1_b'],
              lw['router_wt'], lw['router_b'], lw['ones_bd']]
    in_specs = ([row(D_MODEL), row(POOL_WIDTH)] + [row(RWKV_WIDTH)] * 5 + [full(a) for a in params])
    out_shape = [jax.ShapeDtypeStruct((T, D_MODEL), F32), jax.ShapeDtypeStruct((8, T), F32),
                 jax.ShapeDtypeStruct((8, T), I32),
                 jax.ShapeDtypeStruct((T // PREP_ROWS * N_EXPERTS, LANES), F32)]
    out_specs = [row(D_MODEL), colb(8), colb(8), pl.BlockSpec((POST_TILES * N_EXPERTS, LANES), lambda i: (i, 0))]
    return pl.pallas_call(
        _post_kernel, grid=(T // ts,), in_specs=in_specs, out_specs=out_specs, out_shape=out_shape,
        compiler_params=pltpu.CompilerParams(dimension_semantics=("arbitrary",),
                                             vmem_limit_bytes=VMEM_LIMIT),
        name="post",
    )(h, ypool, yr, bonus, g, hm, mo, *params)


def _rows_copy(src_ref, src_row, dst_ref, dst_row, n_rows, sem):
    src_row, dst_row = [r if isinstance(r, int) else pl.multiple_of(r, SUBLANES) for r in (src_row, dst_row)]
    return pltpu.make_async_copy(src_ref.at[pl.ds(src_row, n_rows), :], dst_ref.at[pl.ds(dst_row, n_rows), :], sem)


def _stage_rows(ts):
    n = TOP_K * ts + N_EXPERTS * (RUN_PIECE - 1)
    return -(-n // SUBLANES) * SUBLANES


def _for_each_piece(n_pieces, body):
    lax.fori_loop(0, n_pieces, lambda j, c: (body(j), c)[1], 0)


def _run_copies(tab_ref, tile, tile_ref, tile_is_src, hbm_ref, sem, wait):
    for e in range(N_EXPERTS):
        n_rows = pl.multiple_of(tab_ref[tile, 0, e], SUBLANES)

        @pl.when(n_rows > 0)
        def _(e=e, n_rows=n_rows):
            in_tile = (tile_ref, tab_ref[tile, 0, N_EXPERTS + e])
            in_hbm = (hbm_ref, tab_ref[tile, 0, 2 * N_EXPERTS + e])
            (src_ref, src_row), (dst_ref, dst_row) = (in_tile, in_hbm) if tile_is_src else (in_hbm, in_tile)
            cp = _rows_copy(src_ref, src_row, dst_ref, dst_row, n_rows, sem)
            cp.wait() if wait else cp.start()


def _dispatch_kernel(tab_ref, prev_tab_ref, fill_ref, route_ref, gate_ref, h_ref, buf_ref,
                     sorted_scr, zero_scr, sems):
    step = pl.program_id(0)
    last = pl.num_programs(0) - 1
    slot = step % 2
    ts = PREP_ROWS
    tiles = range(MOVE_TILES)
    n_sorted = sorted_scr.shape[2]

    @pl.when(step == 0)
    def _():
        zero_scr[...] = jnp.zeros(zero_scr.shape, F32)

        def fill(wait):
            for e in range(N_EXPERTS):
                def piece(j, e=e):
                    cp = _rows_copy(zero_scr, 0, buf_ref, fill_ref[0, e] + j * FILL_PIECE, FILL_PIECE, sems.at[0])
                    cp.wait() if wait else cp.start()
                _for_each_piece(fill_ref[0, N_EXPERTS + e], piece)

            def tail(j):
                cp = _rows_copy(zero_scr, 0, buf_ref, fill_ref[0, 2 * N_EXPERTS] + j * EXPERT_ROWS,
                                EXPERT_ROWS, sems.at[0])
                cp.wait() if wait else cp.start()
            _for_each_piece(fill_ref[0, 2 * N_EXPERTS + 1], tail)

        fill(False)
        fill(True)

    pos = lax.broadcasted_iota(I32, (n_sorted, ts), 0)
    hots, weights = [], []
    for t in tiles:
        cols = pl.ds(t * ts, ts)
        hot = None
        weight = None
        for kslot in range(TOP_K):
            eq = pos == route_ref[kslot:kslot + 1, cols]
            w = jnp.where(eq, gate_ref[kslot:kslot + 1, cols], 0.0)
            hot = eq if hot is None else (hot | eq)
            weight = w if weight is None else weight + w
        hots.append(hot)
        weights.append(weight)
    rows_sorted = [jnp.dot(hots[t].astype(BF16), h_ref[pl.ds(t * ts, ts), :].astype(BF16),
                           preferred_element_type=F32) for t in tiles]
    ones = jnp.ones((ts, LANES), BF16)
    gate_sorted = [_dot_lhs2(weights[t], ones) for t in tiles]
    for t in tiles:
        sorted_scr[slot, t, :, 0:D_MODEL] = rows_sorted[t]
        sorted_scr[slot, t, :, D_MODEL:BUF_WIDTH] = gate_sorted[t]

    def copies(tab, which, wait):
        for t in tiles:
            _run_copies(tab, t, sorted_scr.at[which, t], True, buf_ref, sems.at[which], wait)

    @pl.when(step > 0)
    def _():
        copies(prev_tab_ref, 1 - slot, True)

    copies(tab_ref, slot, False)

    @pl.when(step == last)
    def _():
        copies(tab_ref, slot, True)


def _dispatch_call(tab_d, fill_tab, route8, gate8, h1, n_rows):
    T = h1.shape[0]
    ts = MOVE_TILES * PREP_ROWS
    smem_tile = pl.BlockSpec((MOVE_TILES, 1, LANES), lambda i: (i, 0, 0), memory_space=pltpu.SMEM)
    smem_prev = pl.BlockSpec((MOVE_TILES, 1, LANES), lambda i: (jnp.maximum(i - 1, 0), 0, 0),
                             memory_space=pltpu.SMEM)
    return pl.pallas_call(
        _dispatch_kernel, grid=(T // ts,),
        in_specs=[smem_tile, smem_prev,
                  pl.BlockSpec((1, LANES), lambda i: (0, 0), memory_space=pltpu.SMEM),
                  pl.BlockSpec((8, ts), lambda i: (0, i)),
                  pl.BlockSpec((8, ts), lambda i: (0, i)),
                  pl.BlockSpec((ts, D_MODEL), lambda i: (i, 0))],
        out_specs=pl.BlockSpec(memory_space=pl.ANY),
        out_shape=jax.ShapeDtypeStruct((n_rows, BUF_WIDTH), F32),
        scratch_shapes=[pltpu.VMEM((2, MOVE_TILES, _stage_rows(PREP_ROWS), BUF_WIDTH), F32),
                        pltpu.VMEM((EXPERT_ROWS, BUF_WIDTH), F32),
                        pltpu.SemaphoreType.DMA((2,))],
        compiler_params=pltpu.CompilerParams(dimension_semantics=("arbitrary",),
                                             vmem_limit_bytes=VMEM_LIMIT),
        name="dispatch",
    )(tab_d, tab_d, fill_tab, route8, gate8, h1)


def _expert_kernel(be_ref, nb_ref, run_ref, next_ref, x_ref, wgu_hbm, bgu_ref, wdn_hbm, bdn_ref, o_ref,
                   wgu_f32, wdn_f32, wgu_scr, wdn_scr, sems, *, layer):
    j = pl.program_id(0)
    used = j < nb_ref[0]
    changed = jnp.logical_or(j == 0, be_ref[j] != be_ref[jnp.maximum(j - 1, 0)])

    def fetch(expert, slot):
        return (pltpu.make_async_copy(wgu_hbm.at[layer, expert], wgu_f32.at[slot], sems.at[0, slot]),
                pltpu.make_async_copy(wdn_hbm.at[layer, expert], wdn_f32.at[slot], sems.at[1, slot]))

    @pl.when(j == 0)
    def _():
        for cp in fetch(be_ref[0], 0):
            cp.start()

    @pl.when(jnp.logical_and(used, changed))
    def _():
        slot = run_ref[j] % 2
        for cp in fetch(be_ref[j], slot):
            cp.wait()
        wgu_scr[...] = wgu_f32[slot].astype(BF16)
        wdn_scr[...] = wdn_f32[slot].astype(BF16)

        @pl.when(next_ref[j] >= 0)
        def _():
            for cp in fetch(next_ref[j], 1 - slot):
                cp.start()

    @pl.when(used)
    def _():
        gu = jnp.dot(x_ref[:, 0:D_MODEL].astype(BF16), wgu_scr[...], preferred_element_type=F32) + bgu_ref[...]
        glu = jnp.minimum(gu[:, 0:D_FF], SWIGLU_LIMIT)
        lin = jnp.clip(gu[:, D_FF:], -SWIGLU_LIMIT, SWIGLU_LIMIT)
        act = glu * _sigmoid(SWIGLU_ALPHA * glu) * (lin + 1.0)
        y = jnp.dot(act.astype(BF16), wdn_scr[...], preferred_element_type=F32) + bdn_ref[...]
        gate = x_ref[:, D_MODEL:BUF_WIDTH]
        o_ref[...] = y * jnp.concatenate([gate] * (D_MODEL // LANES), axis=1)

    @pl.when(jnp.logical_not(used))
    def _():
        o_ref[...] = jnp.zeros(o_ref.shape, F32)


def _expert_call(block_e, n_used, buf, lw):
    n_rows = buf.shape[0]
    rows = EXPERT_ROWS
    n_blocks = n_rows // rows
    j = jnp.arange(n_blocks, dtype=I32)
    first = (j < n_used[0]) & ((j == 0) | (block_e != jnp.roll(block_e, 1)))
    run_idx = jnp.cumsum(first.astype(I32)) - 1
    later_first = lax.cummin(jnp.where(first, j, n_blocks)[::-1])[::-1]
    next_start = jnp.concatenate([later_first[1:], jnp.full((1,), n_blocks, I32)])
    next_e = jnp.where(next_start < n_blocks, block_e[jnp.minimum(next_start, n_blocks - 1)], -1).astype(I32)

    def blk(j, be, nb, run, nxt):
        return jnp.minimum(j, nb[0] - 1)

    layer = lw['layer']
    grid_spec = pltpu.PrefetchScalarGridSpec(
        num_scalar_prefetch=4, grid=(n_blocks,),
        in_specs=[pl.BlockSpec((rows, BUF_WIDTH), lambda j, *s: (blk(j, *s), 0)),
                  pl.BlockSpec(memory_space=pl.ANY),
                  pl.BlockSpec((None, None, 1, 2 * D_FF), lambda j, be, *s: (layer, be[j], 0, 0)),
                  pl.BlockSpec(memory_space=pl.ANY),
                  pl.BlockSpec((None, None, 1, D_MODEL), lambda j, be, *s: (layer, be[j], 0, 0))],
        out_specs=pl.BlockSpec((rows, D_MODEL), lambda j, *s: (j, 0)),
        scratch_shapes=[pltpu.VMEM((2, D_MODEL, 2 * D_FF), F32), pltpu.VMEM((2, D_FF, D_MODEL), F32),
                        pltpu.VMEM((D_MODEL, 2 * D_FF), BF16), pltpu.VMEM((D_FF, D_MODEL), BF16),
                        pltpu.SemaphoreType.DMA((2, 2))])
    return pl.pallas_call(
        functools.partial(_expert_kernel, layer=layer), grid_spec=grid_spec,
        out_shape=jax.ShapeDtypeStruct((n_rows, D_MODEL), F32),
        compiler_params=pltpu.CompilerParams(dimension_semantics=("arbitrary",),
                                             vmem_limit_bytes=56 * 1024 * 1024),
        name="expert",
    )(block_e, n_used, run_idx, next_e, buf, lw['w_gate_up'], lw['b_gate_up'], lw['w_down'], lw['b_down'])


def _combine_kernel(tab_ref, next_tab_ref, route_ref, h_ref, out_hbm_ref, ln2g_ref, ln2b_ref, o_ref,
                    stage_scr, sems):
    step = pl.program_id(0)
    last = pl.num_programs(0) - 1
    slot = step % 2
    ts = PREP_ROWS
    tiles = range(MOVE_TILES)
    n_stage = stage_scr.shape[2]

    def copies(tab, which, wait):
        for t in tiles:
            _run_copies(tab, t, stage_scr.at[which, t], False, out_hbm_ref, sems.at[which], wait)

    @pl.when(step == 0)
    def _():
        stage_scr[...] = jnp.zeros(stage_scr.shape, F32)
        copies(tab_ref, 0, False)

    @pl.when(step < last)
    def _():
        copies(next_tab_ref, 1 - slot, False)

    lane = lax.broadcasted_iota(I32, (ts, n_stage), 1).astype(F32)
    sels = []
    for t in tiles:
        slot_f = route_ref[:, pl.ds(t * ts, ts)].astype(F32)
        slot_cols = jnp.concatenate([slot_f, jnp.zeros((LANES - 8, ts), F32)], axis=0).T
        sel = None
        for kslot in range(TOP_K):
            eq = lane == slot_cols[:, kslot:kslot + 1]
            sel = eq if sel is None else (sel | eq)
        sels.append(sel.astype(BF16))
    copies(tab_ref, slot, True)
    ys = [jnp.dot(sels[t], stage_scr[slot, t].astype(BF16), preferred_element_type=F32) for t in tiles]
    for t in tiles:
        rows = pl.ds(t * ts, ts)
        o_ref[rows, :] = _layer_norm(DEEPNORM_ALPHA * h_ref[rows, :] + ys[t], ln2g_ref[...], ln2b_ref[...])


def _combine_call(tab_c, route8, h1, out_rows, lw):
    T = h1.shape[0]
    ts = MOVE_TILES * PREP_ROWS
    n_tiles = T // ts
    full = lambda a: pl.BlockSpec(a.shape, lambda i: (0,) * a.ndim)
    n_stage = _stage_rows(PREP_ROWS)
    return pl.pallas_call(
        _combine_kernel, grid=(n_tiles,),
        in_specs=[pl.BlockSpec((MOVE_TILES, 1, LANES), lambda i: (i, 0, 0), memory_space=pltpu.SMEM),
                  pl.BlockSpec((MOVE_TILES, 1, LANES), lambda i: (jnp.minimum(i + 1, n_tiles - 1), 0, 0),
                               memory_space=pltpu.SMEM),
                  pl.BlockSpec((8, ts), lambda i: (0, i)),
                  pl.BlockSpec((ts, D_MODEL), lambda i: (i, 0)),
                  pl.BlockSpec(memory_space=pl.ANY),
                  full(lw['ln2_g']), full(lw['ln2_b'])],
        out_specs=pl.BlockSpec((ts, D_MODEL), lambda i: (i, 0)),
        out_shape=jax.ShapeDtypeStruct((T, D_MODEL), F32),
        scratch_shapes=[pltpu.VMEM((2, MOVE_TILES, n_stage, D_MODEL), F32), pltpu.SemaphoreType.DMA((2,))],
        compiler_params=pltpu.CompilerParams(dimension_semantics=("arbitrary",),
                                             vmem_limit_bytes=VMEM_LIMIT),
        name="combine",
    )(tab_c, tab_c, route8, h1, out_rows, lw['ln2_g'], lw['ln2_b'])


def _block_diag_ones(width):
    hid = jnp.arange(width) // HEAD_DIM
    return (hid[:, None] == hid[None, :]).astype(BF16)


def _layer_params(l, w_in, pool_w, pool_scale, rwkv_mu, rwkv_w0, rwkv_w2, rwkv_a0, rwkv_a2, rwkv_g2,
                  rwkv_kk_scale, rwkv_ka, rwkv_rk, rwkv_lnx_g, rwkv_lnx_b, rwkv_v0, rwkv_v1, rwkv_v2,
                  mlstm_conv_w, mlstm_conv_b, mlstm_b_i, mlstm_b_f, mlstm_norm_g, w_out, ln1_g, ln1_b,
                  router_w, router_b, w_gate_up, b_gate_up, w_down, b_down, ln2_g, ln2_b):
    row = lambda a: a.reshape(1, -1).astype(F32)
    pad_cols = D_IN_PAD - D_IN
    if l > 0:
        extra = jnp.concatenate([rwkv_v1[l - 1], jnp.zeros((D_MODEL, pad_cols - VRES_LORA), F32)], axis=1)
        v0 = row(rwkv_v0[l - 1])
        v2 = jnp.zeros((LANES, RWKV_WIDTH), F32).at[VRES_OFF:VRES_OFF + VRES_LORA].set(rwkv_v2[l - 1])
    else:
        extra = jnp.zeros((D_MODEL, pad_cols), F32)
        v0 = jnp.zeros((1, RWKV_WIDTH), F32)
        v2 = jnp.zeros((LANES, RWKV_WIDTH), F32)
    pw = jnp.zeros((POOL_WIDTH, POOL_WIDTH), F32)
    for gi in range(len(POOL_WINDOWS)):
        sl = slice(gi * POOL_GROUP, (gi + 1) * POOL_GROUP)
        pw = pw.at[sl, sl].set(pool_w[l, gi])
    zero_lora = jnp.zeros((DECAY_LORA, RWKV_WIDTH), F32)
    gate_bias = jnp.zeros((1, LANES), F32).at[0, 0:MLSTM_HEADS].set(mlstm_b_i[l])
    gate_bias = gate_bias.at[0, MLSTM_HEADS:2 * MLSTM_HEADS].set(mlstm_b_f[l])
    return {
        'w_in': jnp.concatenate([w_in[l], extra], axis=1).astype(BF16),
        'pool_w': pw.astype(BF16), 'pool_scale': row(pool_scale[l]), 'mu': row(rwkv_mu[l]),
        'w0': row(rwkv_w0[l]), 'w2': jnp.concatenate([rwkv_w2[l], zero_lora], axis=0).astype(BF16),
        'a0': row(rwkv_a0[l]), 'a2': jnp.concatenate([zero_lora, rwkv_a2[l]], axis=0).astype(BF16),
        'g2': rwkv_g2[l].astype(BF16), 'kk_scale': row(rwkv_kk_scale[l]), 'ka': row(rwkv_ka[l]),
        'rk': row(rwkv_rk[l]), 'v0': v0, 'v2': v2.astype(BF16),
        'conv_w': mlstm_conv_w[l], 'conv_b': row(mlstm_conv_b[l]), 'gate_bias': gate_bias,
        'ones_bd': _block_diag_ones(RWKV_WIDTH),
        'lnx_g': row(rwkv_lnx_g[l]), 'lnx_b': row(rwkv_lnx_b[l]), 'norm_g': row(mlstm_norm_g[l]),
        'w_out': w_out[l].astype(BF16), 'ln1_g': row(ln1_g[l]), 'ln1_b': row(ln1_b[l]),
        'router_wt': router_w[l].T, 'router_b': router_b[l].reshape(N_EXPERTS, 1),
        'layer': l, 'w_gate_up': w_gate_up, 'b_gate_up': b_gate_up.reshape(-1, N_EXPERTS, 1, 2 * D_FF),
        'w_down': w_down, 'b_down': b_down.reshape(-1, N_EXPERTS, 1, D_MODEL),
        'ln2_g': row(ln2_g[l]), 'ln2_b': row(ln2_b[l]),
    }


def _layer(h, v_first, lw, *, has_vres):
    B, S, _ = h.shape
    T = B * S
    outs = _prep_call(h, lw, v_first, has_vres=has_vres)
    ypool, r, ld, k, v, kk, b, g, bonus, mq, mk, mv, mo, mg = outs
    yr = _rwkv_call(r, ld, k, v, kk, b)
    hm = _mlstm_call(mq, mk, mv, mg)
    flat = lambda a: a.reshape(T, a.shape[-1])
    h_flat = flat(h)
    h1, gate8, route8, cnt = _post_call(h_flat, flat(ypool), flat(yr), flat(bonus), flat(g), flat(hm),
                                        flat(mo), lw)
    n_tiles = T // PREP_ROWS
    cnt = cnt.reshape(n_tiles, N_EXPERTS, LANES)[:, :, 0].astype(I32)
    rows = EXPERT_ROWS
    run_len = (cnt + RUN_PIECE - 1) // RUN_PIECE * RUN_PIECE
    region = jnp.sum(run_len, axis=0)
    padded = (region + rows - 1) // rows * rows
    pad_end = jnp.cumsum(padded)
    pad_start = pad_end - padded
    n_blocks = -(-(T * TOP_K + n_tiles * N_EXPERTS * (RUN_PIECE - 1)) // rows) + N_EXPERTS
    n_used = (pad_end[-1] // rows).astype(I32).reshape(1)
    starts = jnp.minimum(jnp.arange(n_blocks, dtype=I32), n_used[0] - 1) * rows
    block_e = jnp.sum((pad_end[None, :] <= starts[:, None]).astype(I32), axis=1)
    run_src = jnp.cumsum(run_len, axis=1) - run_len
    run_dst = pad_start[None, :] + jnp.cumsum(run_len, axis=0) - run_len
    run_tab = jnp.concatenate([run_len, run_src, run_dst,
                               jnp.zeros((n_tiles, LANES - 3 * N_EXPERTS), I32)], axis=1).reshape(n_tiles, 1, LANES)
    fill_start = (pad_start + region) // FILL_PIECE * FILL_PIECE
    fill_tab = jnp.concatenate([fill_start, (pad_end - fill_start) // FILL_PIECE,
                                pad_end[-1:], n_blocks - n_used,
                                jnp.zeros((LANES - 2 * N_EXPERTS - 2,), I32)]).reshape(1, LANES)
    buf = _dispatch_call(run_tab, fill_tab, route8, gate8, h1, n_blocks * rows)
    out_rows = _expert_call(block_e, n_used, buf, lw)
    h2 = _combine_call(run_tab, route8, h1, out_rows, lw)
    return h2.reshape(B, S, D_MODEL), v


def kernel(x, w_in, pool_w, pool_scale, rwkv_mu, rwkv_w0, rwkv_w2, rwkv_a0, rwkv_a2, rwkv_g2, rwkv_kk_scale, rwkv_ka, rwkv_rk, rwkv_lnx_g, rwkv_lnx_b, rwkv_v0, rwkv_v1, rwkv_v2, mlstm_conv_w, mlstm_conv_b, mlstm_b_i, mlstm_b_f, mlstm_norm_g, w_out, ln1_g, ln1_b, router_w, router_b, w_gate_up, b_gate_up, w_down, b_down, ln2_g, ln2_b):
    weights = (w_in, pool_w, pool_scale, rwkv_mu, rwkv_w0, rwkv_w2, rwkv_a0, rwkv_a2, rwkv_g2, rwkv_kk_scale,
               rwkv_ka, rwkv_rk, rwkv_lnx_g, rwkv_lnx_b, rwkv_v0, rwkv_v1, rwkv_v2, mlstm_conv_w, mlstm_conv_b,
               mlstm_b_i, mlstm_b_f, mlstm_norm_g, w_out, ln1_g, ln1_b, router_w, router_b, w_gate_up,
               b_gate_up, w_down, b_down, ln2_g, ln2_b)
    h = x
    v_first = jnp.zeros(x.shape[:2] + (RWKV_WIDTH,), F32)
    for l in range(w_in.shape[0]):
        lw = _layer_params(l, *weights)
        h, v_l = _layer(h, v_first, lw, has_vres=l > 0)
        if l == 0:
            v_first = v_l
    return h
```

```python
import functools

import jax
import jax.numpy as jnp
from jax import lax
from jax.experimental import pallas as pl
from jax.experimental.pallas import tpu as pltpu

F32 = jnp.float32
BF16 = jnp.bfloat16
I32 = jnp.int32

D_MODEL = 1024
HEAD_DIM = 64
POOL_WINDOWS = (2, 4, 8, 16)
POOL_WIDTH = 256
POOL_GROUP = 64
RWKV_WIDTH = 384
RWKV_HEADS = 6
DECAY_LORA = 64
ICLR_LORA = 64
GATE_LORA = 128
VRES_LORA = 32
RWKV_GN_EPS = 64e-5
RWKV_COLS = 3 * RWKV_WIDTH + DECAY_LORA + ICLR_LORA + GATE_LORA
MLSTM_WIDTH = 384
MLSTM_HEADS = 6
MLSTM_CONV = 4
MLSTM_COLS = 4 * MLSTM_WIDTH + 2 * MLSTM_HEADS
D_IN = POOL_WIDTH + RWKV_COLS + MLSTM_COLS
N_EXPERTS = 32
TOP_K = 4
D_FF = D_MODEL
SWIGLU_LIMIT = 7.0
SWIGLU_ALPHA = 1.702
LN_EPS = 1e-5
DEPTH = 2
DEEPNORM_ALPHA = (2 * DEPTH) ** 0.25

LANES = 128
D_IN_PAD = 3328
RWKV_OFF = POOL_WIDTH
MLSTM_OFF = POOL_WIDTH + RWKV_COLS
GATE_OFF = MLSTM_OFF + 4 * MLSTM_WIDTH
VRES_OFF = 2 * MLSTM_HEADS
HALO = 16
CHUNK = 64

PROJ_ROWS = 512
PREP_ROWS = 256
POST_TILES = 4
MOVE_TILES = 2
SEQ_ROWS = 512
EXPERT_ROWS = 512
BUF_WIDTH = D_MODEL + LANES
SUBLANES = 8
RUN_PIECE = SUBLANES
FILL_PIECE = 64
VMEM_LIMIT = 48 * 1024 * 1024


def _dot(a, b):
    return jnp.dot(a.astype(BF16), b.astype(BF16), preferred_element_type=F32)


def _dot_nt(a, b):
    return lax.dot_general(a.astype(BF16), b.astype(BF16), (((1,), (1,)), ((), ())),
                           preferred_element_type=F32)


def _dot_tn(a, b):
    return lax.dot_general(a.astype(BF16), b.astype(BF16), (((0,), (0,)), ((), ())),
                           preferred_element_type=F32)


def _split(x):
    hi = x.astype(BF16)
    lo = (x - hi.astype(F32)).astype(BF16)
    return hi, lo


def _dot_lhs2(a, b_bf16):
    hi, lo = _split(a)
    return (jnp.dot(hi, b_bf16, preferred_element_type=F32)
            + jnp.dot(lo, b_bf16, preferred_element_type=F32))


def _dot_rhs2(a_bf16, b):
    hi, lo = _split(b)
    return (jnp.dot(a_bf16, hi, preferred_element_type=F32)
            + jnp.dot(a_bf16, lo, preferred_element_type=F32))


def _sigmoid(x):
    return 1.0 / (1.0 + jnp.exp(-x))


def _softplus(x):
    return jnp.maximum(x, 0.0) + jnp.log(1.0 + jnp.exp(-jnp.abs(x)))


def _head_norm(y, ones_bd, eps):
    inv = 1.0 / HEAD_DIM
    mean = _dot(y, ones_bd) * inv
    d = y - mean
    var = _dot(d * d, ones_bd) * inv
    return d * lax.rsqrt(var + eps)


def _prep_kernel(x_ref, w_ref, poolw_ref, pscale_ref, mu_ref, w0_ref, w2_ref, a0_ref, a2_ref, g2_ref,
                 kks_ref, ka_ref, rk_ref, v0_ref, v2_ref, vfirst_ref, cw_ref, cb_ref, gbias_ref, ones_ref,
                 ypool_ref, r_ref, ld_ref, k_ref, v_ref, kk_ref, b_ref, g_ref, bonus_ref,
                 mq_ref, mk_ref, mv_ref, mo_ref, mg_ref,
                 p_scr, *, has_vres):
    i = pl.program_id(1)
    ts = x_ref.shape[0]

    @pl.when(i == 0)
    def _():
        p_scr[0:HALO, :] = jnp.zeros((HALO, D_IN_PAD), F32)

    @pl.when(i > 0)
    def _():
        p_scr[0:HALO, :] = p_scr[ts:ts + HALO, :]

    p_scr[HALO:HALO + ts, :] = jnp.dot(x_ref[...].astype(BF16), w_ref[...], preferred_element_type=F32)

    def rows(shift, c0, c1):
        return p_scr[HALO - shift:HALO - shift + ts, c0:c1]

    u = rows(0, 0, POOL_WIDTH)
    acc = u
    sums = {}
    for s in range(1, POOL_WINDOWS[-1]):
        acc = acc + rows(s, 0, POOL_WIDTH)
        if s + 1 in POOL_WINDOWS:
            sums[s + 1] = acc
    pos = (i * ts + lax.broadcasted_iota(I32, (ts, 1), 0) + 1).astype(F32)
    lane = lax.broadcasted_iota(I32, (ts, POOL_WIDTH), 1)
    d = None
    for gi, win in reversed(list(enumerate(POOL_WINDOWS))):
        dg = sums[win] / jnp.minimum(pos, float(win))
        d = dg if d is None else jnp.where(lane < (gi + 1) * POOL_GROUP, dg, d)
    d = d - u
    ypool_ref[...] = (_dot(d, poolw_ref[...]) * pscale_ref[...]).astype(ypool_ref.dtype)

    cur = rows(0, RWKV_OFF, RWKV_OFF + RWKV_COLS)
    prev = rows(1, RWKV_OFF, RWKV_OFF + RWKV_COLS)
    pf = cur + mu_ref[...] * (prev - cur)
    W = RWKV_WIDTH
    r = pf[:, 0:W]
    k = pf[:, W:2 * W]
    v = pf[:, 2 * W:3 * W]
    z = pf[:, 3 * W:3 * W + LANES]
    gd = pf[:, 3 * W + LANES:3 * W + 2 * LANES]
    w_log = -_softplus(-(w0_ref[...] + _dot(jnp.tanh(z), w2_ref[...]))) - 0.5
    ld_ref[...] = -jnp.exp(w_log)
    a = _sigmoid(a0_ref[...] + _dot(z, a2_ref[...]))
    g_ref[...] = _dot(_sigmoid(gd), g2_ref[...]).astype(g_ref.dtype)
    gates = rows(0, GATE_OFF, GATE_OFF + LANES)
    if has_vres:
        v_gate = _sigmoid(v0_ref[...] + _dot(gates, v2_ref[...]))
        v = v + (vfirst_ref[...] - v) * v_gate
    ones_bd = ones_ref[...]
    kk = k * kks_ref[...]
    kk = kk / jnp.maximum(jnp.sqrt(_dot(kk * kk, ones_bd)), 1e-12)
    k = k * (1.0 + (a - 1.0) * ka_ref[...])
    r_ref[...] = r.astype(r_ref.dtype)
    k_ref[...] = k.astype(k_ref.dtype)
    v_ref[...] = v.astype(v_ref.dtype)
    kk_ref[...] = kk.astype(kk_ref.dtype)
    b_ref[...] = (kk * a).astype(b_ref.dtype)
    bonus_ref[...] = (_dot(r * k * rk_ref[...], ones_bd) * v).astype(bonus_ref.dtype)

    qk = cb_ref[...] + rows(0, MLSTM_OFF, MLSTM_OFF + 2 * MLSTM_WIDTH) * cw_ref[MLSTM_CONV - 1:MLSTM_CONV, :]
    for tap in range(MLSTM_CONV - 1):
        shift = MLSTM_CONV - 1 - tap
        qk = qk + rows(shift, MLSTM_OFF, MLSTM_OFF + 2 * MLSTM_WIDTH) * cw_ref[tap:tap + 1, :]
    qk = qk * _sigmoid(qk)
    mq_ref[...] = (qk[:, 0:MLSTM_WIDTH] * (HEAD_DIM ** -0.5)).astype(mq_ref.dtype)
    mk_ref[...] = qk[:, MLSTM_WIDTH:].astype(mk_ref.dtype)
    mv_ref[...] = rows(0, MLSTM_OFF + 2 * MLSTM_WIDTH, MLSTM_OFF + 3 * MLSTM_WIDTH).astype(mv_ref.dtype)
    mo_ref[...] = _sigmoid(rows(0, MLSTM_OFF + 3 * MLSTM_WIDTH, MLSTM_OFF + 4 * MLSTM_WIDTH)).astype(mo_ref.dtype)
    gb = gates + gbias_ref[...]
    glane = lax.broadcasted_iota(I32, (ts, LANES), 1)
    mg_ref[...] = jnp.where(glane < MLSTM_HEADS, gb, -_softplus(-gb))


def _prep_call(x, lw, vfirst, *, has_vres):
    B, S, _ = x.shape
    ts = PROJ_ROWS
    grid = (B, S // ts)
    row3 = lambda c: pl.BlockSpec((None, ts, c), lambda b, i: (b, i, 0))
    full = lambda a: pl.BlockSpec(a.shape, lambda b, i: (0,) * a.ndim)
    params = [lw['w_in'], lw['pool_w'], lw['pool_scale'], lw['mu'], lw['w0'], lw['w2'], lw['a0'], lw['a2'],
              lw['g2'], lw['kk_scale'], lw['ka'], lw['rk'], lw['v0'], lw['v2']]
    tail = [lw['conv_w'], lw['conv_b'], lw['gate_bias'], lw['ones_bd']]
    in_specs = ([row3(D_MODEL)] + [full(a) for a in params] + [row3(RWKV_WIDTH)] + [full(a) for a in tail])
    widths = [POOL_WIDTH] + [RWKV_WIDTH] * 8 + [MLSTM_WIDTH] * 4 + [LANES]
    dtypes = [F32 if i in (2, len(widths) - 1) else BF16 for i in range(len(widths))]
    out_shape = [jax.ShapeDtypeStruct((B, S, c), dt) for c, dt in zip(widths, dtypes)]
    out_specs = [row3(c) for c in widths]
    return pl.pallas_call(
        functools.partial(_prep_kernel, has_vres=has_vres),
        grid=grid, in_specs=in_specs, out_specs=out_specs, out_shape=out_shape,
        scratch_shapes=[pltpu.VMEM((HALO + ts, D_IN_PAD), F32)],
        compiler_params=pltpu.CompilerParams(dimension_semantics=("arbitrary", "arbitrary"),
                                             vmem_limit_bytes=VMEM_LIMIT),
        name="prep",
    )(x, *params, vfirst, *tail)


def _rwkv_kernel(r_ref, ld_ref, k_ref, v_ref, kk_ref, b_ref, y_ref, s_scr):
    c = pl.program_id(1)
    L = CHUNK
    n_chunks = r_ref.shape[0] // L

    @pl.when(c == 0)
    def _():
        s_scr[...] = jnp.zeros(s_scr.shape, F32)

    row = lax.broadcasted_iota(I32, (L, L), 0)
    col = lax.broadcasted_iota(I32, (L, L), 1)
    tri = (col <= row).astype(BF16)
    row_l = lax.broadcasted_iota(I32, (L, LANES), 0)
    col_l = lax.broadcasted_iota(I32, (L, LANES), 1) % HEAD_DIM
    strict = col_l < row_l
    incl = col_l <= row_l
    eye = (col_l == row_l).astype(F32)
    r2 = lax.broadcasted_iota(I32, (LANES, LANES), 0)
    c2 = lax.broadcasted_iota(I32, (LANES, LANES), 1)
    same_head = (r2 // HEAD_DIM) == (c2 // HEAD_DIM)
    eye2 = (r2 == c2).astype(F32)
    low_lanes = lax.broadcasted_iota(I32, (L, LANES), 1) < HEAD_DIM
    n_pairs = RWKV_HEADS // 2

    def block_diag(x):
        xb = x.astype(BF16)
        return jnp.where(same_head, jnp.concatenate([xb, xb], axis=0), jnp.zeros((), BF16))

    def diag_blocks(z):
        return jnp.where(low_lanes, z[0:L, :], z[L:2 * L, :])

    units = [(ci, p) for ci in range(n_chunks) for p in range(n_pairs)]
    per_chunk = []
    for ci in range(n_chunks):
        sl = pl.ds(ci * L, L)
        ld = ld_ref[sl, :]
        cin = _dot_rhs2(tri, ld)
        c_last = cin[L - 1:L, :]
        e_neg = jnp.exp(-cin)
        e_tail = jnp.exp(c_last - cin)
        kk = kk_ref[sl, :]
        bb = b_ref[sl, :]
        kx = k_ref[sl, :]
        per_chunk.append(dict(
            A=-kk * jnp.exp(cin - ld), R=r_ref[sl, :] * jnp.exp(cin), B=bb * e_neg, K=kx * e_neg,
            Bh=bb * e_tail, Kh=kx * e_tail, V=v_ref[sl, :], g_last=jnp.exp(c_last)))

    def part(name, u):
        ci, p = u
        return per_chunk[ci][name][:, p * LANES:(p + 1) * LANES]

    M = [_dot_nt(jnp.concatenate([part('A', u), part('R', u)], axis=0),
                 jnp.concatenate([block_diag(part('B', u)), block_diag(part('K', u))], axis=0)) for u in units]
    m_ab = [jnp.where(strict, m[0:L, 0:LANES], 0.0) for m in M]
    m_ak = [jnp.where(strict, m[0:L, LANES:], 0.0) for m in M]
    m_rb = [jnp.where(incl, m[L:, 0:LANES], 0.0) for m in M]
    m_rk = [jnp.where(incl, m[L:, LANES:], 0.0) for m in M]
    MVYK = [_dot(jnp.concatenate([ak, rk], axis=0), block_diag(part('V', u)))
            for ak, rk, u in zip(m_ak, m_rk, units)]
    MV = [x[0:L, :] for x in MVYK]
    YK = [x[L:, :] for x in MVYK]
    T = [eye + m for m in m_ab]
    pw = [_dot(m, block_diag(m)) for m in m_ab]
    for _ in range(4):
        both = [_dot(jnp.concatenate([t, p], axis=0), block_diag(p)) for t, p in zip(T, pw)]
        T = [t + x[0:L, :] for t, x in zip(T, both)]
        pw = [x[L:, :] for x in both]
    T = [t + _dot(t, block_diag(p)) for t, p in zip(T, pw)]
    WU = [_dot(t, jnp.concatenate([block_diag(part('A', u)), block_diag(mv)], axis=1))
          for t, u, mv in zip(T, units, MV)]
    GY = [_dot(m, jnp.concatenate([block_diag(wu[:, 0:LANES]), block_diag(wu[:, LANES:])], axis=1))
          for m, wu in zip(m_rb, WU)]
    G = [part('R', u) + gy[:, 0:LANES] for u, gy in zip(units, GY)]
    Y0 = [gy[:, LANES:] + yk for gy, yk in zip(GY, YK)]
    P = [jnp.where(same_head, _dot_tn(wu[:, 0:LANES], part('Bh', u)), 0.0) + eye2 * part('g_last', u)
         for wu, u in zip(WU, units)]
    Q = [diag_blocks(_dot_tn(jnp.concatenate([wu[:, LANES:], part('V', u)], axis=0),
                             jnp.concatenate([part('Bh', u), part('Kh', u)], axis=0)))
         for wu, u in zip(WU, units)]

    state = [s_scr[p] for p in range(n_pairs)]
    for ci in range(n_chunks):
        base = ci * n_pairs
        ys = [_dot_nt(G[base + p], block_diag(state[p])) + Y0[base + p] for p in range(n_pairs)]
        state = [_dot(state[p], P[base + p]) + Q[base + p] for p in range(n_pairs)]
        y_ref[pl.ds(ci * L, L), :] = jnp.concatenate(ys, axis=1).astype(y_ref.dtype)
    for p in range(n_pairs):
        s_scr[p] = state[p]


def _rwkv_call(r, ld, k, v, kk, b):
    B, S, W = r.shape
    ts = SEQ_ROWS
    spec = pl.BlockSpec((None, ts, W), lambda bi, c: (bi, c, 0))
    return pl.pallas_call(
        _rwkv_kernel, grid=(B, S // ts), in_specs=[spec] * 6, out_specs=spec,
        out_shape=jax.ShapeDtypeStruct((B, S, W), BF16),
        scratch_shapes=[pltpu.VMEM((RWKV_HEADS // 2, HEAD_DIM, LANES), F32)],
        compiler_params=pltpu.CompilerParams(dimension_semantics=("arbitrary", "arbitrary"),
                                             vmem_limit_bytes=VMEM_LIMIT),
        name="rwkv",
    )(r, ld, k, v, kk, b)


def _mlstm_kernel(q_ref, k_ref, v_ref, g_ref, expand_ref, h_ref, cn_scr, m_scr):
    c = pl.program_id(1)
    L = CHUNK
    n_chunks = q_ref.shape[0] // L
    H = MLSTM_HEADS
    n_pairs = H // 2

    @pl.when(c == 0)
    def _():
        cn_scr[...] = jnp.zeros(cn_scr.shape, F32)
        m_scr[...] = jnp.zeros(m_scr.shape, F32)

    row = lax.broadcasted_iota(I32, (L, L), 0)
    col = lax.broadcasted_iota(I32, (L, L), 1)
    tri = (col <= row).astype(BF16)
    row_l = lax.broadcasted_iota(I32, (L, LANES), 0)
    col_l = lax.broadcasted_iota(I32, (L, LANES), 1)
    incl2 = (col_l % HEAD_DIM) <= row_l
    r2 = lax.broadcasted_iota(I32, (LANES, LANES), 0)
    c2 = lax.broadcasted_iota(I32, (LANES, LANES), 1)
    same_head = (r2 // HEAD_DIM) == (c2 // HEAD_DIM)
    same_head2 = jnp.concatenate([same_head, same_head], axis=1)

    units = [(ci, p) for ci in range(n_chunks) for p in range(n_pairs)]
    ig_rep, g_rep, x_t = [], [], []
    for ci in range(n_chunks):
        gt = g_ref[pl.ds(ci * L, L), :]
        rep = _dot_lhs2(gt, expand_ref[...])
        ig_rep.append(rep[:, 0:MLSTM_WIDTH])
        g_rep.append(_dot_rhs2(tri, rep[:, MLSTM_WIDTH:]))
        x_t.append(gt.T[0:H, :] - _dot_rhs2(tri, gt).T[H:2 * H, :])

    def part(ref, u):
        ci, p = u
        return ref[pl.ds(ci * L, L), p * LANES:(p + 1) * LANES]

    def pair(x, p):
        return x[:, p * LANES:(p + 1) * LANES]

    def block_diag(x):
        return jnp.where(same_head, jnp.concatenate([x, x], axis=0), 0.0)

    ig_c = [pair(ig_rep[ci], p) for ci, p in units]
    g_c = [pair(g_rep[ci], p) for ci, p in units]
    g_last = [g[L - 1:L, :] for g in g_c]
    x_row = [jnp.concatenate([x_t[ci][2 * p:2 * p + 1, :], x_t[ci][2 * p + 1:2 * p + 2, :]], axis=1)
             for ci, p in units]
    d_log = [jnp.where(incl2, g + xr, -jnp.inf) for g, xr in zip(g_c, x_row)]
    x_run = [ig - g for ig, g in zip(ig_c, g_c)]
    shift = 1
    while shift < L:
        x_run = [jnp.maximum(x, jnp.where(row_l >= shift, pltpu.roll(x, shift, 0), -jnp.inf)) for x in x_run]
        shift *= 2
    d_max = [g + x for g, x in zip(g_c, x_run)]
    qk = [_dot_nt(part(q_ref, u), block_diag(part(k_ref, u))) for u in units]
    e = [gl - g + ig for gl, g, ig in zip(g_last, g_c, ig_c)]
    m_loc = [jnp.max(x, axis=0, keepdims=True) for x in e]
    wk = [part(k_ref, u) * jnp.exp(x - m) for u, x, m in zip(units, e, m_loc)]
    ones = jnp.ones((L, LANES), F32)
    kvn_loc = [jnp.where(same_head2, _dot_tn(w, jnp.concatenate([part(v_ref, u), ones], axis=1)), 0.0)
               for w, u in zip(wk, units)]

    cn_st = [cn_scr[p] for p in range(n_pairs)]
    m_st = [m_scr[p:p + 1, :] for p in range(n_pairs)]
    cn_prev, m_prev = [], []
    for i, (ci, p) in enumerate(units):
        cn_prev.append(cn_st[p])
        m_prev.append(m_st[p])
        m_new = jnp.maximum(g_last[i] + m_st[p], m_loc[i])
        a_old = jnp.exp(g_last[i] + m_st[p] - m_new)
        a_new = jnp.exp(m_loc[i] - m_new)
        cn_st[p] = (jnp.concatenate([a_old, a_old], axis=1) * cn_st[p]
                    + jnp.concatenate([a_new, a_new], axis=1) * kvn_loc[i])
        m_st[p] = m_new
    for p in range(n_pairs):
        cn_scr[p] = cn_st[p]
        m_scr[p:p + 1, :] = m_st[p]

    inter_log = [g + m for g, m in zip(g_c, m_prev)]
    m_j = [jnp.maximum(dm, il) for dm, il in zip(d_max, inter_log)]
    w_intra = [jnp.exp(d - m) * s for d, m, s in zip(d_log, m_j, qk)]
    w_inter = [jnp.exp(il - m) for il, m in zip(inter_log, m_j)]
    ones_bd = same_head.astype(F32)
    intra = [_dot(w, jnp.concatenate([block_diag(part(v_ref, u)), ones_bd], axis=1))
             for w, u in zip(w_intra, units)]
    inter = [_dot(part(q_ref, u), cn) for u, cn in zip(units, cn_prev)]
    outs = [(ia[:, 0:LANES] + wi * ie[:, 0:LANES])
            / jnp.maximum(jnp.abs(ia[:, LANES:] + wi * ie[:, LANES:]), jnp.exp(-m))
            for ia, wi, ie, m in zip(intra, w_inter, inter, m_j)]
    for ci in range(n_chunks):
        h_ref[pl.ds(ci * L, L), :] = jnp.concatenate(outs[ci * n_pairs:(ci + 1) * n_pairs],
                                                      axis=1).astype(h_ref.dtype)


def _gate_expand_matrix():
    lane = jnp.arange(LANES)[:, None]
    col = jnp.arange(2 * MLSTM_WIDTH)[None, :]
    src = jnp.where(col < MLSTM_WIDTH, col // HEAD_DIM, MLSTM_HEADS + (col - MLSTM_WIDTH) // HEAD_DIM)
    return (lane == src).astype(BF16)


def _mlstm_call(q, k, v, g):
    B, S, W = q.shape
    ts = SEQ_ROWS
    spec = pl.BlockSpec((None, ts, W), lambda bi, c: (bi, c, 0))
    gspec = pl.BlockSpec((None, ts, LANES), lambda bi, c: (bi, c, 0))
    espec = pl.BlockSpec((LANES, 2 * W), lambda bi, c: (0, 0))
    return pl.pallas_call(
        _mlstm_kernel, grid=(B, S // ts), in_specs=[spec, spec, spec, gspec, espec], out_specs=spec,
        out_shape=jax.ShapeDtypeStruct((B, S, W), BF16),
        scratch_shapes=[pltpu.VMEM((MLSTM_HEADS // 2, LANES, 2 * LANES), F32),
                        pltpu.VMEM((8, LANES), F32)],
        compiler_params=pltpu.CompilerParams(dimension_semantics=("arbitrary", "arbitrary"),
                                             vmem_limit_bytes=VMEM_LIMIT),
        name="mlstm",
    )(q, k, v, g, _gate_expand_matrix())


def _layer_norm(z, g, b):
    mu = jnp.mean(z, axis=-1, keepdims=True)
    d = z - mu
    var = jnp.mean(d * d, axis=-1, keepdims=True)
    return d * lax.rsqrt(var + LN_EPS) * g + b


def _post_kernel(h_ref, ypool_ref, yr_ref, bonus_ref, g_ref, hm_ref, mo_ref,
                 lnxg_ref, lnxb_ref, ng_ref, wout_ref, ln1g_ref, ln1b_ref, rwt_ref, rb_ref, ones_ref,
                 h1_ref, gate_ref, route_ref, cnt_ref):
    ts = PREP_ROWS
    tiles = range(h_ref.shape[0] // ts)
    rows = lambda ref, i: ref[pl.ds(i * ts, ts), :]
    ones_bd = ones_ref[...]
    nrm_r = [_head_norm(rows(yr_ref, i), ones_bd, RWKV_GN_EPS) for i in tiles]
    nrm_m = [_head_norm(rows(hm_ref, i), ones_bd, LN_EPS) for i in tiles]
    y_rwkv = [(n * lnxg_ref[...] + lnxb_ref[...] + rows(bonus_ref, i)) * rows(g_ref, i) for i, n in zip(tiles, nrm_r)]
    y_ml = [rows(mo_ref, i) * (n * ng_ref[...]) for i, n in zip(tiles, nrm_m)]
    mix = [_dot(rows(ypool_ref, i), wout_ref[0:POOL_WIDTH, :])
           + _dot(y_rwkv[i], wout_ref[POOL_WIDTH:POOL_WIDTH + RWKV_WIDTH, :])
           + _dot(y_ml[i], wout_ref[POOL_WIDTH + RWKV_WIDTH:, :]) for i in tiles]
    h1 = [_layer_norm(DEEPNORM_ALPHA * rows(h_ref, i) + mix[i], ln1g_ref[...], ln1b_ref[...]) for i in tiles]
    for i in tiles:
        h1_ref[pl.ds(i * ts, ts), :] = h1[i]

    wh, wl = _split(rwt_ref[...])
    nt = lambda a, b: lax.dot_general(a, b, (((1,), (1,)), ((), ())), preferred_element_type=F32)
    parts = [_split(x) for x in h1]
    vals = [nt(wh, hh) + nt(wh, hl) + nt(wl, hh) + rb_ref[...] for hh, hl in parts]
    eidx = lax.broadcasted_iota(I32, (N_EXPERTS, ts), 0)
    tops, hots = [[] for _ in tiles], [[] for _ in tiles]
    for _ in range(TOP_K):
        mx = [jnp.max(v, axis=0, keepdims=True) for v in vals]
        idx = [jnp.min(jnp.where(v == m, eidx, N_EXPERTS), axis=0, keepdims=True) for v, m in zip(vals, mx)]
        hot = [eidx == ix for ix in idx]
        vals = [jnp.where(h, -jnp.inf, v) for h, v in zip(hot, vals)]
        for i in tiles:
            tops[i].append(mx[i])
            hots[i].append(hot[i])
    for i in tiles:
        exps = [jnp.exp(t - tops[i][0]) for t in tops[i]]
        denom = exps[0] + exps[1] + exps[2] + exps[3]
        gate_ref[:, pl.ds(i * ts, ts)] = jnp.concatenate([e / denom for e in exps]
                                                         + [jnp.zeros((8 - TOP_K, ts), F32)], axis=0)

    hot_f = [(h[0] | h[1] | h[2] | h[3]).astype(F32) for h in hots]
    r_i = lax.broadcasted_iota(I32, (ts, ts), 0)
    c_i = lax.broadcasted_iota(I32, (ts, ts), 1)
    before = (r_i < c_i).astype(BF16)
    within = [jnp.dot(hf.astype(BF16), before, preferred_element_type=F32) for hf in hot_f]
    n_e = [jnp.broadcast_to(jnp.sum(hf, axis=1, keepdims=True), (N_EXPERTS, LANES)) for hf in hot_f]
    run_len = [jnp.floor((n + (RUN_PIECE - 1)) * (1.0 / RUN_PIECE)) * RUN_PIECE for n in n_e]
    e_r = lax.broadcasted_iota(I32, (N_EXPERTS, N_EXPERTS), 0)
    e_c = lax.broadcasted_iota(I32, (N_EXPERTS, N_EXPERTS), 1)
    lower = (e_c < e_r).astype(BF16)
    run_start = [_dot_rhs2(lower, rl)[:, 0:1] for rl in run_len]
    for i in tiles:
        slots = [jnp.sum(jnp.where(hot, run_start[i] + within[i], 0.0), axis=0, keepdims=True) for hot in hots[i]]
        route_ref[:, pl.ds(i * ts, ts)] = jnp.concatenate([p.astype(I32) for p in slots]
                                                          + [jnp.zeros((8 - TOP_K, ts), I32)], axis=0)
        cnt_ref[pl.ds(i * N_EXPERTS, N_EXPERTS), :] = n_e[i]


def _post_call(h, ypool, yr, bonus, g, hm, mo, lw):
    T = h.shape[0]
    ts = POST_TILES * PREP_ROWS
    row = lambda c: pl.BlockSpec((ts, c), lambda i: (i, 0))
    colb = lambda r: pl.BlockSpec((r, ts), lambda i: (0, i))
    full = lambda a: pl.BlockSpec(a.shape, lambda i: (0,) * a.ndim)
    params = [lw['lnx_g'], lw['lnx_b'], lw['norm_g'], lw['w_out'], lw['ln1_g'], lw['ln1_b'],
              lw['router_wt'], lw['router_b'], lw['ones_bd']]
    in_specs = ([row(D_MODEL), row(POOL_WIDTH)] + [row(RWKV_WIDTH)] * 5 + [full(a) for a in params])
    out_shape = [jax.ShapeDtypeStruct((T, D_MODEL), F32), jax.ShapeDtypeStruct((8, T), F32),
                 jax.ShapeDtypeStruct((8, T), I32),
                 jax.ShapeDtypeStruct((T // PREP_ROWS * N_EXPERTS, LANES), F32)]
    out_specs = [row(D_MODEL), colb(8), colb(8), pl.BlockSpec((POST_TILES * N_EXPERTS, LANES), lambda i: (i, 0))]
    return pl.pallas_call(
        _post_kernel, grid=(T // ts,), in_specs=in_specs, out_specs=out_specs, out_shape=out_shape,
        compiler_params=pltpu.CompilerParams(dimension_semantics=("arbitrary",),
                                             vmem_limit_bytes=VMEM_LIMIT),
        name="post",
    )(h, ypool, yr, bonus, g, hm, mo, *params)


def _rows_copy(src_ref, src_row, dst_ref, dst_row, n_rows, sem):
    src_row, dst_row = [r if isinstance(r, int) else pl.multiple_of(r, SUBLANES) for r in (src_row, dst_row)]
    return pltpu.make_async_copy(src_ref.at[pl.ds(src_row, n_rows), :], dst_ref.at[pl.ds(dst_row, n_rows), :], sem)


def _stage_rows(ts):
    n = TOP_K * ts + N_EXPERTS * (RUN_PIECE - 1)
    return -(-n // SUBLANES) * SUBLANES


def _for_each_piece(n_pieces, body):
    lax.fori_loop(0, n_pieces, lambda j, c: (body(j), c)[1], 0)


def _run_copies(tab_ref, tile, tile_ref, tile_is_src, hbm_ref, sem, wait):
    for e in range(N_EXPERTS):
        n_rows = pl.multiple_of(tab_ref[tile, 0, e], SUBLANES)

        @pl.when(n_rows > 0)
        def _(e=e, n_rows=n_rows):
            in_tile = (tile_ref, tab_ref[tile, 0, N_EXPERTS + e])
            in_hbm = (hbm_ref, tab_ref[tile, 0, 2 * N_EXPERTS + e])
            (src_ref, src_row), (dst_ref, dst_row) = (in_tile, in_hbm) if tile_is_src else (in_hbm, in_tile)
            cp = _rows_copy(src_ref, src_row, dst_ref, dst_row, n_rows, sem)
            cp.wait() if wait else cp.start()


def _dispatch_kernel(tab_ref, prev_tab_ref, fill_ref, route_ref, gate_ref, h_ref, buf_ref,
                     sorted_scr, zero_scr, sems):
    step = pl.program_id(0)
    last = pl.num_programs(0) - 1
    slot = step % 2
    ts = PREP_ROWS
    tiles = range(MOVE_TILES)
    n_sorted = sorted_scr.shape[2]

    @pl.when(step == 0)
    def _():
        zero_scr[...] = jnp.zeros(zero_scr.shape, F32)

        def fill(wait):
            for e in range(N_EXPERTS):
                def piece(j, e=e):
                    cp = _rows_copy(zero_scr, 0, buf_ref, fill_ref[0, e] + j * FILL_PIECE, FILL_PIECE, sems.at[0])
                    cp.wait() if wait else cp.start()
                _for_each_piece(fill_ref[0, N_EXPERTS + e], piece)

            def tail(j):
                cp = _rows_copy(zero_scr, 0, buf_ref, fill_ref[0, 2 * N_EXPERTS] + j * EXPERT_ROWS,
                                EXPERT_ROWS, sems.at[0])
                cp.wait() if wait else cp.start()
            _for_each_piece(fill_ref[0, 2 * N_EXPERTS + 1], tail)

        fill(False)
        fill(True)

    pos = lax.broadcasted_iota(I32, (n_sorted, ts), 0)
    hots, weights = [], []
    for t in tiles:
        cols = pl.ds(t * ts, ts)
        hot = None
        weight = None
        for kslot in range(TOP_K):
            eq = pos == route_ref[kslot:kslot + 1, cols]
            w = jnp.where(eq, gate_ref[kslot:kslot + 1, cols], 0.0)
            hot = eq if hot is None else (hot | eq)
            weight = w if weight is None else weight + w
        hots.append(hot)
        weights.append(weight)
    rows_sorted = [jnp.dot(hots[t].astype(BF16), h_ref[pl.ds(t * ts, ts), :].astype(BF16),
                           preferred_element_type=F32) for t in tiles]
    ones = jnp.ones((ts, LANES), BF16)
    gate_sorted = [_dot_lhs2(weights[t], ones) for t in tiles]
    for t in tiles:
        sorted_scr[slot, t, :, 0:D_MODEL] = rows_sorted[t]
        sorted_scr[slot, t, :, D_MODEL:BUF_WIDTH] = gate_sorted[t]

    def copies(tab, which, wait):
        for t in tiles:
            _run_copies(tab, t, sorted_scr.at[which, t], True, buf_ref, sems.at[which], wait)

    @pl.when(step > 0)
    def _():
        copies(prev_tab_ref, 1 - slot, True)

    copies(tab_ref, slot, False)

    @pl.when(step == last)
    def _():
        copies(tab_ref, slot, True)


def _dispatch_call(tab_d, fill_tab, route8, gate8, h1, n_rows):
    T = h1.shape[0]
    ts = MOVE_TILES * PREP_ROWS
    smem_tile = pl.BlockSpec((MOVE_TILES, 1, LANES), lambda i: (i, 0, 0), memory_space=pltpu.SMEM)
    smem_prev = pl.BlockSpec((MOVE_TILES, 1, LANES), lambda i: (jnp.maximum(i - 1, 0), 0, 0),
                             memory_space=pltpu.SMEM)
    return pl.pallas_call(
        _dispatch_kernel, grid=(T // ts,),
        in_specs=[smem_tile, smem_prev,
                  pl.BlockSpec((1, LANES), lambda i: (0, 0), memory_space=pltpu.SMEM),
                  pl.BlockSpec((8, ts), lambda i: (0, i)),
                  pl.BlockSpec((8, ts), lambda i: (0, i)),
                  pl.BlockSpec((ts, D_MODEL), lambda i: (i, 0))],
        out_specs=pl.BlockSpec(memory_space=pl.ANY),
        out_shape=jax.ShapeDtypeStruct((n_rows, BUF_WIDTH), F32),
        scratch_shapes=[pltpu.VMEM((2, MOVE_TILES, _stage_rows(PREP_ROWS), BUF_WIDTH), F32),
                        pltpu.VMEM((EXPERT_ROWS, BUF_WIDTH), F32),
                        pltpu.SemaphoreType.DMA((2,))],
        compiler_params=pltpu.CompilerParams(dimension_semantics=("arbitrary",),
                                             vmem_limit_bytes=VMEM_LIMIT),
        name="dispatch",
    )(tab_d, tab_d, fill_tab, route8, gate8, h1)


def _expert_kernel(be_ref, nb_ref, run_ref, next_ref, x_ref, wgu_hbm, bgu_ref, wdn_hbm, bdn_ref, o_ref,
                   wgu_f32, wdn_f32, wgu_scr, wdn_scr, sems, *, layer):
    j = pl.program_id(0)
    used = j < nb_ref[0]
    changed = jnp.logical_or(j == 0, be_ref[j] != be_ref[jnp.maximum(j - 1, 0)])

    def fetch(expert, slot):
        return (pltpu.make_async_copy(wgu_hbm.at[layer, expert], wgu_f32.at[slot], sems.at[0, slot]),
                pltpu.make_async_copy(wdn_hbm.at[layer, expert], wdn_f32.at[slot], sems.at[1, slot]))

    @pl.when(j == 0)
    def _():
        for cp in fetch(be_ref[0], 0):
            cp.start()

    @pl.when(jnp.logical_and(used, changed))
    def _():
        slot = run_ref[j] % 2
        for cp in fetch(be_ref[j], slot):
            cp.wait()
        wgu_scr[...] = wgu_f32[slot].astype(BF16)
        wdn_scr[...] = wdn_f32[slot].astype(BF16)

        @pl.when(next_ref[j] >= 0)
        def _():
            for cp in fetch(next_ref[j], 1 - slot):
                cp.start()

    @pl.when(used)
    def _():
        gu = jnp.dot(x_ref[:, 0:D_MODEL].astype(BF16), wgu_scr[...], preferred_element_type=F32) + bgu_ref[...]
        glu = jnp.minimum(gu[:, 0:D_FF], SWIGLU_LIMIT)
        lin = jnp.clip(gu[:, D_FF:], -SWIGLU_LIMIT, SWIGLU_LIMIT)
        act = glu * _sigmoid(SWIGLU_ALPHA * glu) * (lin + 1.0)
        y = jnp.dot(act.astype(BF16), wdn_scr[...], preferred_element_type=F32) + bdn_ref[...]
        gate = x_ref[:, D_MODEL:BUF_WIDTH]
        o_ref[...] = y * jnp.concatenate([gate] * (D_MODEL // LANES), axis=1)

    @pl.when(jnp.logical_not(used))
    def _():
        o_ref[...] = jnp.zeros(o_ref.shape, F32)


def _expert_call(block_e, n_used, buf, lw):
    n_rows = buf.shape[0]
    rows = EXPERT_ROWS
    n_blocks = n_rows // rows
    j = jnp.arange(n_blocks, dtype=I32)
    first = (j < n_used[0]) & ((j == 0) | (block_e != jnp.roll(block_e, 1)))
    run_idx = jnp.cumsum(first.astype(I32)) - 1
    later_first = lax.cummin(jnp.where(first, j, n_blocks)[::-1])[::-1]
    next_start = jnp.concatenate([later_first[1:], jnp.full((1,), n_blocks, I32)])
    next_e = jnp.where(next_start < n_blocks, block_e[jnp.minimum(next_start, n_blocks - 1)], -1).astype(I32)

    def blk(j, be, nb, run, nxt):
        return jnp.minimum(j, nb[0] - 1)

    layer = lw['layer']
    grid_spec = pltpu.PrefetchScalarGridSpec(
        num_scalar_prefetch=4, grid=(n_blocks,),
        in_specs=[pl.BlockSpec((rows, BUF_WIDTH), lambda j, *s: (blk(j, *s), 0)),
                  pl.BlockSpec(memory_space=pl.ANY),
                  pl.BlockSpec((None, None, 1, 2 * D_FF), lambda j, be, *s: (layer, be[j], 0, 0)),
                  pl.BlockSpec(memory_space=pl.ANY),
                  pl.BlockSpec((None, None, 1, D_MODEL), lambda j, be, *s: (layer, be[j], 0, 0))],
        out_specs=pl.BlockSpec((rows, D_MODEL), lambda j, *s: (j, 0)),
        scratch_shapes=[pltpu.VMEM((2, D_MODEL, 2 * D_FF), F32), pltpu.VMEM((2, D_FF, D_MODEL), F32),
                        pltpu.VMEM((D_MODEL, 2 * D_FF), BF16), pltpu.VMEM((D_FF, D_MODEL), BF16),
                        pltpu.SemaphoreType.DMA((2, 2))])
    return pl.pallas_call(
        functools.partial(_expert_kernel, layer=layer), grid_spec=grid_spec,
        out_shape=jax.ShapeDtypeStruct((n_rows, D_MODEL), F32),
        compiler_params=pltpu.CompilerParams(dimension_semantics=("arbitrary",),
                                             vmem_limit_bytes=56 * 1024 * 1024),
        name="expert",
    )(block_e, n_used, run_idx, next_e, buf, lw['w_gate_up'], lw['b_gate_up'], lw['w_down'], lw['b_down'])


def _combine_kernel(tab_ref, next_tab_ref, route_ref, h_ref, out_hbm_ref, ln2g_ref, ln2b_ref, o_ref,
                    stage_scr, sems):
    step = pl.program_id(0)
    last = pl.num_programs(0) - 1
    slot = step % 2
    ts = PREP_ROWS
    tiles = range(MOVE_TILES)
    n_stage = stage_scr.shape[2]

    def copies(tab, which, wait):
        for t in tiles:
            _run_copies(tab, t, stage_scr.at[which, t], False, out_hbm_ref, sems.at[which], wait)

    @pl.when(step == 0)
    def _():
        stage_scr[...] = jnp.zeros(stage_scr.shape, F32)
        copies(tab_ref, 0, False)

    @pl.when(step < last)
    def _():
        copies(next_tab_ref, 1 - slot, False)

    lane = lax.broadcasted_iota(I32, (ts, n_stage), 1).astype(F32)
    sels = []
    for t in tiles:
        slot_f = route_ref[:, pl.ds(t * ts, ts)].astype(F32)
        slot_cols = jnp.concatenate([slot_f, jnp.zeros((LANES - 8, ts), F32)], axis=0).T
        sel = None
        for kslot in range(TOP_K):
            eq = lane == slot_cols[:, kslot:kslot + 1]
            sel = eq if sel is None else (sel | eq)
        sels.append(sel.astype(BF16))
    copies(tab_ref, slot, True)
    ys = [jnp.dot(sels[t], stage_scr[slot, t].astype(BF16), preferred_element_type=F32) for t in tiles]
    for t in tiles:
        rows = pl.ds(t * ts, ts)
        o_ref[rows, :] = _layer_norm(DEEPNORM_ALPHA * h_ref[rows, :] + ys[t], ln2g_ref[...], ln2b_ref[...])


def _combine_call(tab_c, route8, h1, out_rows, lw):
    T = h1.shape[0]
    ts = MOVE_TILES * PREP_ROWS
    n_tiles = T // ts
    full = lambda a: pl.BlockSpec(a.shape, lambda i: (0,) * a.ndim)
    n_stage = _stage_rows(PREP_ROWS)
    return pl.pallas_call(
        _combine_kernel, grid=(n_tiles,),
        in_specs=[pl.BlockSpec((MOVE_TILES, 1, LANES), lambda i: (i, 0, 0), memory_space=pltpu.SMEM),
                  pl.BlockSpec((MOVE_TILES, 1, LANES), lambda i: (jnp.minimum(i + 1, n_tiles - 1), 0, 0),
                               memory_space=pltpu.SMEM),
                  pl.BlockSpec((8, ts), lambda i: (0, i)),
                  pl.BlockSpec((ts, D_MODEL), lambda i: (i, 0)),
                  pl.BlockSpec(memory_space=pl.ANY),
                  full(lw['ln2_g']), full(lw['ln2_b'])],
        out_specs=pl.BlockSpec((ts, D_MODEL), lambda i: (i, 0)),
        out_shape=jax.ShapeDtypeStruct((T, D_MODEL), F32),
        scratch_shapes=[pltpu.VMEM((2, MOVE_TILES, n_stage, D_MODEL), F32), pltpu.SemaphoreType.DMA((2,))],
        compiler_params=pltpu.CompilerParams(dimension_semantics=("arbitrary",),
                                             vmem_limit_bytes=VMEM_LIMIT),
        name="combine",
    )(tab_c, tab_c, route8, h1, out_rows, lw['ln2_g'], lw['ln2_b'])


def _block_diag_ones(width):
    hid = jnp.arange(width) // HEAD_DIM
    return (hid[:, None] == hid[None, :]).astype(BF16)


def _layer_params(l, w_in, pool_w, pool_scale, rwkv_mu, rwkv_w0, rwkv_w2, rwkv_a0, rwkv_a2, rwkv_g2,
                  rwkv_kk_scale, rwkv_ka, rwkv_rk, rwkv_lnx_g, rwkv_lnx_b, rwkv_v0, rwkv_v1, rwkv_v2,
                  mlstm_conv_w, mlstm_conv_b, mlstm_b_i, mlstm_b_f, mlstm_norm_g, w_out, ln1_g, ln1_b,
                  router_w, router_b, w_gate_up, b_gate_up, w_down, b_down, ln2_g, ln2_b):
    row = lambda a: a.reshape(1, -1).astype(F32)
    pad_cols = D_IN_PAD - D_IN
    if l > 0:
        extra = jnp.concatenate([rwkv_v1[l - 1], jnp.zeros((D_MODEL, pad_cols - VRES_LORA), F32)], axis=1)
        v0 = row(rwkv_v0[l - 1])
        v2 = jnp.zeros((LANES, RWKV_WIDTH), F32).at[VRES_OFF:VRES_OFF + VRES_LORA].set(rwkv_v2[l - 1])
    else:
        extra = jnp.zeros((D_MODEL, pad_cols), F32)
        v0 = jnp.zeros((1, RWKV_WIDTH), F32)
        v2 = jnp.zeros((LANES, RWKV_WIDTH), F32)
    pw = jnp.zeros((POOL_WIDTH, POOL_WIDTH), F32)
    for gi in range(len(POOL_WINDOWS)):
        sl = slice(gi * POOL_GROUP, (gi + 1) * POOL_GROUP)
        pw = pw.at[sl, sl].set(pool_w[l, gi])
    zero_lora = jnp.zeros((DECAY_LORA, RWKV_WIDTH), F32)
    gate_bias = jnp.zeros((1, LANES), F32).at[0, 0:MLSTM_HEADS].set(mlstm_b_i[l])
    gate_bias = gate_bias.at[0, MLSTM_HEADS:2 * MLSTM_HEADS].set(mlstm_b_f[l])
    return {
        'w_in': jnp.concatenate([w_in[l], extra], axis=1).astype(BF16),
        'pool_w': pw.astype(BF16), 'pool_scale': row(pool_scale[l]), 'mu': row(rwkv_mu[l]),
        'w0': row(rwkv_w0[l]), 'w2': jnp.concatenate([rwkv_w2[l], zero_lora], axis=0).astype(BF16),
        'a0': row(rwkv_a0[l]), 'a2': jnp.concatenate([zero_lora, rwkv_a2[l]], axis=0).astype(BF16),
        'g2': rwkv_g2[l].astype(BF16), 'kk_scale': row(rwkv_kk_scale[l]), 'ka': row(rwkv_ka[l]),
        'rk': row(rwkv_rk[l]), 'v0': v0, 'v2': v2.astype(BF16),
        'conv_w': mlstm_conv_w[l], 'conv_b': row(mlstm_conv_b[l]), 'gate_bias': gate_bias,
        'ones_bd': _block_diag_ones(RWKV_WIDTH),
        'lnx_g': row(rwkv_lnx_g[l]), 'lnx_b': row(rwkv_lnx_b[l]), 'norm_g': row(mlstm_norm_g[l]),
        'w_out': w_out[l].astype(BF16), 'ln1_g': row(ln1_g[l]), 'ln1_b': row(ln1_b[l]),
        'router_wt': router_w[l].T, 'router_b': router_b[l].reshape(N_EXPERTS, 1),
        'layer': l, 'w_gate_up': w_gate_up, 'b_gate_up': b_gate_up.reshape(-1, N_EXPERTS, 1, 2 * D_FF),
        'w_down': w_down, 'b_down': b_down.reshape(-1, N_EXPERTS, 1, D_MODEL),
        'ln2_g': row(ln2_g[l]), 'ln2_b': row(ln2_b[l]),
    }


def _layer(h, v_first, lw, *, has_vres):
    B, S, _ = h.shape
    T = B * S
    outs = _prep_call(h, lw, v_first, has_vres=has_vres)
    ypool, r, ld, k, v, kk, b, g, bonus, mq, mk, mv, mo, mg = outs
    yr = _rwkv_call(r, ld, k, v, kk, b)
    hm = _mlstm_call(mq, mk, mv, mg)
    flat = lambda a: a.reshape(T, a.shape[-1])
    h_flat = flat(h)
    h1, gate8, route8, cnt = _post_call(h_flat, flat(ypool), flat(yr), flat(bonus), flat(g), flat(hm),
                                        flat(mo), lw)
    n_tiles = T // PREP_ROWS
    cnt = cnt.reshape(n_tiles, N_EXPERTS, LANES)[:, :, 0].astype(I32)
    rows = EXPERT_ROWS
    run_len = (cnt + RUN_PIECE - 1) // RUN_PIECE * RUN_PIECE
    region = jnp.sum(run_len, axis=0)
    padded = (region + rows - 1) // rows * rows
    pad_end = jnp.cumsum(padded)
    pad_start = pad_end - padded
    n_blocks = -(-(T * TOP_K + n_tiles * N_EXPERTS * (RUN_PIECE - 1)) // rows) + N_EXPERTS
    n_used = (pad_end[-1] // rows).astype(I32).reshape(1)
    starts = jnp.minimum(jnp.arange(n_blocks, dtype=I32), n_used[0] - 1) * rows
    block_e = jnp.sum((pad_end[None, :] <= starts[:, None]).astype(I32), axis=1)
    run_src = jnp.cumsum(run_len, axis=1) - run_len
    run_dst = pad_start[None, :] + jnp.cumsum(run_len, axis=0) - run_len
    run_tab = jnp.concatenate([run_len, run_src, run_dst,
                               jnp.zeros((n_tiles, LANES - 3 * N_EXPERTS), I32)], axis=1).reshape(n_tiles, 1, LANES)
    fill_start = (pad_start + region) // FILL_PIECE * FILL_PIECE
    fill_tab = jnp.concatenate([fill_start, (pad_end - fill_start) // FILL_PIECE,
                                pad_end[-1:], n_blocks - n_used,
                                jnp.zeros((LANES - 2 * N_EXPERTS - 2,), I32)]).reshape(1, LANES)
    buf = _dispatch_call(run_tab, fill_tab, route8, gate8, h1, n_blocks * rows)
    out_rows = _expert_call(block_e, n_used, buf, lw)
    h2 = _combine_call(run_tab, route8, h1, out_rows, lw)
    return h2.reshape(B, S, D_MODEL), v


def kernel(x, w_in, pool_w, pool_scale, rwkv_mu, rwkv_w0, rwkv_w2, rwkv_a0, rwkv_a2, rwkv_g2, rwkv_kk_scale, rwkv_ka, rwkv_rk, rwkv_lnx_g, rwkv_lnx_b, rwkv_v0, rwkv_v1, rwkv_v2, mlstm_conv_w, mlstm_conv_b, mlstm_b_i, mlstm_b_f, mlstm_norm_g, w_out, ln1_g, ln1_b, router_w, router_b, w_gate_up, b_gate_up, w_down, b_down, ln2_g, ln2_b):
    weights = (w_in, pool_w, pool_scale, rwkv_mu, rwkv_w0, rwkv_w2, rwkv_a0, rwkv_a2, rwkv_g2, rwkv_kk_scale,
               rwkv_ka, rwkv_rk, rwkv_lnx_g, rwkv_lnx_b, rwkv_v0, rwkv_v1, rwkv_v2, mlstm_conv_w, mlstm_conv_b,
               mlstm_b_i, mlstm_b_f, mlstm_norm_g, w_out, ln1_g, ln1_b, router_w, router_b, w_gate_up,
               b_gate_up, w_down, b_down, ln2_g, ln2_b)
    h = x
    v_first = jnp.zeros(x.shape[:2] + (RWKV_WIDTH,), BF16)
    for l in range(w_in.shape[0]):
        lw = _layer_params(l, *weights)
        h, v_l = _layer(h, v_first, lw, has_vres=l > 0)
        if l == 0:
            v_first = v_l
    return h
```

```python
import functools

import jax
import jax.numpy as jnp
from jax import lax
from jax.experimental import pallas as pl
from jax.experimental.pallas import tpu as pltpu

F32 = jnp.float32
BF16 = jnp.bfloat16
I32 = jnp.int32

D_MODEL = 1024
HEAD_DIM = 64
POOL_WINDOWS = (2, 4, 8, 16)
POOL_WIDTH = 256
POOL_GROUP = 64
RWKV_WIDTH = 384
RWKV_HEADS = 6
DECAY_LORA = 64
ICLR_LORA = 64
GATE_LORA = 128
VRES_LORA = 32
RWKV_GN_EPS = 64e-5
RWKV_COLS = 3 * RWKV_WIDTH + DECAY_LORA + ICLR_LORA + GATE_LORA
MLSTM_WIDTH = 384
MLSTM_HEADS = 6
MLSTM_CONV = 4
MLSTM_COLS = 4 * MLSTM_WIDTH + 2 * MLSTM_HEADS
D_IN = POOL_WIDTH + RWKV_COLS + MLSTM_COLS
N_EXPERTS = 32
TOP_K = 4
D_FF = D_MODEL
SWIGLU_LIMIT = 7.0
SWIGLU_ALPHA = 1.702
LN_EPS = 1e-5
DEPTH = 2
DEEPNORM_ALPHA = (2 * DEPTH) ** 0.25

LANES = 128
D_IN_PAD = 3328
RWKV_OFF = POOL_WIDTH
MLSTM_OFF = POOL_WIDTH + RWKV_COLS
GATE_OFF = MLSTM_OFF + 4 * MLSTM_WIDTH
VRES_OFF = 2 * MLSTM_HEADS
HALO = 16
CHUNK = 64

PROJ_ROWS = 512
PREP_ROWS = 256
POST_TILES = 4
MOVE_TILES = 2
SEQ_ROWS = 512
EXPERT_ROWS = 512
BUF_WIDTH = D_MODEL + LANES
SUBLANES = 8
RUN_PIECE = SUBLANES
FILL_PIECE = 64
VMEM_LIMIT = 48 * 1024 * 1024


def _dot(a, b):
    return jnp.dot(a.astype(BF16), b.astype(BF16), preferred_element_type=F32)


def _dot_nt(a, b):
    return lax.dot_general(a.astype(BF16), b.astype(BF16), (((1,), (1,)), ((), ())),
                           preferred_element_type=F32)


def _dot_tn(a, b):
    return lax.dot_general(a.astype(BF16), b.astype(BF16), (((0,), (0,)), ((), ())),
                           preferred_element_type=F32)


def _split(x):
    hi = x.astype(BF16)
    lo = (x - hi.astype(F32)).astype(BF16)
    return hi, lo


def _dot_lhs2(a, b_bf16):
    hi, lo = _split(a)
    return (jnp.dot(hi, b_bf16, preferred_element_type=F32)
            + jnp.dot(lo, b_bf16, preferred_element_type=F32))


def _dot_rhs2(a_bf16, b):
    hi, lo = _split(b)
    return (jnp.dot(a_bf16, hi, preferred_element_type=F32)
            + jnp.dot(a_bf16, lo, preferred_element_type=F32))


def _sigmoid(x):
    return 1.0 / (1.0 + jnp.exp(-x))


def _softplus(x):
    return jnp.maximum(x, 0.0) + jnp.log(1.0 + jnp.exp(-jnp.abs(x)))


def _head_norm(y, ones_bd, eps):
    inv = 1.0 / HEAD_DIM
    mean = _dot(y, ones_bd) * inv
    d = y - mean
    var = _dot(d * d, ones_bd) * inv
    return d * lax.rsqrt(var + eps)


def _prep_kernel(x_ref, w_ref, poolw_ref, pscale_ref, mu_ref, w0_ref, w2_ref, a0_ref, a2_ref, g2_ref,
                 kks_ref, ka_ref, rk_ref, v0_ref, v2_ref, vfirst_ref, cw_ref, cb_ref, gbias_ref, ones_ref,
                 ypool_ref, r_ref, ld_ref, k_ref, v_ref, kk_ref, b_ref, g_ref, bonus_ref,
                 mq_ref, mk_ref, mv_ref, mo_ref, mg_ref,
                 p_scr, *, has_vres):
    i = pl.program_id(1)
    ts = x_ref.shape[0]

    @pl.when(i == 0)
    def _():
        p_scr[0:HALO, :] = jnp.zeros((HALO, D_IN_PAD), F32)

    @pl.when(i > 0)
    def _():
        p_scr[0:HALO, :] = p_scr[ts:ts + HALO, :]

    p_scr[HALO:HALO + ts, :] = jnp.dot(x_ref[...].astype(BF16), w_ref[...], preferred_element_type=F32)

    def rows(shift, c0, c1):
        return p_scr[HALO - shift:HALO - shift + ts, c0:c1]

    u = rows(0, 0, POOL_WIDTH)
    acc = u
    sums = {}
    for s in range(1, POOL_WINDOWS[-1]):
        acc = acc + rows(s, 0, POOL_WIDTH)
        if s + 1 in POOL_WINDOWS:
            sums[s + 1] = acc
    pos = (i * ts + lax.broadcasted_iota(I32, (ts, 1), 0) + 1).astype(F32)
    lane = lax.broadcasted_iota(I32, (ts, POOL_WIDTH), 1)
    d = None
    for gi, win in reversed(list(enumerate(POOL_WINDOWS))):
        dg = sums[win] / jnp.minimum(pos, float(win))
        d = dg if d is None else jnp.where(lane < (gi + 1) * POOL_GROUP, dg, d)
    d = d - u
    ypool_ref[...] = _dot(d, poolw_ref[...]) * pscale_ref[...]

    cur = rows(0, RWKV_OFF, RWKV_OFF + RWKV_COLS)
    prev = rows(1, RWKV_OFF, RWKV_OFF + RWKV_COLS)
    pf = cur + mu_ref[...] * (prev - cur)
    W = RWKV_WIDTH
    r = pf[:, 0:W]
    k = pf[:, W:2 * W]
    v = pf[:, 2 * W:3 * W]
    z = pf[:, 3 * W:3 * W + LANES]
    gd = pf[:, 3 * W + LANES:3 * W + 2 * LANES]
    w_log = -_softplus(-(w0_ref[...] + _dot(jnp.tanh(z), w2_ref[...]))) - 0.5
    ld_ref[...] = -jnp.exp(w_log)
    a = _sigmoid(a0_ref[...] + _dot(z, a2_ref[...]))
    g_ref[...] = _dot(_sigmoid(gd), g2_ref[...])
    gates = rows(0, GATE_OFF, GATE_OFF + LANES)
    if has_vres:
        v_gate = _sigmoid(v0_ref[...] + _dot(gates, v2_ref[...]))
        v = v + (vfirst_ref[...] - v) * v_gate
    ones_bd = ones_ref[...]
    kk = k * kks_ref[...]
    kk = kk / jnp.maximum(jnp.sqrt(_dot(kk * kk, ones_bd)), 1e-12)
    k = k * (1.0 + (a - 1.0) * ka_ref[...])
    r_ref[...] = r
    k_ref[...] = k
    v_ref[...] = v
    kk_ref[...] = kk
    b_ref[...] = kk * a
    bonus_ref[...] = _dot(r * k * rk_ref[...], ones_bd) * v

    qk = cb_ref[...] + rows(0, MLSTM_OFF, MLSTM_OFF + 2 * MLSTM_WIDTH) * cw_ref[MLSTM_CONV - 1:MLSTM_CONV, :]
    for tap in range(MLSTM_CONV - 1):
        shift = MLSTM_CONV - 1 - tap
        qk = qk + rows(shift, MLSTM_OFF, MLSTM_OFF + 2 * MLSTM_WIDTH) * cw_ref[tap:tap + 1, :]
    qk = qk * _sigmoid(qk)
    mq_ref[...] = qk[:, 0:MLSTM_WIDTH] * (HEAD_DIM ** -0.5)
    mk_ref[...] = qk[:, MLSTM_WIDTH:]
    mv_ref[...] = rows(0, MLSTM_OFF + 2 * MLSTM_WIDTH, MLSTM_OFF + 3 * MLSTM_WIDTH)
    mo_ref[...] = _sigmoid(rows(0, MLSTM_OFF + 3 * MLSTM_WIDTH, MLSTM_OFF + 4 * MLSTM_WIDTH))
    gb = gates + gbias_ref[...]
    glane = lax.broadcasted_iota(I32, (ts, LANES), 1)
    mg_ref[...] = jnp.where(glane < MLSTM_HEADS, gb, -_softplus(-gb))


def _prep_call(x, lw, vfirst, *, has_vres):
    B, S, _ = x.shape
    ts = PROJ_ROWS
    grid = (B, S // ts)
    row3 = lambda c: pl.BlockSpec((None, ts, c), lambda b, i: (b, i, 0))
    full = lambda a: pl.BlockSpec(a.shape, lambda b, i: (0,) * a.ndim)
    params = [lw['w_in'], lw['pool_w'], lw['pool_scale'], lw['mu'], lw['w0'], lw['w2'], lw['a0'], lw['a2'],
              lw['g2'], lw['kk_scale'], lw['ka'], lw['rk'], lw['v0'], lw['v2']]
    tail = [lw['conv_w'], lw['conv_b'], lw['gate_bias'], lw['ones_bd']]
    in_specs = ([row3(D_MODEL)] + [full(a) for a in params] + [row3(RWKV_WIDTH)] + [full(a) for a in tail])
    widths = [POOL_WIDTH] + [RWKV_WIDTH] * 8 + [MLSTM_WIDTH] * 4 + [LANES]
    out_shape = [jax.ShapeDtypeStruct((B, S, c), F32) for c in widths]
    out_specs = [row3(c) for c in widths]
    return pl.pallas_call(
        functools.partial(_prep_kernel, has_vres=has_vres),
        grid=grid, in_specs=in_specs, out_specs=out_specs, out_shape=out_shape,
        scratch_shapes=[pltpu.VMEM((HALO + ts, D_IN_PAD), F32)],
        compiler_params=pltpu.CompilerParams(dimension_semantics=("arbitrary", "arbitrary"),
                                             vmem_limit_bytes=VMEM_LIMIT),
        name="prep",
    )(x, *params, vfirst, *tail)


def _rwkv_kernel(r_ref, ld_ref, k_ref, v_ref, kk_ref, b_ref, y_ref, s_scr):
    c = pl.program_id(1)
    L = CHUNK
    n_chunks = r_ref.shape[0] // L

    @pl.when(c == 0)
    def _():
        s_scr[...] = jnp.zeros(s_scr.shape, F32)

    row = lax.broadcasted_iota(I32, (L, L), 0)
    col = lax.broadcasted_iota(I32, (L, L), 1)
    tri = (col <= row).astype(BF16)
    row_l = lax.broadcasted_iota(I32, (L, LANES), 0)
    col_l = lax.broadcasted_iota(I32, (L, LANES), 1) % HEAD_DIM
    strict = col_l < row_l
    incl = col_l <= row_l
    eye = (col_l == row_l).astype(F32)
    r2 = lax.broadcasted_iota(I32, (LANES, LANES), 0)
    c2 = lax.broadcasted_iota(I32, (LANES, LANES), 1)
    same_head = (r2 // HEAD_DIM) == (c2 // HEAD_DIM)
    eye2 = (r2 == c2).astype(F32)
    low_lanes = lax.broadcasted_iota(I32, (L, LANES), 1) < HEAD_DIM
    n_pairs = RWKV_HEADS // 2

    def block_diag(x):
        xb = x.astype(BF16)
        return jnp.where(same_head, jnp.concatenate([xb, xb], axis=0), jnp.zeros((), BF16))

    def diag_blocks(z):
        return jnp.where(low_lanes, z[0:L, :], z[L:2 * L, :])

    units = [(ci, p) for ci in range(n_chunks) for p in range(n_pairs)]
    per_chunk = []
    for ci in range(n_chunks):
        sl = pl.ds(ci * L, L)
        ld = ld_ref[sl, :]
        cin = _dot_rhs2(tri, ld)
        c_last = cin[L - 1:L, :]
        e_neg = jnp.exp(-cin)
        e_tail = jnp.exp(c_last - cin)
        kk = kk_ref[sl, :]
        bb = b_ref[sl, :]
        kx = k_ref[sl, :]
        per_chunk.append(dict(
            A=-kk * jnp.exp(cin - ld), R=r_ref[sl, :] * jnp.exp(cin), B=bb * e_neg, K=kx * e_neg,
            Bh=bb * e_tail, Kh=kx * e_tail, V=v_ref[sl, :], g_last=jnp.exp(c_last)))

    def part(name, u):
        ci, p = u
        return per_chunk[ci][name][:, p * LANES:(p + 1) * LANES]

    M = [_dot_nt(jnp.concatenate([part('A', u), part('R', u)], axis=0),
                 jnp.concatenate([block_diag(part('B', u)), block_diag(part('K', u))], axis=0)) for u in units]
    m_ab = [jnp.where(strict, m[0:L, 0:LANES], 0.0) for m in M]
    m_ak = [jnp.where(strict, m[0:L, LANES:], 0.0) for m in M]
    m_rb = [jnp.where(incl, m[L:, 0:LANES], 0.0) for m in M]
    m_rk = [jnp.where(incl, m[L:, LANES:], 0.0) for m in M]
    MVYK = [_dot(jnp.concatenate([ak, rk], axis=0), block_diag(part('V', u)))
            for ak, rk, u in zip(m_ak, m_rk, units)]
    MV = [x[0:L, :] for x in MVYK]
    YK = [x[L:, :] for x in MVYK]
    T = [eye + m for m in m_ab]
    pw = [_dot(m, block_diag(m)) for m in m_ab]
    for _ in range(4):
        both = [_dot(jnp.concatenate([t, p], axis=0), block_diag(p)) for t, p in zip(T, pw)]
        T = [t + x[0:L, :] for t, x in zip(T, both)]
        pw = [x[L:, :] for x in both]
    T = [t + _dot(t, block_diag(p)) for t, p in zip(T, pw)]
    WU = [_dot(t, jnp.concatenate([block_diag(part('A', u)), block_diag(mv)], axis=1))
          for t, u, mv in zip(T, units, MV)]
    GY = [_dot(m, jnp.concatenate([block_diag(wu[:, 0:LANES]), block_diag(wu[:, LANES:])], axis=1))
          for m, wu in zip(m_rb, WU)]
    G = [part('R', u) + gy[:, 0:LANES] for u, gy in zip(units, GY)]
    Y0 = [gy[:, LANES:] + yk for gy, yk in zip(GY, YK)]
    P = [jnp.where(same_head, _dot_tn(wu[:, 0:LANES], part('Bh', u)), 0.0) + eye2 * part('g_last', u)
         for wu, u in zip(WU, units)]
    Q = [diag_blocks(_dot_tn(jnp.concatenate([wu[:, LANES:], part('V', u)], axis=0),
                             jnp.concatenate([part('Bh', u), part('Kh', u)], axis=0)))
         for wu, u in zip(WU, units)]

    state = [s_scr[p] for p in range(n_pairs)]
    for ci in range(n_chunks):
        base = ci * n_pairs
        ys = [_dot_nt(G[base + p], block_diag(state[p])) + Y0[base + p] for p in range(n_pairs)]
        state = [_dot(state[p], P[base + p]) + Q[base + p] for p in range(n_pairs)]
        y_ref[pl.ds(ci * L, L), :] = jnp.concatenate(ys, axis=1)
    for p in range(n_pairs):
        s_scr[p] = state[p]


def _rwkv_call(r, ld, k, v, kk, b):
    B, S, W = r.shape
    ts = SEQ_ROWS
    spec = pl.BlockSpec((None, ts, W), lambda bi, c: (bi, c, 0))
    return pl.pallas_call(
        _rwkv_kernel, grid=(B, S // ts), in_specs=[spec] * 6, out_specs=spec,
        out_shape=jax.ShapeDtypeStruct((B, S, W), F32),
        scratch_shapes=[pltpu.VMEM((RWKV_HEADS // 2, HEAD_DIM, LANES), F32)],
        compiler_params=pltpu.CompilerParams(dimension_semantics=("arbitrary", "arbitrary"),
                                             vmem_limit_bytes=VMEM_LIMIT),
        name="rwkv",
    )(r, ld, k, v, kk, b)


def _mlstm_kernel(q_ref, k_ref, v_ref, g_ref, expand_ref, h_ref, cn_scr, m_scr):
    c = pl.program_id(1)
    L = CHUNK
    n_chunks = q_ref.shape[0] // L
    H = MLSTM_HEADS
    n_pairs = H // 2

    @pl.when(c == 0)
    def _():
        cn_scr[...] = jnp.zeros(cn_scr.shape, F32)
        m_scr[...] = jnp.zeros(m_scr.shape, F32)

    row = lax.broadcasted_iota(I32, (L, L), 0)
    col = lax.broadcasted_iota(I32, (L, L), 1)
    tri = (col <= row).astype(BF16)
    row_l = lax.broadcasted_iota(I32, (L, LANES), 0)
    col_l = lax.broadcasted_iota(I32, (L, LANES), 1)
    incl2 = (col_l % HEAD_DIM) <= row_l
    r2 = lax.broadcasted_iota(I32, (LANES, LANES), 0)
    c2 = lax.broadcasted_iota(I32, (LANES, LANES), 1)
    same_head = (r2 // HEAD_DIM) == (c2 // HEAD_DIM)
    same_head2 = jnp.concatenate([same_head, same_head], axis=1)

    units = [(ci, p) for ci in range(n_chunks) for p in range(n_pairs)]
    ig_rep, g_rep, x_t = [], [], []
    for ci in range(n_chunks):
        gt = g_ref[pl.ds(ci * L, L), :]
        rep = _dot_lhs2(gt, expand_ref[...])
        ig_rep.append(rep[:, 0:MLSTM_WIDTH])
        g_rep.append(_dot_rhs2(tri, rep[:, MLSTM_WIDTH:]))
        x_t.append(gt.T[0:H, :] - _dot_rhs2(tri, gt).T[H:2 * H, :])

    def part(ref, u):
        ci, p = u
        return ref[pl.ds(ci * L, L), p * LANES:(p + 1) * LANES]

    def pair(x, p):
        return x[:, p * LANES:(p + 1) * LANES]

    def block_diag(x):
        return jnp.where(same_head, jnp.concatenate([x, x], axis=0), 0.0)

    ig_c = [pair(ig_rep[ci], p) for ci, p in units]
    g_c = [pair(g_rep[ci], p) for ci, p in units]
    g_last = [g[L - 1:L, :] for g in g_c]
    x_row = [jnp.concatenate([x_t[ci][2 * p:2 * p + 1, :], x_t[ci][2 * p + 1:2 * p + 2, :]], axis=1)
             for ci, p in units]
    d_log = [jnp.where(incl2, g + xr, -jnp.inf) for g, xr in zip(g_c, x_row)]
    x_run = [ig - g for ig, g in zip(ig_c, g_c)]
    shift = 1
    while shift < L:
        x_run = [jnp.maximum(x, jnp.where(row_l >= shift, pltpu.roll(x, shift, 0), -jnp.inf)) for x in x_run]
        shift *= 2
    d_max = [g + x for g, x in zip(g_c, x_run)]
    qk = [_dot_nt(part(q_ref, u), block_diag(part(k_ref, u))) for u in units]
    e = [gl - g + ig for gl, g, ig in zip(g_last, g_c, ig_c)]
    m_loc = [jnp.max(x, axis=0, keepdims=True) for x in e]
    wk = [part(k_ref, u) * jnp.exp(x - m) for u, x, m in zip(units, e, m_loc)]
    ones = jnp.ones((L, LANES), F32)
    kvn_loc = [jnp.where(same_head2, _dot_tn(w, jnp.concatenate([part(v_ref, u), ones], axis=1)), 0.0)
               for w, u in zip(wk, units)]

    cn_st = [cn_scr[p] for p in range(n_pairs)]
    m_st = [m_scr[p:p + 1, :] for p in range(n_pairs)]
    cn_prev, m_prev = [], []
    for i, (ci, p) in enumerate(units):
        cn_prev.append(cn_st[p])
        m_prev.append(m_st[p])
        m_new = jnp.maximum(g_last[i] + m_st[p], m_loc[i])
        a_old = jnp.exp(g_last[i] + m_st[p] - m_new)
        a_new = jnp.exp(m_loc[i] - m_new)
        cn_st[p] = (jnp.concatenate([a_old, a_old], axis=1) * cn_st[p]
                    + jnp.concatenate([a_new, a_new], axis=1) * kvn_loc[i])
        m_st[p] = m_new
    for p in range(n_pairs):
        cn_scr[p] = cn_st[p]
        m_scr[p:p + 1, :] = m_st[p]

    inter_log = [g + m for g, m in zip(g_c, m_prev)]
    m_j = [jnp.maximum(dm, il) for dm, il in zip(d_max, inter_log)]
    w_intra = [jnp.exp(d - m) * s for d, m, s in zip(d_log, m_j, qk)]
    w_inter = [jnp.exp(il - m) for il, m in zip(inter_log, m_j)]
    ones_bd = same_head.astype(F32)
    intra = [_dot(w, jnp.concatenate([block_diag(part(v_ref, u)), ones_bd], axis=1))
             for w, u in zip(w_intra, units)]
    inter = [_dot(part(q_ref, u), cn) for u, cn in zip(units, cn_prev)]
    outs = [(ia[:, 0:LANES] + wi * ie[:, 0:LANES])
            / jnp.maximum(jnp.abs(ia[:, LANES:] + wi * ie[:, LANES:]), jnp.exp(-m))
            for ia, wi, ie, m in zip(intra, w_inter, inter, m_j)]
    for ci in range(n_chunks):
        h_ref[pl.ds(ci * L, L), :] = jnp.concatenate(outs[ci * n_pairs:(ci + 1) * n_pairs], axis=1)


def _gate_expand_matrix():
    lane = jnp.arange(LANES)[:, None]
    col = jnp.arange(2 * MLSTM_WIDTH)[None, :]
    src = jnp.where(col < MLSTM_WIDTH, col // HEAD_DIM, MLSTM_HEADS + (col - MLSTM_WIDTH) // HEAD_DIM)
    return (lane == src).astype(BF16)


def _mlstm_call(q, k, v, g):
    B, S, W = q.shape
    ts = SEQ_ROWS
    spec = pl.BlockSpec((None, ts, W), lambda bi, c: (bi, c, 0))
    gspec = pl.BlockSpec((None, ts, LANES), lambda bi, c: (bi, c, 0))
    espec = pl.BlockSpec((LANES, 2 * W), lambda bi, c: (0, 0))
    return pl.pallas_call(
        _mlstm_kernel, grid=(B, S // ts), in_specs=[spec, spec, spec, gspec, espec], out_specs=spec,
        out_shape=jax.ShapeDtypeStruct((B, S, W), F32),
        scratch_shapes=[pltpu.VMEM((MLSTM_HEADS // 2, LANES, 2 * LANES), F32),
                        pltpu.VMEM((8, LANES), F32)],
        compiler_params=pltpu.CompilerParams(dimension_semantics=("arbitrary", "arbitrary"),
                                             vmem_limit_bytes=VMEM_LIMIT),
        name="mlstm",
    )(q, k, v, g, _gate_expand_matrix())


def _layer_norm(z, g, b):
    mu = jnp.mean(z, axis=-1, keepdims=True)
    d = z - mu
    var = jnp.mean(d * d, axis=-1, keepdims=True)
    return d * lax.rsqrt(var + LN_EPS) * g + b


def _post_kernel(h_ref, ypool_ref, yr_ref, bonus_ref, g_ref, hm_ref, mo_ref,
                 lnxg_ref, lnxb_ref, ng_ref, wout_ref, ln1g_ref, ln1b_ref, rwt_ref, rb_ref, ones_ref,
                 h1_ref, gate_ref, route_ref, cnt_ref):
    ts = PREP_ROWS
    tiles = range(h_ref.shape[0] // ts)
    rows = lambda ref, i: ref[pl.ds(i * ts, ts), :]
    ones_bd = ones_ref[...]
    nrm_r = [_head_norm(rows(yr_ref, i), ones_bd, RWKV_GN_EPS) for i in tiles]
    nrm_m = [_head_norm(rows(hm_ref, i), ones_bd, LN_EPS) for i in tiles]
    y_rwkv = [(n * lnxg_ref[...] + lnxb_ref[...] + rows(bonus_ref, i)) * rows(g_ref, i) for i, n in zip(tiles, nrm_r)]
    y_ml = [rows(mo_ref, i) * (n * ng_ref[...]) for i, n in zip(tiles, nrm_m)]
    mix = [_dot(rows(ypool_ref, i), wout_ref[0:POOL_WIDTH, :])
           + _dot(y_rwkv[i], wout_ref[POOL_WIDTH:POOL_WIDTH + RWKV_WIDTH, :])
           + _dot(y_ml[i], wout_ref[POOL_WIDTH + RWKV_WIDTH:, :]) for i in tiles]
    h1 = [_layer_norm(DEEPNORM_ALPHA * rows(h_ref, i) + mix[i], ln1g_ref[...], ln1b_ref[...]) for i in tiles]
    for i in tiles:
        h1_ref[pl.ds(i * ts, ts), :] = h1[i]

    wh, wl = _split(rwt_ref[...])
    nt = lambda a, b: lax.dot_general(a, b, (((1,), (1,)), ((), ())), preferred_element_type=F32)
    parts = [_split(x) for x in h1]
    vals = [nt(wh, hh) + nt(wh, hl) + nt(wl, hh) + rb_ref[...] for hh, hl in parts]
    eidx = lax.broadcasted_iota(I32, (N_EXPERTS, ts), 0)
    tops, hots = [[] for _ in tiles], [[] for _ in tiles]
    for _ in range(TOP_K):
        mx = [jnp.max(v, axis=0, keepdims=True) for v in vals]
        idx = [jnp.min(jnp.where(v == m, eidx, N_EXPERTS), axis=0, keepdims=True) for v, m in zip(vals, mx)]
        hot = [eidx == ix for ix in idx]
        vals = [jnp.where(h, -jnp.inf, v) for h, v in zip(hot, vals)]
        for i in tiles:
            tops[i].append(mx[i])
            hots[i].append(hot[i])
    for i in tiles:
        exps = [jnp.exp(t - tops[i][0]) for t in tops[i]]
        denom = exps[0] + exps[1] + exps[2] + exps[3]
        gate_ref[:, pl.ds(i * ts, ts)] = jnp.concatenate([e / denom for e in exps]
                                                         + [jnp.zeros((8 - TOP_K, ts), F32)], axis=0)

    hot_f = [(h[0] | h[1] | h[2] | h[3]).astype(F32) for h in hots]
    r_i = lax.broadcasted_iota(I32, (ts, ts), 0)
    c_i = lax.broadcasted_iota(I32, (ts, ts), 1)
    before = (r_i < c_i).astype(BF16)
    within = [jnp.dot(hf.astype(BF16), before, preferred_element_type=F32) for hf in hot_f]
    n_e = [jnp.broadcast_to(jnp.sum(hf, axis=1, keepdims=True), (N_EXPERTS, LANES)) for hf in hot_f]
    run_len = [jnp.floor((n + (RUN_PIECE - 1)) * (1.0 / RUN_PIECE)) * RUN_PIECE for n in n_e]
    e_r = lax.broadcasted_iota(I32, (N_EXPERTS, N_EXPERTS), 0)
    e_c = lax.broadcasted_iota(I32, (N_EXPERTS, N_EXPERTS), 1)
    lower = (e_c < e_r).astype(BF16)
    run_start = [_dot_rhs2(lower, rl)[:, 0:1] for rl in run_len]
    for i in tiles:
        slots = [jnp.sum(jnp.where(hot, run_start[i] + within[i], 0.0), axis=0, keepdims=True) for hot in hots[i]]
        route_ref[:, pl.ds(i * ts, ts)] = jnp.concatenate([p.astype(I32) for p in slots]
                                                          + [jnp.zeros((8 - TOP_K, ts), I32)], axis=0)
        cnt_ref[pl.ds(i * N_EXPERTS, N_EXPERTS), :] = n_e[i]


def _post_call(h, ypool, yr, bonus, g, hm, mo, lw):
    T = h.shape[0]
    ts = POST_TILES * PREP_ROWS
    row = lambda c: pl.BlockSpec((ts, c), lambda i: (i, 0))
    colb = lambda r: pl.BlockSpec((r, ts), lambda i: (0, i))
    full = lambda a: pl.BlockSpec(a.shape, lambda i: (0,) * a.ndim)
    params = [lw['lnx_g'], lw['lnx_b'], lw['norm_g'], lw['w_out'], lw['ln1_g'], lw['ln1_b'],
              lw['router_wt'], lw['router_b'], lw['ones_bd']]
    in_specs = ([row(D_MODEL), row(POOL_WIDTH)] + [row(RWKV_WIDTH)] * 5 + [full(a) for a in params])
    out_shape = [jax.ShapeDtypeStruct((T, D_MODEL), F32), jax.ShapeDtypeStruct((8, T), F32),
                 jax.ShapeDtypeStruct((8, T), I32),
                 jax.ShapeDtypeStruct((T // PREP_ROWS * N_EXPERTS, LANES), F32)]
    out_specs = [row(D_MODEL), colb(8), colb(8), pl.BlockSpec((POST_TILES * N_EXPERTS, LANES), lambda i: (i, 0))]
    return pl.pallas_call(
        _post_kernel, grid=(T // ts,), in_specs=in_specs, out_specs=out_specs, out_shape=out_shape,
        compiler_params=pltpu.CompilerParams(dimension_semantics=("arbitrary",),
                                             vmem_limit_bytes=VMEM_LIMIT),
        name="post",
    )(h, ypool, yr, bonus, g, hm, mo, *params)


def _rows_copy(src_ref, src_row, dst_ref, dst_row, n_rows, sem):
    src_row, dst_row = [r if isinstance(r, int) else pl.multiple_of(r, SUBLANES) for r in (src_row, dst_row)]
    return pltpu.make_async_copy(src_ref.at[pl.ds(src_row, n_rows), :], dst_ref.at[pl.ds(dst_row, n_rows), :], sem)


def _stage_rows(ts):
    n = TOP_K * ts + N_EXPERTS * (RUN_PIECE - 1)
    return -(-n // SUBLANES) * SUBLANES


def _for_each_piece(n_pieces, body):
    lax.fori_loop(0, n_pieces, lambda j, c: (body(j), c)[1], 0)


def _run_copies(tab_ref, tile, tile_ref, tile_is_src, hbm_ref, sem, wait):
    for e in range(N_EXPERTS):
        n_rows = pl.multiple_of(tab_ref[tile, 0, e], SUBLANES)

        @pl.when(n_rows > 0)
        def _(e=e, n_rows=n_rows):
            in_tile = (tile_ref, tab_ref[tile, 0, N_EXPERTS + e])
            in_hbm = (hbm_ref, tab_ref[tile, 0, 2 * N_EXPERTS + e])
            (src_ref, src_row), (dst_ref, dst_row) = (in_tile, in_hbm) if tile_is_src else (in_hbm, in_tile)
            cp = _rows_copy(src_ref, src_row, dst_ref, dst_row, n_rows, sem)
            cp.wait() if wait else cp.start()


def _dispatch_kernel(tab_ref, prev_tab_ref, fill_ref, route_ref, gate_ref, h_ref, buf_ref,
                     sorted_scr, zero_scr, sems):
    step = pl.program_id(0)
    last = pl.num_programs(0) - 1
    slot = step % 2
    ts = PREP_ROWS
    tiles = range(MOVE_TILES)
    n_sorted = sorted_scr.shape[2]

    @pl.when(step == 0)
    def _():
        zero_scr[...] = jnp.zeros(zero_scr.shape, F32)

        def fill(wait):
            for e in range(N_EXPERTS):
                def piece(j, e=e):
                    cp = _rows_copy(zero_scr, 0, buf_ref, fill_ref[0, e] + j * FILL_PIECE, FILL_PIECE, sems.at[0])
                    cp.wait() if wait else cp.start()
                _for_each_piece(fill_ref[0, N_EXPERTS + e], piece)

            def tail(j):
                cp = _rows_copy(zero_scr, 0, buf_ref, fill_ref[0, 2 * N_EXPERTS] + j * EXPERT_ROWS,
                                EXPERT_ROWS, sems.at[0])
                cp.wait() if wait else cp.start()
            _for_each_piece(fill_ref[0, 2 * N_EXPERTS + 1], tail)

        fill(False)
        fill(True)

    pos = lax.broadcasted_iota(I32, (n_sorted, ts), 0)
    hots, weights = [], []
    for t in tiles:
        cols = pl.ds(t * ts, ts)
        hot = None
        weight = None
        for kslot in range(TOP_K):
            eq = pos == route_ref[kslot:kslot + 1, cols]
            w = jnp.where(eq, gate_ref[kslot:kslot + 1, cols], 0.0)
            hot = eq if hot is None else (hot | eq)
            weight = w if weight is None else weight + w
        hots.append(hot)
        weights.append(weight)
    rows_sorted = [jnp.dot(hots[t].astype(BF16), h_ref[pl.ds(t * ts, ts), :].astype(BF16),
                           preferred_element_type=F32) for t in tiles]
    ones = jnp.ones((ts, LANES), BF16)
    gate_sorted = [_dot_lhs2(weights[t], ones) for t in tiles]
    for t in tiles:
        sorted_scr[slot, t, :, 0:D_MODEL] = rows_sorted[t]
        sorted_scr[slot, t, :, D_MODEL:BUF_WIDTH] = gate_sorted[t]

    def copies(tab, which, wait):
        for t in tiles:
            _run_copies(tab, t, sorted_scr.at[which, t], True, buf_ref, sems.at[which], wait)

    @pl.when(step > 0)
    def _():
        copies(prev_tab_ref, 1 - slot, True)

    copies(tab_ref, slot, False)

    @pl.when(step == last)
    def _():
        copies(tab_ref, slot, True)


def _dispatch_call(tab_d, fill_tab, route8, gate8, h1, n_rows):
    T = h1.shape[0]
    ts = MOVE_TILES * PREP_ROWS
    smem_tile = pl.BlockSpec((MOVE_TILES, 1, LANES), lambda i: (i, 0, 0), memory_space=pltpu.SMEM)
    smem_prev = pl.BlockSpec((MOVE_TILES, 1, LANES), lambda i: (jnp.maximum(i - 1, 0), 0, 0),
                             memory_space=pltpu.SMEM)
    return pl.pallas_call(
        _dispatch_kernel, grid=(T // ts,),
        in_specs=[smem_tile, smem_prev,
                  pl.BlockSpec((1, LANES), lambda i: (0, 0), memory_space=pltpu.SMEM),
                  pl.BlockSpec((8, ts), lambda i: (0, i)),
                  pl.BlockSpec((8, ts), lambda i: (0, i)),
                  pl.BlockSpec((ts, D_MODEL), lambda i: (i, 0))],
        out_specs=pl.BlockSpec(memory_space=pl.ANY),
        out_shape=jax.ShapeDtypeStruct((n_rows, BUF_WIDTH), F32),
        scratch_shapes=[pltpu.VMEM((2, MOVE_TILES, _stage_rows(PREP_ROWS), BUF_WIDTH), F32),
                        pltpu.VMEM((EXPERT_ROWS, BUF_WIDTH), F32),
                        pltpu.SemaphoreType.DMA((2,))],
        compiler_params=pltpu.CompilerParams(dimension_semantics=("arbitrary",),
                                             vmem_limit_bytes=VMEM_LIMIT),
        name="dispatch",
    )(tab_d, tab_d, fill_tab, route8, gate8, h1)


def _expert_kernel(be_ref, nb_ref, run_ref, next_ref, valid_ref, x_ref, wgu_hbm, bgu_ref, wdn_hbm, bdn_ref, o_ref,
                   wgu_f32, wdn_f32, wgu_scr, wdn_scr, sems, *, layer):
    j = pl.program_id(0)
    used = j < nb_ref[0]
    changed = jnp.logical_or(j == 0, be_ref[j] != be_ref[jnp.maximum(j - 1, 0)])

    def fetch(expert, slot):
        return (pltpu.make_async_copy(wgu_hbm.at[layer, expert], wgu_f32.at[slot], sems.at[0, slot]),
                pltpu.make_async_copy(wdn_hbm.at[layer, expert], wdn_f32.at[slot], sems.at[1, slot]))

    @pl.when(j == 0)
    def _():
        for cp in fetch(be_ref[0], 0):
            cp.start()

    @pl.when(jnp.logical_and(used, changed))
    def _():
        slot = run_ref[j] % 2
        for cp in fetch(be_ref[j], slot):
            cp.wait()
        wgu_scr[...] = wgu_f32[slot].astype(BF16)
        wdn_scr[...] = wdn_f32[slot].astype(BF16)

        @pl.when(next_ref[j] >= 0)
        def _():
            for cp in fetch(next_ref[j], 1 - slot):
                cp.start()

    def ffn(n_rows):
        gu = (jnp.dot(x_ref[0:n_rows, 0:D_MODEL].astype(BF16), wgu_scr[...], preferred_element_type=F32)
              + bgu_ref[...])
        glu = jnp.minimum(gu[:, 0:D_FF], SWIGLU_LIMIT)
        lin = jnp.clip(gu[:, D_FF:], -SWIGLU_LIMIT, SWIGLU_LIMIT)
        act = glu * _sigmoid(SWIGLU_ALPHA * glu) * (lin + 1.0)
        y = jnp.dot(act.astype(BF16), wdn_scr[...], preferred_element_type=F32) + bdn_ref[...]
        gate = x_ref[0:n_rows, D_MODEL:BUF_WIDTH]
        return y * jnp.concatenate([gate] * (D_MODEL // LANES), axis=1)

    half = EXPERT_ROWS // 2
    upper_needed = valid_ref[j] > half

    @pl.when(jnp.logical_and(used, upper_needed))
    def _():
        o_ref[...] = ffn(EXPERT_ROWS)

    @pl.when(jnp.logical_and(used, jnp.logical_not(upper_needed)))
    def _():
        o_ref[0:half, :] = ffn(half)
        o_ref[half:, :] = jnp.zeros((EXPERT_ROWS - half, D_MODEL), F32)

    @pl.when(jnp.logical_not(used))
    def _():
        o_ref[...] = jnp.zeros(o_ref.shape, F32)


def _expert_call(block_e, n_used, data_end, buf, lw):
    n_rows = buf.shape[0]
    rows = EXPERT_ROWS
    n_blocks = n_rows // rows
    j = jnp.arange(n_blocks, dtype=I32)
    first = (j < n_used[0]) & ((j == 0) | (block_e != jnp.roll(block_e, 1)))
    run_idx = jnp.cumsum(first.astype(I32)) - 1
    later_first = lax.cummin(jnp.where(first, j, n_blocks)[::-1])[::-1]
    next_start = jnp.concatenate([later_first[1:], jnp.full((1,), n_blocks, I32)])
    next_e = jnp.where(next_start < n_blocks, block_e[jnp.minimum(next_start, n_blocks - 1)], -1).astype(I32)
    valid = jnp.clip(jnp.sum(jnp.where(block_e[:, None] == jnp.arange(N_EXPERTS, dtype=I32), data_end, 0), axis=1)
                     - j * rows, 0, rows).astype(I32)

    def blk(j, be, nb, *s):
        return jnp.minimum(j, nb[0] - 1)

    layer = lw['layer']
    grid_spec = pltpu.PrefetchScalarGridSpec(
        num_scalar_prefetch=5, grid=(n_blocks,),
        in_specs=[pl.BlockSpec((rows, BUF_WIDTH), lambda j, *s: (blk(j, *s), 0)),
                  pl.BlockSpec(memory_space=pl.ANY),
                  pl.BlockSpec((None, None, 1, 2 * D_FF), lambda j, be, *s: (layer, be[j], 0, 0)),
                  pl.BlockSpec(memory_space=pl.ANY),
                  pl.BlockSpec((None, None, 1, D_MODEL), lambda j, be, *s: (layer, be[j], 0, 0))],
        out_specs=pl.BlockSpec((rows, D_MODEL), lambda j, *s: (j, 0)),
        scratch_shapes=[pltpu.VMEM((2, D_MODEL, 2 * D_FF), F32), pltpu.VMEM((2, D_FF, D_MODEL), F32),
                        pltpu.VMEM((D_MODEL, 2 * D_FF), BF16), pltpu.VMEM((D_FF, D_MODEL), BF16),
                        pltpu.SemaphoreType.DMA((2, 2))])
    return pl.pallas_call(
        functools.partial(_expert_kernel, layer=layer), grid_spec=grid_spec,
        out_shape=jax.ShapeDtypeStruct((n_rows, D_MODEL), F32),
        compiler_params=pltpu.CompilerParams(dimension_semantics=("arbitrary",),
                                             vmem_limit_bytes=56 * 1024 * 1024),
        name="expert",
    )(block_e, n_used, run_idx, next_e, valid, buf, lw['w_gate_up'], lw['b_gate_up'], lw['w_down'], lw['b_down'])


def _combine_kernel(tab_ref, next_tab_ref, route_ref, h_ref, out_hbm_ref, ln2g_ref, ln2b_ref, o_ref,
                    stage_scr, sems):
    step = pl.program_id(0)
    last = pl.num_programs(0) - 1
    slot = step % 2
    ts = PREP_ROWS
    tiles = range(MOVE_TILES)
    n_stage = stage_scr.shape[2]

    def copies(tab, which, wait):
        for t in tiles:
            _run_copies(tab, t, stage_scr.at[which, t], False, out_hbm_ref, sems.at[which], wait)

    @pl.when(step == 0)
    def _():
        stage_scr[...] = jnp.zeros(stage_scr.shape, F32)
        copies(tab_ref, 0, False)

    @pl.when(step < last)
    def _():
        copies(next_tab_ref, 1 - slot, False)

    lane = lax.broadcasted_iota(I32, (ts, n_stage), 1).astype(F32)
    sels = []
    for t in tiles:
        slot_f = route_ref[:, pl.ds(t * ts, ts)].astype(F32)
        slot_cols = jnp.concatenate([slot_f, jnp.zeros((LANES - 8, ts), F32)], axis=0).T
        sel = None
        for kslot in range(TOP_K):
            eq = lane == slot_cols[:, kslot:kslot + 1]
            sel = eq if sel is None else (sel | eq)
        sels.append(sel.astype(BF16))
    copies(tab_ref, slot, True)
    ys = [jnp.dot(sels[t], stage_scr[slot, t].astype(BF16), preferred_element_type=F32) for t in tiles]
    for t in tiles:
        rows = pl.ds(t * ts, ts)
        o_ref[rows, :] = _layer_norm(DEEPNORM_ALPHA * h_ref[rows, :] + ys[t], ln2g_ref[...], ln2b_ref[...])


def _combine_call(tab_c, route8, h1, out_rows, lw):
    T = h1.shape[0]
    ts = MOVE_TILES * PREP_ROWS
    n_tiles = T // ts
    full = lambda a: pl.BlockSpec(a.shape, lambda i: (0,) * a.ndim)
    n_stage = _stage_rows(PREP_ROWS)
    return pl.pallas_call(
        _combine_kernel, grid=(n_tiles,),
        in_specs=[pl.BlockSpec((MOVE_TILES, 1, LANES), lambda i: (i, 0, 0), memory_space=pltpu.SMEM),
                  pl.BlockSpec((MOVE_TILES, 1, LANES), lambda i: (jnp.minimum(i + 1, n_tiles - 1), 0, 0),
                               memory_space=pltpu.SMEM),
                  pl.BlockSpec((8, ts), lambda i: (0, i)),
                  pl.BlockSpec((ts, D_MODEL), lambda i: (i, 0)),
                  pl.BlockSpec(memory_space=pl.ANY),
                  full(lw['ln2_g']), full(lw['ln2_b'])],
        out_specs=pl.BlockSpec((ts, D_MODEL), lambda i: (i, 0)),
        out_shape=jax.ShapeDtypeStruct((T, D_MODEL), F32),
        scratch_shapes=[pltpu.VMEM((2, MOVE_TILES, n_stage, D_MODEL), F32), pltpu.SemaphoreType.DMA((2,))],
        compiler_params=pltpu.CompilerParams(dimension_semantics=("arbitrary",),
                                             vmem_limit_bytes=VMEM_LIMIT),
        name="combine",
    )(tab_c, tab_c, route8, h1, out_rows, lw['ln2_g'], lw['ln2_b'])


def _block_diag_ones(width):
    hid = jnp.arange(width) // HEAD_DIM
    return (hid[:, None] == hid[None, :]).astype(BF16)


def _layer_params(l, w_in, pool_w, pool_scale, rwkv_mu, rwkv_w0, rwkv_w2, rwkv_a0, rwkv_a2, rwkv_g2,
                  rwkv_kk_scale, rwkv_ka, rwkv_rk, rwkv_lnx_g, rwkv_lnx_b, rwkv_v0, rwkv_v1, rwkv_v2,
                  mlstm_conv_w, mlstm_conv_b, mlstm_b_i, mlstm_b_f, mlstm_norm_g, w_out, ln1_g, ln1_b,
                  router_w, router_b, w_gate_up, b_gate_up, w_down, b_down, ln2_g, ln2_b):
    row = lambda a: a.reshape(1, -1).astype(F32)
    pad_cols = D_IN_PAD - D_IN
    if l > 0:
        extra = jnp.concatenate([rwkv_v1[l - 1], jnp.zeros((D_MODEL, pad_cols - VRES_LORA), F32)], axis=1)
        v0 = row(rwkv_v0[l - 1])
        v2 = jnp.zeros((LANES, RWKV_WIDTH), F32).at[VRES_OFF:VRES_OFF + VRES_LORA].set(rwkv_v2[l - 1])
    else:
        extra = jnp.zeros((D_MODEL, pad_cols), F32)
        v0 = jnp.zeros((1, RWKV_WIDTH), F32)
        v2 = jnp.zeros((LANES, RWKV_WIDTH), F32)
    pw = jnp.zeros((POOL_WIDTH, POOL_WIDTH), F32)
    for gi in range(len(POOL_WINDOWS)):
        sl = slice(gi * POOL_GROUP, (gi + 1) * POOL_GROUP)
        pw = pw.at[sl, sl].set(pool_w[l, gi])
    zero_lora = jnp.zeros((DECAY_LORA, RWKV_WIDTH), F32)
    gate_bias = jnp.zeros((1, LANES), F32).at[0, 0:MLSTM_HEADS].set(mlstm_b_i[l])
    gate_bias = gate_bias.at[0, MLSTM_HEADS:2 * MLSTM_HEADS].set(mlstm_b_f[l])
    return {
        'w_in': jnp.concatenate([w_in[l], extra], axis=1).astype(BF16),
        'pool_w': pw.astype(BF16), 'pool_scale': row(pool_scale[l]), 'mu': row(rwkv_mu[l]),
        'w0': row(rwkv_w0[l]), 'w2': jnp.concatenate([rwkv_w2[l], zero_lora], axis=0).astype(BF16),
        'a0': row(rwkv_a0[l]), 'a2': jnp.concatenate([zero_lora, rwkv_a2[l]], axis=0).astype(BF16),
        'g2': rwkv_g2[l].astype(BF16), 'kk_scale': row(rwkv_kk_scale[l]), 'ka': row(rwkv_ka[l]),
        'rk': row(rwkv_rk[l]), 'v0': v0, 'v2': v2.astype(BF16),
        'conv_w': mlstm_conv_w[l], 'conv_b': row(mlstm_conv_b[l]), 'gate_bias': gate_bias,
        'ones_bd': _block_diag_ones(RWKV_WIDTH),
        'lnx_g': row(rwkv_lnx_g[l]), 'lnx_b': row(rwkv_lnx_b[l]), 'norm_g': row(mlstm_norm_g[l]),
        'w_out': w_out[l].astype(BF16), 'ln1_g': row(ln1_g[l]), 'ln1_b': row(ln1_b[l]),
        'router_wt': router_w[l].T, 'router_b': router_b[l].reshape(N_EXPERTS, 1),
        'layer': l, 'w_gate_up': w_gate_up, 'b_gate_up': b_gate_up.reshape(-1, N_EXPERTS, 1, 2 * D_FF),
        'w_down': w_down, 'b_down': b_down.reshape(-1, N_EXPERTS, 1, D_MODEL),
        'ln2_g': row(ln2_g[l]), 'ln2_b': row(ln2_b[l]),
    }


def _layer(h, v_first, lw, *, has_vres):
    B, S, _ = h.shape
    T = B * S
    outs = _prep_call(h, lw, v_first, has_vres=has_vres)
    ypool, r, ld, k, v, kk, b, g, bonus, mq, mk, mv, mo, mg = outs
    yr = _rwkv_call(r, ld, k, v, kk, b)
    hm = _mlstm_call(mq, mk, mv, mg)
    flat = lambda a: a.reshape(T, a.shape[-1])
    h_flat = flat(h)
    h1, gate8, route8, cnt = _post_call(h_flat, flat(ypool), flat(yr), flat(bonus), flat(g), flat(hm),
                                        flat(mo), lw)
    n_tiles = T // PREP_ROWS
    cnt = cnt.reshape(n_tiles, N_EXPERTS, LANES)[:, :, 0].astype(I32)
    rows = EXPERT_ROWS
    run_len = (cnt + RUN_PIECE - 1) // RUN_PIECE * RUN_PIECE
    region = jnp.sum(run_len, axis=0)
    padded = (region + rows - 1) // rows * rows
    pad_end = jnp.cumsum(padded)
    pad_start = pad_end - padded
    n_blocks = -(-(T * TOP_K + n_tiles * N_EXPERTS * (RUN_PIECE - 1)) // rows) + N_EXPERTS
    n_used = (pad_end[-1] // rows).astype(I32).reshape(1)
    starts = jnp.minimum(jnp.arange(n_blocks, dtype=I32), n_used[0] - 1) * rows
    block_e = jnp.sum((pad_end[None, :] <= starts[:, None]).astype(I32), axis=1)
    run_src = jnp.cumsum(run_len, axis=1) - run_len
    run_dst = pad_start[None, :] + jnp.cumsum(run_len, axis=0) - run_len
    run_tab = jnp.concatenate([run_len, run_src, run_dst,
                               jnp.zeros((n_tiles, LANES - 3 * N_EXPERTS), I32)], axis=1).reshape(n_tiles, 1, LANES)
    fill_start = (pad_start + region) // FILL_PIECE * FILL_PIECE
    fill_tab = jnp.concatenate([fill_start, (pad_end - fill_start) // FILL_PIECE,
                                pad_end[-1:], n_blocks - n_used,
                                jnp.zeros((LANES - 2 * N_EXPERTS - 2,), I32)]).reshape(1, LANES)
    buf = _dispatch_call(run_tab, fill_tab, route8, gate8, h1, n_blocks * rows)
    out_rows = _expert_call(block_e, n_used, pad_start + region, buf, lw)
    h2 = _combine_call(run_tab, route8, h1, out_rows, lw)
    return h2.reshape(B, S, D_MODEL), v


def kernel(x, w_in, pool_w, pool_scale, rwkv_mu, rwkv_w0, rwkv_w2, rwkv_a0, rwkv_a2, rwkv_g2, rwkv_kk_scale, rwkv_ka, rwkv_rk, rwkv_lnx_g, rwkv_lnx_b, rwkv_v0, rwkv_v1, rwkv_v2, mlstm_conv_w, mlstm_conv_b, mlstm_b_i, mlstm_b_f, mlstm_norm_g, w_out, ln1_g, ln1_b, router_w, router_b, w_gate_up, b_gate_up, w_down, b_down, ln2_g, ln2_b):
    weights = (w_in, pool_w, pool_scale, rwkv_mu, rwkv_w0, rwkv_w2, rwkv_a0, rwkv_a2, rwkv_g2, rwkv_kk_scale,
               rwkv_ka, rwkv_rk, rwkv_lnx_g, rwkv_lnx_b, rwkv_v0, rwkv_v1, rwkv_v2, mlstm_conv_w, mlstm_conv_b,
               mlstm_b_i, mlstm_b_f, mlstm_norm_g, w_out, ln1_g, ln1_b, router_w, router_b, w_gate_up,
               b_gate_up, w_down, b_down, ln2_g, ln2_b)
    h = x
    v_first = jnp.zeros(x.shape[:2] + (RWKV_WIDTH,), F32)
    for l in range(w_in.shape[0]):
        lw = _layer_params(l, *weights)
        h, v_l = _layer(h, v_first, lw, has_vres=l > 0)
        if l == 0:
            v_first = v_l
    return h
```

```python
import functools

import jax
import jax.numpy as jnp
from jax import lax
from jax.experimental import pallas as pl
from jax.experimental.pallas import tpu as pltpu

F32 = jnp.float32
BF16 = jnp.bfloat16
I32 = jnp.int32

D_MODEL = 1024
HEAD_DIM = 64
POOL_WINDOWS = (2, 4, 8, 16)
POOL_WIDTH = 256
POOL_GROUP = 64
RWKV_WIDTH = 384
RWKV_HEADS = 6
DECAY_LORA = 64
ICLR_LORA = 64
GATE_LORA = 128
VRES_LORA = 32
RWKV_GN_EPS = 64e-5
RWKV_COLS = 3 * RWKV_WIDTH + DECAY_LORA + ICLR_LORA + GATE_LORA
MLSTM_WIDTH = 384
MLSTM_HEADS = 6
MLSTM_CONV = 4
MLSTM_COLS = 4 * MLSTM_WIDTH + 2 * MLSTM_HEADS
D_IN = POOL_WIDTH + RWKV_COLS + MLSTM_COLS
N_EXPERTS = 32
TOP_K = 4
D_FF = D_MODEL
SWIGLU_LIMIT = 7.0
SWIGLU_ALPHA = 1.702
LN_EPS = 1e-5
DEPTH = 2
DEEPNORM_ALPHA = (2 * DEPTH) ** 0.25

LANES = 128
D_IN_PAD = 3328
RWKV_OFF = POOL_WIDTH
MLSTM_OFF = POOL_WIDTH + RWKV_COLS
GATE_OFF = MLSTM_OFF + 4 * MLSTM_WIDTH
VRES_OFF = 2 * MLSTM_HEADS
HALO = 16
CHUNK = 64

PROJ_ROWS = 512
PREP_ROWS = 256
POST_TILES = 4
MOVE_TILES = 2
SEQ_ROWS = 512
EXPERT_ROWS = 512
BUF_WIDTH = D_MODEL + LANES
SUBLANES = 8
RUN_PIECE = SUBLANES
FILL_PIECE = 64
VMEM_LIMIT = 48 * 1024 * 1024


def _dot(a, b):
    return jnp.dot(a.astype(BF16), b.astype(BF16), preferred_element_type=F32)


def _dot_nt(a, b):
    return lax.dot_general(a.astype(BF16), b.astype(BF16), (((1,), (1,)), ((), ())),
                           preferred_element_type=F32)


def _dot_tn(a, b):
    return lax.dot_general(a.astype(BF16), b.astype(BF16), (((0,), (0,)), ((), ())),
                           preferred_element_type=F32)


def _split(x):
    hi = x.astype(BF16)
    lo = (x - hi.astype(F32)).astype(BF16)
    return hi, lo


def _dot_lhs2(a, b_bf16):
    hi, lo = _split(a)
    return (jnp.dot(hi, b_bf16, preferred_element_type=F32)
            + jnp.dot(lo, b_bf16, preferred_element_type=F32))


def _dot_rhs2(a_bf16, b):
    hi, lo = _split(b)
    return (jnp.dot(a_bf16, hi, preferred_element_type=F32)
            + jnp.dot(a_bf16, lo, preferred_element_type=F32))


def _sigmoid(x):
    return 1.0 / (1.0 + jnp.exp(-x))


def _softplus(x):
    return jnp.maximum(x, 0.0) + jnp.log(1.0 + jnp.exp(-jnp.abs(x)))


def _head_norm(y, ones_bd, eps):
    inv = 1.0 / HEAD_DIM
    mean = _dot(y, ones_bd) * inv
    d = y - mean
    var = _dot(d * d, ones_bd) * inv
    return d * lax.rsqrt(var + eps)


def _prep_kernel(x_ref, w_ref, poolw_ref, pscale_ref, mu_ref, w0_ref, w2_ref, a0_ref, a2_ref, g2_ref,
                 kks_ref, ka_ref, rk_ref, v0_ref, v2_ref, vfirst_ref, cw_ref, cb_ref, gbias_ref, ones_ref,
                 ypool_ref, r_ref, ld_ref, k_ref, v_ref, kk_ref, b_ref, g_ref, bonus_ref,
                 mq_ref, mk_ref, mv_ref, mo_ref, mg_ref,
                 p_scr, *, has_vres):
    i = pl.program_id(1)
    ts = x_ref.shape[0]

    @pl.when(i == 0)
    def _():
        p_scr[0:HALO, :] = jnp.zeros((HALO, D_IN_PAD), F32)

    @pl.when(i > 0)
    def _():
        p_scr[0:HALO, :] = p_scr[ts:ts + HALO, :]

    p_scr[HALO:HALO + ts, :] = jnp.dot(x_ref[...].astype(BF16), w_ref[...], preferred_element_type=F32)

    def rows(shift, c0, c1):
        return p_scr[HALO - shift:HALO - shift + ts, c0:c1]

    u = rows(0, 0, POOL_WIDTH)
    acc = u
    sums = {}
    for s in range(1, POOL_WINDOWS[-1]):
        acc = acc + rows(s, 0, POOL_WIDTH)
        if s + 1 in POOL_WINDOWS:
            sums[s + 1] = acc
    pos = (i * ts + lax.broadcasted_iota(I32, (ts, 1), 0) + 1).astype(F32)
    lane = lax.broadcasted_iota(I32, (ts, POOL_WIDTH), 1)
    d = None
    for gi, win in reversed(list(enumerate(POOL_WINDOWS))):
        dg = sums[win] / jnp.minimum(pos, float(win))
        d = dg if d is None else jnp.where(lane < (gi + 1) * POOL_GROUP, dg, d)
    d = d - u
    ypool_ref[...] = _dot(d, poolw_ref[...]) * pscale_ref[...]

    cur = rows(0, RWKV_OFF, RWKV_OFF + RWKV_COLS)
    prev = rows(1, RWKV_OFF, RWKV_OFF + RWKV_COLS)
    pf = cur + mu_ref[...] * (prev - cur)
    W = RWKV_WIDTH
    r = pf[:, 0:W]
    k = pf[:, W:2 * W]
    v = pf[:, 2 * W:3 * W]
    z = pf[:, 3 * W:3 * W + LANES]
    gd = pf[:, 3 * W + LANES:3 * W + 2 * LANES]
    w_log = -_softplus(-(w0_ref[...] + _dot(jnp.tanh(z), w2_ref[...]))) - 0.5
    ld_ref[...] = -jnp.exp(w_log)
    a = _sigmoid(a0_ref[...] + _dot(z, a2_ref[...]))
    g_ref[...] = _dot(_sigmoid(gd), g2_ref[...])
    gates = rows(0, GATE_OFF, GATE_OFF + LANES)
    if has_vres:
        v_gate = _sigmoid(v0_ref[...] + _dot(gates, v2_ref[...]))
        v = v + (vfirst_ref[...] - v) * v_gate
    ones_bd = ones_ref[...]
    kk = k * kks_ref[...]
    kk = kk / jnp.maximum(jnp.sqrt(_dot(kk * kk, ones_bd)), 1e-12)
    k = k * (1.0 + (a - 1.0) * ka_ref[...])
    r_ref[...] = r
    k_ref[...] = k
    v_ref[...] = v
    kk_ref[...] = kk
    b_ref[...] = kk * a
    bonus_ref[...] = _dot(r * k * rk_ref[...], ones_bd) * v

    qk = cb_ref[...] + rows(0, MLSTM_OFF, MLSTM_OFF + 2 * MLSTM_WIDTH) * cw_ref[MLSTM_CONV - 1:MLSTM_CONV, :]
    for tap in range(MLSTM_CONV - 1):
        shift = MLSTM_CONV - 1 - tap
        qk = qk + rows(shift, MLSTM_OFF, MLSTM_OFF + 2 * MLSTM_WIDTH) * cw_ref[tap:tap + 1, :]
    qk = qk * _sigmoid(qk)
    mq_ref[...] = qk[:, 0:MLSTM_WIDTH] * (HEAD_DIM ** -0.5)
    mk_ref[...] = qk[:, MLSTM_WIDTH:]
    mv_ref[...] = rows(0, MLSTM_OFF + 2 * MLSTM_WIDTH, MLSTM_OFF + 3 * MLSTM_WIDTH)
    mo_ref[...] = _sigmoid(rows(0, MLSTM_OFF + 3 * MLSTM_WIDTH, MLSTM_OFF + 4 * MLSTM_WIDTH))
    gb = gates + gbias_ref[...]
    glane = lax.broadcasted_iota(I32, (ts, LANES), 1)
    mg_ref[...] = jnp.where(glane < MLSTM_HEADS, gb, -_softplus(-gb))


def _prep_call(x, lw, vfirst, *, has_vres):
    B, S, _ = x.shape
    ts = PROJ_ROWS
    grid = (B, S // ts)
    row3 = lambda c: pl.BlockSpec((None, ts, c), lambda b, i: (b, i, 0))
    full = lambda a: pl.BlockSpec(a.shape, lambda b, i: (0,) * a.ndim)
    params = [lw['w_in'], lw['pool_w'], lw['pool_scale'], lw['mu'], lw['w0'], lw['w2'], lw['a0'], lw['a2'],
              lw['g2'], lw['kk_scale'], lw['ka'], lw['rk'], lw['v0'], lw['v2']]
    tail = [lw['conv_w'], lw['conv_b'], lw['gate_bias'], lw['ones_bd']]
    in_specs = ([row3(D_MODEL)] + [full(a) for a in params] + [row3(RWKV_WIDTH)] + [full(a) for a in tail])
    widths = [POOL_WIDTH] + [RWKV_WIDTH] * 8 + [MLSTM_WIDTH] * 4 + [LANES]
    out_shape = [jax.ShapeDtypeStruct((B, S, c), F32) for c in widths]
    out_specs = [row3(c) for c in widths]
    return pl.pallas_call(
        functools.partial(_prep_kernel, has_vres=has_vres),
        grid=grid, in_specs=in_specs, out_specs=out_specs, out_shape=out_shape,
        scratch_shapes=[pltpu.VMEM((HALO + ts, D_IN_PAD), F32)],
        compiler_params=pltpu.CompilerParams(dimension_semantics=("arbitrary", "arbitrary"),
                                             vmem_limit_bytes=VMEM_LIMIT),
        name="prep",
    )(x, *params, vfirst, *tail)


def _rwkv_kernel(r_ref, ld_ref, k_ref, v_ref, kk_ref, b_ref, y_ref, s_scr):
    c = pl.program_id(1)
    L = CHUNK
    n_chunks = r_ref.shape[0] // L

    @pl.when(c == 0)
    def _():
        s_scr[...] = jnp.zeros(s_scr.shape, F32)

    row = lax.broadcasted_iota(I32, (L, L), 0)
    col = lax.broadcasted_iota(I32, (L, L), 1)
    tri = (col <= row).astype(BF16)
    row_l = lax.broadcasted_iota(I32, (L, LANES), 0)
    col_l = lax.broadcasted_iota(I32, (L, LANES), 1) % HEAD_DIM
    strict = col_l < row_l
    incl = col_l <= row_l
    eye = (col_l == row_l).astype(F32)
    r2 = lax.broadcasted_iota(I32, (LANES, LANES), 0)
    c2 = lax.broadcasted_iota(I32, (LANES, LANES), 1)
    same_head = (r2 // HEAD_DIM) == (c2 // HEAD_DIM)
    eye2 = (r2 == c2).astype(F32)
    low_lanes = lax.broadcasted_iota(I32, (L, LANES), 1) < HEAD_DIM
    n_pairs = RWKV_HEADS // 2

    def block_diag(x):
        xb = x.astype(BF16)
        return jnp.where(same_head, jnp.concatenate([xb, xb], axis=0), jnp.zeros((), BF16))

    def diag_blocks(z):
        return jnp.where(low_lanes, z[0:L, :], z[L:2 * L, :])

    units = [(ci, p) for ci in range(n_chunks) for p in range(n_pairs)]
    per_chunk = []
    for ci in range(n_chunks):
        sl = pl.ds(ci * L, L)
        ld = ld_ref[sl, :]
        cin = _dot_rhs2(tri, ld)
        c_last = cin[L - 1:L, :]
        e_neg = jnp.exp(-cin)
        e_tail = jnp.exp(c_last - cin)
        kk = kk_ref[sl, :]
        bb = b_ref[sl, :]
        kx = k_ref[sl, :]
        per_chunk.append(dict(
            A=-kk * jnp.exp(cin - ld), R=r_ref[sl, :] * jnp.exp(cin), B=bb * e_neg, K=kx * e_neg,
            Bh=bb * e_tail, Kh=kx * e_tail, V=v_ref[sl, :], g_last=jnp.exp(c_last)))

    def part(name, u):
        ci, p = u
        return per_chunk[ci][name][:, p * LANES:(p + 1) * LANES]

    M = [_dot_nt(jnp.concatenate([part('A', u), part('R', u)], axis=0),
                 jnp.concatenate([block_diag(part('B', u)), block_diag(part('K', u))], axis=0)) for u in units]
    m_ab = [jnp.where(strict, m[0:L, 0:LANES], 0.0) for m in M]
    m_ak = [jnp.where(strict, m[0:L, LANES:], 0.0) for m in M]
    m_rb = [jnp.where(incl, m[L:, 0:LANES], 0.0) for m in M]
    m_rk = [jnp.where(incl, m[L:, LANES:], 0.0) for m in M]
    MVYK = [_dot(jnp.concatenate([ak, rk], axis=0), block_diag(part('V', u)))
            for ak, rk, u in zip(m_ak, m_rk, units)]
    MV = [x[0:L, :] for x in MVYK]
    YK = [x[L:, :] for x in MVYK]
    T = [eye + m for m in m_ab]
    pw = [_dot(m, block_diag(m)) for m in m_ab]
    for _ in range(4):
        both = [_dot(jnp.concatenate([t, p], axis=0), block_diag(p)) for t, p in zip(T, pw)]
        T = [t + x[0:L, :] for t, x in zip(T, both)]
        pw = [x[L:, :] for x in both]
    T = [t + _dot(t, block_diag(p)) for t, p in zip(T, pw)]
    WU = [_dot(t, jnp.concatenate([block_diag(part('A', u)), block_diag(mv)], axis=1))
          for t, u, mv in zip(T, units, MV)]
    GY = [_dot(m, jnp.concatenate([block_diag(wu[:, 0:LANES]), block_diag(wu[:, LANES:])], axis=1))
          for m, wu in zip(m_rb, WU)]
    G = [part('R', u) + gy[:, 0:LANES] for u, gy in zip(units, GY)]
    Y0 = [gy[:, LANES:] + yk for gy, yk in zip(GY, YK)]
    P = [jnp.where(same_head, _dot_tn(wu[:, 0:LANES], part('Bh', u)), 0.0) + eye2 * part('g_last', u)
         for wu, u in zip(WU, units)]
    Q = [diag_blocks(_dot_tn(jnp.concatenate([wu[:, LANES:], part('V', u)], axis=0),
                             jnp.concatenate([part('Bh', u), part('Kh', u)], axis=0)))
         for wu, u in zip(WU, units)]

    state = [s_scr[p] for p in range(n_pairs)]
    for ci in range(n_chunks):
        base = ci * n_pairs
        ys = [_dot_nt(G[base + p], block_diag(state[p])) + Y0[base + p] for p in range(n_pairs)]
        state = [_dot(state[p], P[base + p]) + Q[base + p] for p in range(n_pairs)]
        y_ref[pl.ds(ci * L, L), :] = jnp.concatenate(ys, axis=1)
    for p in range(n_pairs):
        s_scr[p] = state[p]


def _rwkv_call(r, ld, k, v, kk, b):
    B, S, W = r.shape
    ts = SEQ_ROWS
    spec = pl.BlockSpec((None, ts, W), lambda bi, c: (bi, c, 0))
    return pl.pallas_call(
        _rwkv_kernel, grid=(B, S // ts), in_specs=[spec] * 6, out_specs=spec,
        out_shape=jax.ShapeDtypeStruct((B, S, W), F32),
        scratch_shapes=[pltpu.VMEM((RWKV_HEADS // 2, HEAD_DIM, LANES), F32)],
        compiler_params=pltpu.CompilerParams(dimension_semantics=("arbitrary", "arbitrary"),
                                             vmem_limit_bytes=VMEM_LIMIT),
        name="rwkv",
    )(r, ld, k, v, kk, b)


def _mlstm_kernel(q_ref, k_ref, v_ref, g_ref, expand_ref, h_ref, cn_scr, m_scr):
    c = pl.program_id(1)
    L = CHUNK
    n_chunks = q_ref.shape[0] // L
    H = MLSTM_HEADS
    n_pairs = H // 2

    @pl.when(c == 0)
    def _():
        cn_scr[...] = jnp.zeros(cn_scr.shape, F32)
        m_scr[...] = jnp.zeros(m_scr.shape, F32)

    row = lax.broadcasted_iota(I32, (L, L), 0)
    col = lax.broadcasted_iota(I32, (L, L), 1)
    tri = (col <= row).astype(BF16)
    row_l = lax.broadcasted_iota(I32, (L, LANES), 0)
    col_l = lax.broadcasted_iota(I32, (L, LANES), 1)
    incl2 = (col_l % HEAD_DIM) <= row_l
    r2 = lax.broadcasted_iota(I32, (LANES, LANES), 0)
    c2 = lax.broadcasted_iota(I32, (LANES, LANES), 1)
    same_head = (r2 // HEAD_DIM) == (c2 // HEAD_DIM)
    same_head2 = jnp.concatenate([same_head, same_head], axis=1)

    units = [(ci, p) for ci in range(n_chunks) for p in range(n_pairs)]
    ig_rep, g_rep, x_t = [], [], []
    for ci in range(n_chunks):
        gt = g_ref[pl.ds(ci * L, L), :]
        rep = _dot_lhs2(gt, expand_ref[...])
        ig_rep.append(rep[:, 0:MLSTM_WIDTH])
        g_rep.append(_dot_rhs2(tri, rep[:, MLSTM_WIDTH:]))
        x_t.append(gt.T[0:H, :] - _dot_rhs2(tri, gt).T[H:2 * H, :])

    def part(ref, u):
        ci, p = u
        return ref[pl.ds(ci * L, L), p * LANES:(p + 1) * LANES]

    def pair(x, p):
        return x[:, p * LANES:(p + 1) * LANES]

    def block_diag(x):
        return jnp.where(same_head, jnp.concatenate([x, x], axis=0), 0.0)

    ig_c = [pair(ig_rep[ci], p) for ci, p in units]
    g_c = [pair(g_rep[ci], p) for ci, p in units]
    g_last = [g[L - 1:L, :] for g in g_c]
    x_row = [jnp.concatenate([x_t[ci][2 * p:2 * p + 1, :], x_t[ci][2 * p + 1:2 * p + 2, :]], axis=1)
             for ci, p in units]
    d_log = [jnp.where(incl2, g + xr, -jnp.inf) for g, xr in zip(g_c, x_row)]
    x_run = [ig - g for ig, g in zip(ig_c, g_c)]
    shift = 1
    while shift < L:
        x_run = [jnp.maximum(x, jnp.where(row_l >= shift, pltpu.roll(x, shift, 0), -jnp.inf)) for x in x_run]
        shift *= 2
    d_max = [g + x for g, x in zip(g_c, x_run)]
    qk = [_dot_nt(part(q_ref, u), block_diag(part(k_ref, u))) for u in units]
    e = [gl - g + ig for gl, g, ig in zip(g_last, g_c, ig_c)]
    m_loc = [jnp.max(x, axis=0, keepdims=True) for x in e]
    wk = [part(k_ref, u) * jnp.exp(x - m) for u, x, m in zip(units, e, m_loc)]
    ones = jnp.ones((L, LANES), F32)
    kvn_loc = [jnp.where(same_head2, _dot_tn(w, jnp.concatenate([part(v_ref, u), ones], axis=1)), 0.0)
               for w, u in zip(wk, units)]

    cn_st = [cn_scr[p] for p in range(n_pairs)]
    m_st = [m_scr[p:p + 1, :] for p in range(n_pairs)]
    cn_prev, m_prev = [], []
    for i, (ci, p) in enumerate(units):
        cn_prev.append(cn_st[p])
        m_prev.append(m_st[p])
        m_new = jnp.maximum(g_last[i] + m_st[p], m_loc[i])
        a_old = jnp.exp(g_last[i] + m_st[p] - m_new)
        a_new = jnp.exp(m_loc[i] - m_new)
        cn_st[p] = (jnp.concatenate([a_old, a_old], axis=1) * cn_st[p]
                    + jnp.concatenate([a_new, a_new], axis=1) * kvn_loc[i])
        m_st[p] = m_new
    for p in range(n_pairs):
        cn_scr[p] = cn_st[p]
        m_scr[p:p + 1, :] = m_st[p]

    inter_log = [g + m for g, m in zip(g_c, m_prev)]
    m_j = [jnp.maximum(dm, il) for dm, il in zip(d_max, inter_log)]
    w_intra = [jnp.exp(d - m) * s for d, m, s in zip(d_log, m_j, qk)]
    w_inter = [jnp.exp(il - m) for il, m in zip(inter_log, m_j)]
    ones_bd = same_head.astype(F32)
    intra = [_dot(w, jnp.concatenate([block_diag(part(v_ref, u)), ones_bd], axis=1))
             for w, u in zip(w_intra, units)]
    inter = [_dot(part(q_ref, u), cn) for u, cn in zip(units, cn_prev)]
    outs = [(ia[:, 0:LANES] + wi * ie[:, 0:LANES])
            / jnp.maximum(jnp.abs(ia[:, LANES:] + wi * ie[:, LANES:]), jnp.exp(-m))
            for ia, wi, ie, m in zip(intra, w_inter, inter, m_j)]
    for ci in range(n_chunks):
        h_ref[pl.ds(ci * L, L), :] = jnp.concatenate(outs[ci * n_pairs:(ci + 1) * n_pairs], axis=1)


def _gate_expand_matrix():
    lane = jnp.arange(LANES)[:, None]
    col = jnp.arange(2 * MLSTM_WIDTH)[None, :]
    src = jnp.where(col < MLSTM_WIDTH, col // HEAD_DIM, MLSTM_HEADS + (col - MLSTM_WIDTH) // HEAD_DIM)
    return (lane == src).astype(BF16)


def _mlstm_call(q, k, v, g):
    B, S, W = q.shape
    ts = SEQ_ROWS
    spec = pl.BlockSpec((None, ts, W), lambda bi, c: (bi, c, 0))
    gspec = pl.BlockSpec((None, ts, LANES), lambda bi, c: (bi, c, 0))
    espec = pl.BlockSpec((LANES, 2 * W), lambda bi, c: (0, 0))
    return pl.pallas_call(
        _mlstm_kernel, grid=(B, S // ts), in_specs=[spec, spec, spec, gspec, espec], out_specs=spec,
        out_shape=jax.ShapeDtypeStruct((B, S, W), F32),
        scratch_shapes=[pltpu.VMEM((MLSTM_HEADS // 2, LANES, 2 * LANES), F32),
                        pltpu.VMEM((8, LANES), F32)],
        compiler_params=pltpu.CompilerParams(dimension_semantics=("arbitrary", "arbitrary"),
                                             vmem_limit_bytes=VMEM_LIMIT),
        name="mlstm",
    )(q, k, v, g, _gate_expand_matrix())


def _layer_norm(z, g, b):
    mu = jnp.mean(z, axis=-1, keepdims=True)
    d = z - mu
    var = jnp.mean(d * d, axis=-1, keepdims=True)
    return d * lax.rsqrt(var + LN_EPS) * g + b


def _post_kernel(h_ref, ypool_ref, yr_ref, bonus_ref, g_ref, hm_ref, mo_ref,
                 lnxg_ref, lnxb_ref, ng_ref, wout_ref, ln1g_ref, ln1b_ref, rwt_ref, rb_ref, ones_ref,
                 h1_ref, gate_ref, route_ref, cnt_ref):
    ts = PREP_ROWS
    tiles = range(h_ref.shape[0] // ts)
    rows = lambda ref, i: ref[pl.ds(i * ts, ts), :]
    ones_bd = ones_ref[...]
    nrm_r = [_head_norm(rows(yr_ref, i), ones_bd, RWKV_GN_EPS) for i in tiles]
    nrm_m = [_head_norm(rows(hm_ref, i), ones_bd, LN_EPS) for i in tiles]
    y_rwkv = [(n * lnxg_ref[...] + lnxb_ref[...] + rows(bonus_ref, i)) * rows(g_ref, i) for i, n in zip(tiles, nrm_r)]
    y_ml = [rows(mo_ref, i) * (n * ng_ref[...]) for i, n in zip(tiles, nrm_m)]
    mix = [_dot(rows(ypool_ref, i), wout_ref[0:POOL_WIDTH, :])
           + _dot(y_rwkv[i], wout_ref[POOL_WIDTH:POOL_WIDTH + RWKV_WIDTH, :])
           + _dot(y_ml[i], wout_ref[POOL_WIDTH + RWKV_WIDTH:, :]) for i in tiles]
    h1 = [_layer_norm(DEEPNORM_ALPHA * rows(h_ref, i) + mix[i], ln1g_ref[...], ln1b_ref[...]) for i in tiles]
    for i in tiles:
        h1_ref[pl.ds(i * ts, ts), :] = h1[i]

    wh, wl = _split(rwt_ref[...])
    nt = lambda a, b: lax.dot_general(a, b, (((1,), (1,)), ((), ())), preferred_element_type=F32)
    parts = [_split(x) for x in h1]
    vals = [nt(wh, hh) + nt(wh, hl) + nt(wl, hh) + rb_ref[...] for hh, hl in parts]
    eidx = lax.broadcasted_iota(I32, (N_EXPERTS, ts), 0)
    tops, hots = [[] for _ in tiles], [[] for _ in tiles]
    for _ in range(TOP_K):
        mx = [jnp.max(v, axis=0, keepdims=True) for v in vals]
        idx = [jnp.min(jnp.where(v == m, eidx, N_EXPERTS), axis=0, keepdims=True) for v, m in zip(vals, mx)]
        hot = [eidx == ix for ix in idx]
        vals = [jnp.where(h, -jnp.inf, v) for h, v in zip(hot, vals)]
        for i in tiles:
            tops[i].append(mx[i])
            hots[i].append(hot[i])
    for i in tiles:
        exps = [jnp.exp(t - tops[i][0]) for t in tops[i]]
        denom = exps[0] + exps[1] + exps[2] + exps[3]
        gate_ref[:, pl.ds(i * ts, ts)] = jnp.concatenate([e / denom for e in exps]
                                                         + [jnp.zeros((8 - TOP_K, ts), F32)], axis=0)

    hot_f = [(h[0] | h[1] | h[2] | h[3]).astype(F32) for h in hots]
    r_i = lax.broadcasted_iota(I32, (ts, ts), 0)
    c_i = lax.broadcasted_iota(I32, (ts, ts), 1)
    before = (r_i < c_i).astype(BF16)
    within = [jnp.dot(hf.astype(BF16), before, preferred_element_type=F32) for hf in hot_f]
    n_e = [jnp.broadcast_to(jnp.sum(hf, axis=1, keepdims=True), (N_EXPERTS, LANES)) for hf in hot_f]
    run_len = [jnp.floor((n + (RUN_PIECE - 1)) * (1.0 / RUN_PIECE)) * RUN_PIECE for n in n_e]
    e_r = lax.broadcasted_iota(I32, (N_EXPERTS, N_EXPERTS), 0)
    e_c = lax.broadcasted_iota(I32, (N_EXPERTS, N_EXPERTS), 1)
    lower = (e_c < e_r).astype(BF16)
    run_start = [_dot_rhs2(lower, rl)[:, 0:1] for rl in run_len]
    for i in tiles:
        slots = [jnp.sum(jnp.where(hot, run_start[i] + within[i], 0.0), axis=0, keepdims=True) for hot in hots[i]]
        route_ref[:, pl.ds(i * ts, ts)] = jnp.concatenate([p.astype(I32) for p in slots]
                                                          + [jnp.zeros((8 - TOP_K, ts), I32)], axis=0)
        cnt_ref[pl.ds(i * N_EXPERTS, N_EXPERTS), :] = n_e[i]


def _post_call(h, ypool, yr, bonus, g, hm, mo, lw):
    T = h.shape[0]
    ts = POST_TILES * PREP_ROWS
    row = lambda c: pl.BlockSpec((ts, c), lambda i: (i, 0))
    colb = lambda r: pl.BlockSpec((r, ts), lambda i: (0, i))
    full = lambda a: pl.BlockSpec(a.shape, lambda i: (0,) * a.ndim)
    params = [lw['lnx_g'], lw['lnx_b'], lw['norm_g'], lw['w_out'], lw['ln1_g'], lw['ln1_b'],
              lw['router_wt'], lw['router_b'], lw['ones_bd']]
    in_specs = ([row(D_MODEL), row(POOL_WIDTH)] + [row(RWKV_WIDTH)] * 5 + [full(a) for a in params])
    out_shape = [jax.ShapeDtypeStruct((T, D_MODEL), F32), jax.ShapeDtypeStruct((8, T), F32),
                 jax.ShapeDtypeStruct((8, T), I32),
                 jax.ShapeDtypeStruct((T // PREP_ROWS * N_EXPERTS, LANES), F32)]
    out_specs = [row(D_MODEL), colb(8), colb(8), pl.BlockSpec((POST_TILES * N_EXPERTS, LANES), lambda i: (i, 0))]
    return pl.pallas_call(
        _post_kernel, grid=(T // ts,), in_specs=in_specs, out_specs=out_specs, out_shape=out_shape,
        compiler_params=pltpu.CompilerParams(dimension_semantics=("arbitrary",),
                                             vmem_limit_bytes=VMEM_LIMIT),
        name="post",
    )(h, ypool, yr, bonus, g, hm, mo, *params)


def _rows_copy(src_ref, src_row, dst_ref, dst_row, n_rows, sem):
    src_row, dst_row = [r if isinstance(r, int) else pl.multiple_of(r, SUBLANES) for r in (src_row, dst_row)]
    return pltpu.make_async_copy(src_ref.at[pl.ds(src_row, n_rows), :], dst_ref.at[pl.ds(dst_row, n_rows), :], sem)


def _stage_rows(ts):
    n = TOP_K * ts + N_EXPERTS * (RUN_PIECE - 1)
    return -(-n // SUBLANES) * SUBLANES


def _for_each_piece(n_pieces, body):
    lax.fori_loop(0, n_pieces, lambda j, c: (body(j), c)[1], 0)


def _run_copies(tab_ref, tile, tile_ref, tile_is_src, hbm_ref, sem, wait):
    for e in range(N_EXPERTS):
        n_rows = pl.multiple_of(tab_ref[tile, 0, e], SUBLANES)

        @pl.when(n_rows > 0)
        def _(e=e, n_rows=n_rows):
            in_tile = (tile_ref, tab_ref[tile, 0, N_EXPERTS + e])
            in_hbm = (hbm_ref, tab_ref[tile, 0, 2 * N_EXPERTS + e])
            (src_ref, src_row), (dst_ref, dst_row) = (in_tile, in_hbm) if tile_is_src else (in_hbm, in_tile)
            cp = _rows_copy(src_ref, src_row, dst_ref, dst_row, n_rows, sem)
            cp.wait() if wait else cp.start()


def _dispatch_kernel(tab_ref, prev_tab_ref, fill_ref, route_ref, gate_ref, h_ref, buf_ref,
                     sorted_scr, zero_scr, sems):
    step = pl.program_id(0)
    last = pl.num_programs(0) - 1
    slot = step % 2
    ts = PREP_ROWS
    tiles = range(MOVE_TILES)
    n_sorted = sorted_scr.shape[2]

    @pl.when(step == 0)
    def _():
        zero_scr[...] = jnp.zeros(zero_scr.shape, F32)

        def fill(wait):
            for e in range(N_EXPERTS):
                def piece(j, e=e):
                    cp = _rows_copy(zero_scr, 0, buf_ref, fill_ref[0, e] + j * FILL_PIECE, FILL_PIECE, sems.at[0])
                    cp.wait() if wait else cp.start()
                _for_each_piece(fill_ref[0, N_EXPERTS + e], piece)

            def tail(j):
                cp = _rows_copy(zero_scr, 0, buf_ref, fill_ref[0, 2 * N_EXPERTS] + j * EXPERT_ROWS,
                                EXPERT_ROWS, sems.at[0])
                cp.wait() if wait else cp.start()
            _for_each_piece(fill_ref[0, 2 * N_EXPERTS + 1], tail)

        fill(False)
        fill(True)

    pos = lax.broadcasted_iota(I32, (n_sorted, ts), 0)
    hots, weights = [], []
    for t in tiles:
        cols = pl.ds(t * ts, ts)
        hot = None
        weight = None
        for kslot in range(TOP_K):
            eq = pos == route_ref[kslot:kslot + 1, cols]
            w = jnp.where(eq, gate_ref[kslot:kslot + 1, cols], 0.0)
            hot = eq if hot is None else (hot | eq)
            weight = w if weight is None else weight + w
        hots.append(hot)
        weights.append(weight)
    rows_sorted = [jnp.dot(hots[t].astype(BF16), h_ref[pl.ds(t * ts, ts), :].astype(BF16),
                           preferred_element_type=F32) for t in tiles]
    gate_sorted = [jnp.broadcast_to(jnp.sum(weights[t], axis=1, keepdims=True), (n_sorted, LANES)) for t in tiles]
    for t in tiles:
        sorted_scr[slot, t, :, 0:D_MODEL] = rows_sorted[t]
        sorted_scr[slot, t, :, D_MODEL:BUF_WIDTH] = gate_sorted[t]

    def copies(tab, which, wait):
        for t in tiles:
            _run_copies(tab, t, sorted_scr.at[which, t], True, buf_ref, sems.at[which], wait)

    @pl.when(step > 0)
    def _():
        copies(prev_tab_ref, 1 - slot, True)

    copies(tab_ref, slot, False)

    @pl.when(step == last)
    def _():
        copies(tab_ref, slot, True)


def _dispatch_call(tab_d, fill_tab, route8, gate8, h1, n_rows):
    T = h1.shape[0]
    ts = MOVE_TILES * PREP_ROWS
    smem_tile = pl.BlockSpec((MOVE_TILES, 1, LANES), lambda i: (i, 0, 0), memory_space=pltpu.SMEM)
    smem_prev = pl.BlockSpec((MOVE_TILES, 1, LANES), lambda i: (jnp.maximum(i - 1, 0), 0, 0),
                             memory_space=pltpu.SMEM)
    return pl.pallas_call(
        _dispatch_kernel, grid=(T // ts,),
        in_specs=[smem_tile, smem_prev,
                  pl.BlockSpec((1, LANES), lambda i: (0, 0), memory_space=pltpu.SMEM),
                  pl.BlockSpec((8, ts), lambda i: (0, i)),
                  pl.BlockSpec((8, ts), lambda i: (0, i)),
                  pl.BlockSpec((ts, D_MODEL), lambda i: (i, 0))],
        out_specs=pl.BlockSpec(memory_space=pl.ANY),
        out_shape=jax.ShapeDtypeStruct((n_rows, BUF_WIDTH), F32),
        scratch_shapes=[pltpu.VMEM((2, MOVE_TILES, _stage_rows(PREP_ROWS), BUF_WIDTH), F32),
                        pltpu.VMEM((EXPERT_ROWS, BUF_WIDTH), F32),
                        pltpu.SemaphoreType.DMA((2,))],
        compiler_params=pltpu.CompilerParams(dimension_semantics=("arbitrary",),
                                             vmem_limit_bytes=VMEM_LIMIT),
        name="dispatch",
    )(tab_d, tab_d, fill_tab, route8, gate8, h1)


def _expert_kernel(be_ref, nb_ref, run_ref, next_ref, valid_ref, x_ref, wgu_hbm, bgu_ref, wdn_hbm, bdn_ref, o_ref,
                   wgu_f32, wdn_f32, wgu_scr, wdn_scr, sems, *, layer):
    j = pl.program_id(0)
    used = j < nb_ref[0]
    changed = jnp.logical_or(j == 0, be_ref[j] != be_ref[jnp.maximum(j - 1, 0)])

    def fetch(expert, slot):
        return (pltpu.make_async_copy(wgu_hbm.at[layer, expert], wgu_f32.at[slot], sems.at[0, slot]),
                pltpu.make_async_copy(wdn_hbm.at[layer, expert], wdn_f32.at[slot], sems.at[1, slot]))

    @pl.when(j == 0)
    def _():
        for cp in fetch(be_ref[0], 0):
            cp.start()

    @pl.when(jnp.logical_and(used, changed))
    def _():
        slot = run_ref[j] % 2
        for cp in fetch(be_ref[j], slot):
            cp.wait()
        wgu_scr[...] = wgu_f32[slot].astype(BF16)
        wdn_scr[...] = wdn_f32[slot].astype(BF16)

        @pl.when(next_ref[j] >= 0)
        def _():
            for cp in fetch(next_ref[j], 1 - slot):
                cp.start()

    def ffn(n_rows):
        gu = (jnp.dot(x_ref[0:n_rows, 0:D_MODEL].astype(BF16), wgu_scr[...], preferred_element_type=F32)
              + bgu_ref[...])
        glu = jnp.minimum(gu[:, 0:D_FF], SWIGLU_LIMIT)
        lin = jnp.clip(gu[:, D_FF:], -SWIGLU_LIMIT, SWIGLU_LIMIT)
        act = glu * _sigmoid(SWIGLU_ALPHA * glu) * (lin + 1.0)
        y = jnp.dot(act.astype(BF16), wdn_scr[...], preferred_element_type=F32) + bdn_ref[...]
        gate = x_ref[0:n_rows, D_MODEL:BUF_WIDTH]
        return y * jnp.concatenate([gate] * (D_MODEL // LANES), axis=1)

    half = EXPERT_ROWS // 2
    upper_needed = valid_ref[j] > half

    @pl.when(jnp.logical_and(used, upper_needed))
    def _():
        o_ref[...] = ffn(EXPERT_ROWS)

    @pl.when(jnp.logical_and(used, jnp.logical_not(upper_needed)))
    def _():
        o_ref[0:half, :] = ffn(half)
        o_ref[half:, :] = jnp.zeros((EXPERT_ROWS - half, D_MODEL), F32)

    @pl.when(jnp.logical_not(used))
    def _():
        o_ref[...] = jnp.zeros(o_ref.shape, F32)


def _expert_call(block_e, n_used, data_end, buf, lw):
    n_rows = buf.shape[0]
    rows = EXPERT_ROWS
    n_blocks = n_rows // rows
    j = jnp.arange(n_blocks, dtype=I32)
    first = (j < n_used[0]) & ((j == 0) | (block_e != jnp.roll(block_e, 1)))
    run_idx = jnp.cumsum(first.astype(I32)) - 1
    later_first = lax.cummin(jnp.where(first, j, n_blocks)[::-1])[::-1]
    next_start = jnp.concatenate([later_first[1:], jnp.full((1,), n_blocks, I32)])
    next_e = jnp.where(next_start < n_blocks, block_e[jnp.minimum(next_start, n_blocks - 1)], -1).astype(I32)
    valid = jnp.clip(jnp.sum(jnp.where(block_e[:, None] == jnp.arange(N_EXPERTS, dtype=I32), data_end, 0), axis=1)
                     - j * rows, 0, rows).astype(I32)

    def blk(j, be, nb, *s):
        return jnp.minimum(j, nb[0] - 1)

    layer = lw['layer']
    grid_spec = pltpu.PrefetchScalarGridSpec(
        num_scalar_prefetch=5, grid=(n_blocks,),
        in_specs=[pl.BlockSpec((rows, BUF_WIDTH), lambda j, *s: (blk(j, *s), 0)),
                  pl.BlockSpec(memory_space=pl.ANY),
                  pl.BlockSpec((None, None, 1, 2 * D_FF), lambda j, be, *s: (layer, be[j], 0, 0)),
                  pl.BlockSpec(memory_space=pl.ANY),
                  pl.BlockSpec((None, None, 1, D_MODEL), lambda j, be, *s: (layer, be[j], 0, 0))],
        out_specs=pl.BlockSpec((rows, D_MODEL), lambda j, *s: (j, 0)),
        scratch_shapes=[pltpu.VMEM((2, D_MODEL, 2 * D_FF), F32), pltpu.VMEM((2, D_FF, D_MODEL), F32),
                        pltpu.VMEM((D_MODEL, 2 * D_FF), BF16), pltpu.VMEM((D_FF, D_MODEL), BF16),
                        pltpu.SemaphoreType.DMA((2, 2))])
    return pl.pallas_call(
        functools.partial(_expert_kernel, layer=layer), grid_spec=grid_spec,
        out_shape=jax.ShapeDtypeStruct((n_rows, D_MODEL), F32),
        compiler_params=pltpu.CompilerParams(dimension_semantics=("arbitrary",),
                                             vmem_limit_bytes=56 * 1024 * 1024),
        name="expert",
    )(block_e, n_used, run_idx, next_e, valid, buf, lw['w_gate_up'], lw['b_gate_up'], lw['w_down'], lw['b_down'])


def _combine_kernel(tab_ref, next_tab_ref, route_ref, h_ref, out_hbm_ref, ln2g_ref, ln2b_ref, o_ref,
                    stage_scr, sems):
    step = pl.program_id(0)
    last = pl.num_programs(0) - 1
    slot = step % 2
    ts = PREP_ROWS
    tiles = range(MOVE_TILES)
    n_stage = stage_scr.shape[2]

    def copies(tab, which, wait):
        for t in tiles:
            _run_copies(tab, t, stage_scr.at[which, t], False, out_hbm_ref, sems.at[which], wait)

    @pl.when(step == 0)
    def _():
        stage_scr[...] = jnp.zeros(stage_scr.shape, F32)
        copies(tab_ref, 0, False)

    @pl.when(step < last)
    def _():
        copies(next_tab_ref, 1 - slot, False)

    lane = lax.broadcasted_iota(I32, (ts, n_stage), 1).astype(F32)
    sels = []
    for t in tiles:
        slot_f = route_ref[:, pl.ds(t * ts, ts)].astype(F32)
        slot_cols = jnp.concatenate([slot_f, jnp.zeros((LANES - 8, ts), F32)], axis=0).T
        sel = None
        for kslot in range(TOP_K):
            eq = lane == slot_cols[:, kslot:kslot + 1]
            sel = eq if sel is None else (sel | eq)
        sels.append(sel.astype(BF16))
    copies(tab_ref, slot, True)
    ys = [jnp.dot(sels[t], stage_scr[slot, t].astype(BF16), preferred_element_type=F32) for t in tiles]
    for t in tiles:
        rows = pl.ds(t * ts, ts)
        o_ref[rows, :] = _layer_norm(DEEPNORM_ALPHA * h_ref[rows, :] + ys[t], ln2g_ref[...], ln2b_ref[...])


def _combine_call(tab_c, route8, h1, out_rows, lw):
    T = h1.shape[0]
    ts = MOVE_TILES * PREP_ROWS
    n_tiles = T // ts
    full = lambda a: pl.BlockSpec(a.shape, lambda i: (0,) * a.ndim)
    n_stage = _stage_rows(PREP_ROWS)
    return pl.pallas_call(
        _combine_kernel, grid=(n_tiles,),
        in_specs=[pl.BlockSpec((MOVE_TILES, 1, LANES), lambda i: (i, 0, 0), memory_space=pltpu.SMEM),
                  pl.BlockSpec((MOVE_TILES, 1, LANES), lambda i: (jnp.minimum(i + 1, n_tiles - 1), 0, 0),
                               memory_space=pltpu.SMEM),
                  pl.BlockSpec((8, ts), lambda i: (0, i)),
                  pl.BlockSpec((ts, D_MODEL), lambda i: (i, 0)),
                  pl.BlockSpec(memory_space=pl.ANY),
                  full(lw['ln2_g']), full(lw['ln2_b'])],
        out_specs=pl.BlockSpec((ts, D_MODEL), lambda i: (i, 0)),
        out_shape=jax.ShapeDtypeStruct((T, D_MODEL), F32),
        scratch_shapes=[pltpu.VMEM((2, MOVE_TILES, n_stage, D_MODEL), F32), pltpu.SemaphoreType.DMA((2,))],
        compiler_params=pltpu.CompilerParams(dimension_semantics=("arbitrary",),
                                             vmem_limit_bytes=VMEM_LIMIT),
        name="combine",
    )(tab_c, tab_c, route8, h1, out_rows, lw['ln2_g'], lw['ln2_b'])


def _block_diag_ones(width):
    hid = jnp.arange(width) // HEAD_DIM
    return (hid[:, None] == hid[None, :]).astype(BF16)


def _layer_params(l, w_in, pool_w, pool_scale, rwkv_mu, rwkv_w0, rwkv_w2, rwkv_a0, rwkv_a2, rwkv_g2,
                  rwkv_kk_scale, rwkv_ka, rwkv_rk, rwkv_lnx_g, rwkv_lnx_b, rwkv_v0, rwkv_v1, rwkv_v2,
                  mlstm_conv_w, mlstm_conv_b, mlstm_b_i, mlstm_b_f, mlstm_norm_g, w_out, ln1_g, ln1_b,
                  router_w, router_b, w_gate_up, b_gate_up, w_down, b_down, ln2_g, ln2_b):
    row = lambda a: a.reshape(1, -1).astype(F32)
    pad_cols = D_IN_PAD - D_IN
    if l > 0:
        extra = jnp.concatenate([rwkv_v1[l - 1], jnp.zeros((D_MODEL, pad_cols - VRES_LORA), F32)], axis=1)
        v0 = row(rwkv_v0[l - 1])
        v2 = jnp.zeros((LANES, RWKV_WIDTH), F32).at[VRES_OFF:VRES_OFF + VRES_LORA].set(rwkv_v2[l - 1])
    else:
        extra = jnp.zeros((D_MODEL, pad_cols), F32)
        v0 = jnp.zeros((1, RWKV_WIDTH), F32)
        v2 = jnp.zeros((LANES, RWKV_WIDTH), F32)
    pw = jnp.zeros((POOL_WIDTH, POOL_WIDTH), F32)
    for gi in range(len(POOL_WINDOWS)):
        sl = slice(gi * POOL_GROUP, (gi + 1) * POOL_GROUP)
        pw = pw.at[sl, sl].set(pool_w[l, gi])
    zero_lora = jnp.zeros((DECAY_LORA, RWKV_WIDTH), F32)
    gate_bias = jnp.zeros((1, LANES), F32).at[0, 0:MLSTM_HEADS].set(mlstm_b_i[l])
    gate_bias = gate_bias.at[0, MLSTM_HEADS:2 * MLSTM_HEADS].set(mlstm_b_f[l])
    return {
        'w_in': jnp.concatenate([w_in[l], extra], axis=1).astype(BF16),
        'pool_w': pw.astype(BF16), 'pool_scale': row(pool_scale[l]), 'mu': row(rwkv_mu[l]),
        'w0': row(rwkv_w0[l]), 'w2': jnp.concatenate([rwkv_w2[l], zero_lora], axis=0).astype(BF16),
        'a0': row(rwkv_a0[l]), 'a2': jnp.concatenate([zero_lora, rwkv_a2[l]], axis=0).astype(BF16),
        'g2': rwkv_g2[l].astype(BF16), 'kk_scale': row(rwkv_kk_scale[l]), 'ka': row(rwkv_ka[l]),
        'rk': row(rwkv_rk[l]), 'v0': v0, 'v2': v2.astype(BF16),
        'conv_w': mlstm_conv_w[l], 'conv_b': row(mlstm_conv_b[l]), 'gate_bias': gate_bias,
        'ones_bd': _block_diag_ones(RWKV_WIDTH),
        'lnx_g': row(rwkv_lnx_g[l]), 'lnx_b': row(rwkv_lnx_b[l]), 'norm_g': row(mlstm_norm_g[l]),
        'w_out': w_out[l].astype(BF16), 'ln1_g': row(ln1_g[l]), 'ln1_b': row(ln1_b[l]),
        'router_wt': router_w[l].T, 'router_b': router_b[l].reshape(N_EXPERTS, 1),
        'layer': l, 'w_gate_up': w_gate_up, 'b_gate_up': b_gate_up.reshape(-1, N_EXPERTS, 1, 2 * D_FF),
        'w_down': w_down, 'b_down': b_down.reshape(-1, N_EXPERTS, 1, D_MODEL),
        'ln2_g': row(ln2_g[l]), 'ln2_b': row(ln2_b[l]),
    }


def _layer(h, v_first, lw, *, has_vres):
    B, S, _ = h.shape
    T = B * S
    outs = _prep_call(h, lw, v_first, has_vres=has_vres)
    ypool, r, ld, k, v, kk, b, g, bonus, mq, mk, mv, mo, mg = outs
    yr = _rwkv_call(r, ld, k, v, kk, b)
    hm = _mlstm_call(mq, mk, mv, mg)
    flat = lambda a: a.reshape(T, a.shape[-1])
    h_flat = flat(h)
    h1, gate8, route8, cnt = _post_call(h_flat, flat(ypool), flat(yr), flat(bonus), flat(g), flat(hm),
                                        flat(mo), lw)
    n_tiles = T // PREP_ROWS
    cnt = cnt.reshape(n_tiles, N_EXPERTS, LANES)[:, :, 0].astype(I32)
    rows = EXPERT_ROWS
    run_len = (cnt + RUN_PIECE - 1) // RUN_PIECE * RUN_PIECE
    region = jnp.sum(run_len, axis=0)
    padded = (region + rows - 1) // rows * rows
    pad_end = jnp.cumsum(padded)
    pad_start = pad_end - padded
    n_blocks = -(-(T * TOP_K + n_tiles * N_EXPERTS * (RUN_PIECE - 1)) // rows) + N_EXPERTS
    n_used = (pad_end[-1] // rows).astype(I32).reshape(1)
    starts = jnp.minimum(jnp.arange(n_blocks, dtype=I32), n_used[0] - 1) * rows
    block_e = jnp.sum((pad_end[None, :] <= starts[:, None]).astype(I32), axis=1)
    run_src = jnp.cumsum(run_len, axis=1) - run_len
    run_dst = pad_start[None, :] + jnp.cumsum(run_len, axis=0) - run_len
    run_tab = jnp.concatenate([run_len, run_src, run_dst,
                               jnp.zeros((n_tiles, LANES - 3 * N_EXPERTS), I32)], axis=1).reshape(n_tiles, 1, LANES)
    fill_start = (pad_start + region) // FILL_PIECE * FILL_PIECE
    fill_tab = jnp.concatenate([fill_start, (pad_end - fill_start) // FILL_PIECE,
                                pad_end[-1:], n_blocks - n_used,
                                jnp.zeros((LANES - 2 * N_EXPERTS - 2,), I32)]).reshape(1, LANES)
    buf = _dispatch_call(run_tab, fill_tab, route8, gate8, h1, n_blocks * rows)
    out_rows = _expert_call(block_e, n_used, pad_start + region, buf, lw)
    h2 = _combine_call(run_tab, route8, h1, out_rows, lw)
    return h2.reshape(B, S, D_MODEL), v


def kernel(x, w_in, pool_w, pool_scale, rwkv_mu, rwkv_w0, rwkv_w2, rwkv_a0, rwkv_a2, rwkv_g2, rwkv_kk_scale, rwkv_ka, rwkv_rk, rwkv_lnx_g, rwkv_lnx_b, rwkv_v0, rwkv_v1, rwkv_v2, mlstm_conv_w, mlstm_conv_b, mlstm_b_i, mlstm_b_f, mlstm_norm_g, w_out, ln1_g, ln1_b, router_w, router_b, w_gate_up, b_gate_up, w_down, b_down, ln2_g, ln2_b):
    weights = (w_in, pool_w, pool_scale, rwkv_mu, rwkv_w0, rwkv_w2, rwkv_a0, rwkv_a2, rwkv_g2, rwkv_kk_scale,
               rwkv_ka, rwkv_rk, rwkv_lnx_g, rwkv_lnx_b, rwkv_v0, rwkv_v1, rwkv_v2, mlstm_conv_w, mlstm_conv_b,
               mlstm_b_i, mlstm_b_f, mlstm_norm_g, w_out, ln1_g, ln1_b, router_w, router_b, w_gate_up,
               b_gate_up, w_down, b_down, ln2_g, ln2_b)
    h = x
    v_first = jnp.zeros(x.shape[:2] + (RWKV_WIDTH,), F32)
    for l in range(w_in.shape[0]):
        lw = _layer_params(l, *weights)
        h, v_l = _layer(h, v_first, lw, has_vres=l > 0)
        if l == 0:
            v_first = v_l
    return h
```
